```python
import math
import jax, jax.numpy as jnp
from jax import lax
import numpy as np

D_MODEL = 2048
BATCH = 32
SEQ = 256
DEPTH = 4
DEC_BATCH = 4
DEC_SEQ = 2048
PAST_LEN = 256

GRID_W = 64
WIN_R = 8
WIN_C = 16
N_HEADS_A = 16
HEAD_DIM_A = 64
W_A = N_HEADS_A * HEAD_DIM_A
Q_BLOCK = 128
N_HEADS_R = 16
HEAD_DIM_R = 64
W_R = N_HEADS_R * HEAD_DIM_R
LORA_W = 64
LORA_A = 64
LORA_G = 128
GN_EPS = 64e-5
W_C = 1024
POS_BANDS = 16
POS_EMB = 1 + 2 * POS_BANDS
FILTER_HIDDEN = 64
HY_TARGET = 1e-2
HY_FAST = 0.3
HY_SLOW = 1.5
D_FF = 4 * D_MODEL
N_MOD = 6
N_IN = 3 * W_A + 3 * W_R + 3 * W_C + 3 * D_MODEL
NORM_EPS = 1e-6
NEG_INF = -1e30

kernel_name = 'hybrid_natten_rwkv7_hyena_flow_step'


def rms_norm(x, g):
    xf = x.astype(jnp.float32)
    y = xf * lax.rsqrt(jnp.mean(xf * xf, -1, keepdims=True) + NORM_EPS)
    return (y * g.astype(jnp.float32)).astype(x.dtype)


def ada_modulation(cvec, w_mod, b_mod):
    m = jax.nn.silu(cvec) @ w_mod + b_mod
    return jnp.split(m[:, None, :], N_MOD, axis=-1)


def short_conv3(x, w, b):
    xp = jnp.pad(x, ((0, 0), (1, 1), (0, 0)))
    return xp[:, :-2] * w[0] + xp[:, 1:-1] * w[1] + xp[:, 2:] * w[2] + b


def attn_context(q, k, v):
    B, S, H, dh = q.shape
    nb = S // Q_BLOCK
    qb = jnp.moveaxis(q.reshape(B, nb, Q_BLOCK, H, dh), 1, 0)
    scale = HEAD_DIM_A ** -0.5

    def block(qi):
        s = jnp.einsum('bqhd,bkhd->bhqk', qi, k).astype(jnp.float32) * scale
        p = jax.nn.softmax(s, axis=-1).astype(v.dtype)
        return jnp.einsum('bhqk,bkhd->bqhd', p, v)

    o = lax.map(block, qb)
    return jnp.moveaxis(o, 0, 1).reshape(B, S, H * dh)


def na_indices(rows):
    wr = min(WIN_R, rows)
    r = np.arange(rows)
    r0 = np.clip(r - wr // 2, 0, rows - wr)
    key_rows = r0[:, None] + np.arange(wr)[None, :]
    dr = key_rows - r[:, None] + (WIN_R - 1)
    cq = np.arange(GRID_W)
    c0 = np.clip(cq - WIN_C // 2, 0, GRID_W - WIN_C)
    ck = np.arange(GRID_W)
    col_ok = (ck[None, :] >= c0[:, None]) & (ck[None, :] < c0[:, None] + WIN_C)
    dc = np.clip(ck[None, :] - cq[:, None], -(WIN_C - 1), WIN_C - 1) + (WIN_C - 1)
    mask = np.broadcast_to(col_ok[:, None, :], (GRID_W, wr, GRID_W)).reshape(GRID_W, wr * GRID_W)
    return wr, key_rows, dr, dc, mask


def attn_latent(q, k, v, k_ctx, v_ctx, rpb):
    B, L, H, dh = q.shape
    rows = L // GRID_W
    wr, key_rows, dr, dc, mask = na_indices(rows)
    nl = wr * GRID_W
    qg = q.reshape(B, rows, GRID_W, H, dh)
    kg = k.reshape(B, rows, GRID_W, H, dh)[:, key_rows].reshape(B, rows, nl, H, dh)
    vg = v.reshape(B, rows, GRID_W, H, dh)[:, key_rows].reshape(B, rows, nl, H, dh)
    scale = HEAD_DIM_A ** -0.5
    s_loc = jnp.einsum('brqhd,brkhd->bhrqk', qg, kg).astype(jnp.float32) * scale
    bias = rpb[:, dr[:, None, :, None], dc[None, :, None, :]]
    bias = bias.reshape(H, rows, GRID_W, nl).astype(jnp.float32)
    s_loc = jnp.where(mask, s_loc + bias[None], NEG_INF)
    s_ctx = jnp.einsum('brqhd,bkhd->bhrqk', qg, k_ctx).astype(jnp.float32) * scale
    p = jax.nn.softmax(jnp.concatenate([s_loc, s_ctx], -1), axis=-1).astype(v.dtype)
    o = (jnp.einsum('bhrqk,brkhd->brqhd', p[..., :nl], vg)
         + jnp.einsum('bhrqk,bkhd->brqhd', p[..., nl:], v_ctx))
    return o.reshape(B, L, H * dh)


def wkv_step(S, inp):
    r_t, w_t, k_t, v_t, kk_t, b_t = inp
    sa = jnp.einsum('behvk,behk->behv', S, -kk_t)
    S = S * w_t[..., None, :] + sa[..., None] * b_t[..., None, :] + v_t[..., None] * k_t[..., None, :]
    y = jnp.einsum('behvk,behk->behv', S, r_t)
    return S, y


def rwkv_branch(h, rkv, S0, lp):
    B, L, _ = h.shape
    f32 = jnp.float32
    heads = lambda t: t.reshape(t.shape[:-1] + (N_HEADS_R, HEAD_DIM_R))
    rkv = short_conv3(rkv, lp['wkv_conv_w'], lp['wkv_conv_b']).astype(f32)
    r, k, v = [heads(t) for t in jnp.split(rkv, 3, -1)]
    kk = k * heads(lp['wkv_k_k'].astype(f32))
    kk = kk * lax.rsqrt(jnp.sum(kk * kk, -1, keepdims=True) + 1e-12)
    lw = jnp.tanh(jnp.einsum('bld,edr->bler', h, lp['wkv_w1']).astype(f32))
    w_log = lp['wkv_w0'].astype(f32) + jnp.einsum('bler,erc->blec', lw, lp['wkv_w2'].astype(f32))
    decay = heads(jnp.exp(-jnp.exp(-jax.nn.softplus(-w_log) - 0.5)))
    la = jnp.einsum('bld,edr->bler', h, lp['wkv_a1']).astype(f32)
    a = heads(jax.nn.sigmoid(lp['wkv_a0'].astype(f32)
                             + jnp.einsum('bler,erc->blec', la, lp['wkv_a2'].astype(f32))))
    k_dir = k[:, :, None] * (1.0 + (a - 1.0) * heads(lp['wkv_k_a'].astype(f32)))
    kk_dir = jnp.broadcast_to(kk[:, :, None], a.shape)
    r_dir = jnp.broadcast_to(r[:, :, None], a.shape)
    v_dir = jnp.broadcast_to(v[:, :, None], a.shape)

    def orient(t):
        return jnp.stack([t[:, :, 0], jnp.flip(t[:, :, 1], 1)], 2)

    xs = tuple(jnp.moveaxis(orient(t), 1, 0)
               for t in (r_dir, decay, k_dir, v_dir, kk_dir, kk_dir * a))
    S_fin, ys = lax.scan(wkv_step, S0.astype(f32), xs)
    ys = jnp.moveaxis(ys, 0, 1)
    y = ys[:, :, 0] + jnp.flip(ys[:, :, 1], 1)
    mu = jnp.mean(y, -1, keepdims=True)
    var = jnp.mean(jnp.square(y - mu), -1, keepdims=True)
    yn = ((y - mu) * lax.rsqrt(var + GN_EPS)).reshape(B, L, W_R)
    yn = yn * lp['wkv_gn_g'].astype(f32) + lp['wkv_gn_b'].astype(f32)
    bonus = (jnp.sum(r * k * lp['wkv_r_k'].astype(f32), -1, keepdims=True) * v).reshape(B, L, W_R)
    g = (jax.nn.sigmoid(h @ lp['wkv_g1']) @ lp['wkv_g2']).astype(f32)
    return ((yn + bonus) * g).astype(h.dtype), S_fin


def hyena_pos_features(L):
    t = np.linspace(0.0, 1.0, L, dtype=np.float32)[:, None]
    w = 2.0 * np.pi * np.arange(L, dtype=np.float32)[:, None] / L
    f = np.linspace(1e-4, POS_BANDS - 1, POS_BANDS, dtype=np.float32)[None, :]
    z = np.concatenate([t, np.cos(f * w), -np.sin(f * w)], -1).astype(np.float32)
    dist = (np.abs(np.arange(L) - L // 2).astype(np.float32) / L)[:, None]
    deltas = np.abs(np.linspace(math.log(HY_TARGET) / HY_SLOW, math.log(HY_TARGET) / HY_FAST,
                                W_C, dtype=np.float32))[None, :]
    window = np.exp(-dist * deltas).astype(np.float32)
    return z, window


def hyena_branch(u, lp):
    B, L, _ = u.shape
    f32 = jnp.float32
    u = short_conv3(u, lp['hy_conv_w'], lp['hy_conv_b'])
    x0, x1, vv = jnp.split(u, 3, -1)
    z_pos, window = hyena_pos_features(L)
    freq = lp['hy_freq'].astype(f32)
    t = jnp.sin(freq * (z_pos @ lp['hy_f1'].astype(f32) + lp['hy_fb1'].astype(f32)))
    t = jnp.sin(freq * (t @ lp['hy_f2'].astype(f32) + lp['hy_fb2'].astype(f32)))
    filt = (t @ lp['hy_f3'].astype(f32)) * window
    filt = filt / (jnp.sum(jnp.abs(filt), 0, keepdims=True) + 1e-6)
    z = (vv * x1).astype(f32)
    n = 2 * L
    y = jnp.fft.irfft(jnp.fft.rfft(z, n=n, axis=1) * jnp.fft.rfft(filt, n=n, axis=0)[None],
                      n=n, axis=1)[:, L // 2: L // 2 + L]
    y = y + z * lp['hy_d'].astype(f32)
    return (x0.astype(f32) * y).astype(u.dtype)


def trunk_layer(x, cvec, lp, ctx_k=None, ctx_v=None, S0=None):
    B, L, _ = x.shape
    sh1, sc1, ga1, sh2, sc2, ga2 = ada_modulation(cvec, lp['w_mod'], lp['b_mod'])
    h = rms_norm(x, lp['ln1']) * (1.0 + sc1) + sh1
    proj = h @ lp['w_in']
    s1 = 3 * W_A
    s2 = s1 + 3 * W_R
    s3 = s2 + 3 * W_C
    qkv, rkv, hyu, gl = jnp.split(proj, [s1, s2, s3], -1)
    q, k, v = [t.reshape(B, L, N_HEADS_A, HEAD_DIM_A) for t in jnp.split(qkv, 3, -1)]
    if ctx_k is None:
        o_a = attn_context(q, k, v)
        S0 = jnp.zeros((B, 2, N_HEADS_R, HEAD_DIM_R, HEAD_DIM_R), jnp.float32)
    else:
        o_a = attn_latent(q, k, v, ctx_k, ctx_v, lp['rpb'])
    o_r, S_fin = rwkv_branch(h, rkv, S0, lp)
    o_c = hyena_branch(hyu, lp)
    g_a, g_r, g_c = jnp.split(jax.nn.sigmoid(gl), 3, -1)
    merged = g_a * (o_a @ lp['w_pa']) + g_r * (o_r @ lp['w_pr']) + g_c * (o_c @ lp['w_pc'])
    x = x + ga1 * (merged @ lp['w_out'])
    h2 = rms_norm(x, lp['ln2']) * (1.0 + sc2) + sh2
    f = jnp.square(jax.nn.relu(h2 @ lp['w_ff1'] + lp['b_ff1'])) @ lp['w_ff2'] + lp['b_ff2']
    x = x + ga2 * f
    return x, k, v, S_fin.astype(x.dtype)


def setup_inputs(seed: int = 0) -> dict:
    key = jax.random.key(seed)
    keys = iter(jax.random.split(key, 96))

    def nrm(shape, scale):
        return scale * jax.random.normal(next(keys), shape, jnp.float32)

    D = D_MODEL
    inp = {}
    inp['x_prompt'] = nrm((BATCH, SEQ, D), 1.0)
    inp['x_sample'] = nrm((DEC_BATCH, DEC_SEQ, D), 1.0)
    inp['cache_k'] = nrm((DEC_BATCH, DEPTH, PAST_LEN, N_HEADS_A, HEAD_DIM_A), 1.0)
    inp['cache_v'] = nrm((DEC_BATCH, DEPTH, PAST_LEN, N_HEADS_A, HEAD_DIM_A), 1.0)
    inp['state_wkv'] = nrm((DEC_BATCH, DEPTH, 2, N_HEADS_R, HEAD_DIM_R, HEAD_DIM_R), 0.5)
    inp['c'] = nrm((DEC_BATCH, D), 1.0)
    inp['c_ctx'] = nrm((D,), 1.0)
    inp['ln1_g'] = 1.0 + nrm((DEPTH, D), 0.01)
    inp['ln2_g'] = 1.0 + nrm((DEPTH, D), 0.01)
    inp['w_mod'] = nrm((DEPTH, D, N_MOD * D), 0.5 * D ** -0.5)
    inp['b_mod'] = nrm((DEPTH, N_MOD * D), 0.01)
    inp['w_in'] = nrm((DEPTH, D, N_IN), D ** -0.5)
    inp['rpb'] = nrm((DEPTH, N_HEADS_A, 2 * WIN_R - 1, 2 * WIN_C - 1), 0.1)
    inp['wkv_conv_w'] = nrm((DEPTH, 3, 3 * W_R), 0.5)
    inp['wkv_conv_b'] = nrm((DEPTH, 3 * W_R), 0.01)
    inp['wkv_w0'] = -1.0 + nrm((DEPTH, 2, W_R), 0.5)
    inp['wkv_w1'] = nrm((DEPTH, 2, D, LORA_W), D ** -0.5)
    inp['wkv_w2'] = nrm((DEPTH, 2, LORA_W, W_R), 0.1 * LORA_W ** -0.5)
    inp['wkv_a0'] = nrm((DEPTH, 2, W_R), 0.1)
    inp['wkv_a1'] = nrm((DEPTH, 2, D, LORA_A), D ** -0.5)
    inp['wkv_a2'] = nrm((DEPTH, 2, LORA_A, W_R), 0.1 * LORA_A ** -0.5)
    inp['wkv_g1'] = nrm((DEPTH, D, LORA_G), D ** -0.5)
    inp['wkv_g2'] = nrm((DEPTH, LORA_G, W_R), LORA_G ** -0.5)
    inp['wkv_k_k'] = 1.0 + nrm((DEPTH, W_R), 0.1)
    inp['wkv_k_a'] = 1.0 + nrm((DEPTH, W_R), 0.1)
    inp['wkv_r_k'] = nrm((DEPTH, N_HEADS_R, HEAD_DIM_R), 0.1)
    inp['wkv_gn_g'] = 1.0 + nrm((DEPTH, W_R), 0.01)
    inp['wkv_gn_b'] = nrm((DEPTH, W_R), 0.01)
    inp['hy_conv_w'] = nrm((DEPTH, 3, 3 * W_C), 0.5)
    inp['hy_conv_b'] = nrm((DEPTH, 3 * W_C), 0.01)
    inp['hy_f1'] = nrm((DEPTH, POS_EMB, FILTER_HIDDEN), POS_EMB ** -0.5)
    inp['hy_fb1'] = nrm((DEPTH, FILTER_HIDDEN), 0.1)
    inp['hy_f2'] = nrm((DEPTH, FILTER_HIDDEN, FILTER_HIDDEN), FILTER_HIDDEN ** -0.5)
    inp['hy_fb2'] = nrm((DEPTH, FILTER_HIDDEN), 0.1)
    inp['hy_freq'] = 1.0 + nrm((DEPTH, FILTER_HIDDEN), 0.1)
    inp['hy_f3'] = nrm((DEPTH, FILTER_HIDDEN, W_C), FILTER_HIDDEN ** -0.5)
    inp['hy_d'] = nrm((DEPTH, W_C), 1.0)
    inp['w_pa'] = nrm((DEPTH, W_A, D), W_A ** -0.5)
    inp['w_pr'] = nrm((DEPTH, W_R, D), W_R ** -0.5)
    inp['w_pc'] = nrm((DEPTH, W_C, D), W_C ** -0.5)
    inp['w_out'] = nrm((DEPTH, D, D), D ** -0.5)
    inp['w_ff1'] = nrm((DEPTH, D, D_FF), D ** -0.5)
    inp['b_ff1'] = nrm((DEPTH, D_FF), 0.01)
    inp['w_ff2'] = nrm((DEPTH, D_FF, D), D_FF ** -0.5)
    inp['b_ff2'] = nrm((DEPTH, D), 0.01)
    inp['final_g'] = 1.0 + nrm((D,), 0.01)
    return inp


def reference(x_prompt, x_sample, cache_k, cache_v, state_wkv, c, c_ctx,
              ln1_g, ln2_g, w_mod, b_mod, w_in, rpb,
              wkv_conv_w, wkv_conv_b, wkv_w0, wkv_w1, wkv_w2, wkv_a0, wkv_a1, wkv_a2,
              wkv_g1, wkv_g2, wkv_k_k, wkv_k_a, wkv_r_k, wkv_gn_g, wkv_gn_b,
              hy_conv_w, hy_conv_b, hy_f1, hy_fb1, hy_f2, hy_fb2, hy_freq, hy_f3, hy_d,
              w_pa, w_pr, w_pc, w_out, w_ff1, b_ff1, w_ff2, b_ff2, final_g):
    stacked = {
        'ln1': ln1_g, 'ln2': ln2_g, 'w_mod': w_mod, 'b_mod': b_mod, 'w_in': w_in, 'rpb': rpb,
        'wkv_conv_w': wkv_conv_w, 'wkv_conv_b': wkv_conv_b, 'wkv_w0': wkv_w0, 'wkv_w1': wkv_w1,
        'wkv_w2': wkv_w2, 'wkv_a0': wkv_a0, 'wkv_a1': wkv_a1, 'wkv_a2': wkv_a2,
        'wkv_g1': wkv_g1, 'wkv_g2': wkv_g2, 'wkv_k_k': wkv_k_k, 'wkv_k_a': wkv_k_a,
        'wkv_r_k': wkv_r_k, 'wkv_gn_g': wkv_gn_g, 'wkv_gn_b': wkv_gn_b,
        'hy_conv_w': hy_conv_w, 'hy_conv_b': hy_conv_b, 'hy_f1': hy_f1, 'hy_fb1': hy_fb1,
        'hy_f2': hy_f2, 'hy_fb2': hy_fb2, 'hy_freq': hy_freq, 'hy_f3': hy_f3, 'hy_d': hy_d,
        'w_pa': w_pa, 'w_pr': w_pr, 'w_pc': w_pc, 'w_out': w_out,
        'w_ff1': w_ff1, 'b_ff1': b_ff1, 'w_ff2': w_ff2, 'b_ff2': b_ff2,
    }
    yp = x_prompt
    ys = x_sample
    c_context = c_ctx[None, :]
    ks, vs, ss = [], [], []
    for l in range(DEPTH):
        lp = {name: arr[l] for name, arr in stacked.items()}
        yp, k_l, v_l, s_l = trunk_layer(yp, c_context, lp)
        ks.append(k_l)
        vs.append(v_l)
        ss.append(s_l)
        ys = trunk_layer(ys, c, lp, cache_k[:, l], cache_v[:, l], state_wkv[:, l])[0]
    y_prompt = rms_norm(yp, final_g)
    y_sample = rms_norm(ys, final_g)
    new_cache_k = jnp.stack(ks, 1)
    new_cache_v = jnp.stack(vs, 1)
    new_state_wkv = jnp.stack(ss, 1)
    return (y_prompt, y_sample, new_cache_k, new_cache_v, new_state_wkv)
```

```python
import functools
import math

import numpy as np
import jax
import jax.numpy as jnp
from jax import lax
from jax.experimental import pallas as pl
from jax.experimental.pallas import tpu as pltpu

F32 = jnp.float32
BF16 = jnp.bfloat16

D = 2048
DEPTH = 4
B_CTX, L_CTX = 32, 256
B_LAT, L_LAT = 4, 2048
N_CTX = B_CTX * L_CTX
N_LAT = B_LAT * L_LAT
N_TOK = N_CTX + N_LAT
H = 16
DH = 64
W = H * DH
GRID_W = 64
ROWS = L_LAT // GRID_W
WIN_R, WIN_C = 8, 16
NL = WIN_R * GRID_W
N_IN = 9 * W + 3 * D
LORA = 64
LORA_G = 128
N_LORA = 512
N_EXT = N_IN + N_LORA
GL_OFF = 9 * W
D_FF = 4 * D
N_MOD = 6
NORM_EPS = 1e-6
GN_EPS = 64e-5
NEG_INF = -1e30
POS_BANDS = 16
FILTER_HIDDEN = 64
FH_PAD = 128
LANES = 128
VMEM_LIMIT = 56 * 1024 * 1024


def _params(sem):
    return pltpu.CompilerParams(dimension_semantics=sem, vmem_limit_bytes=VMEM_LIMIT)


def _call(kernel, lidx, args, *, grid, in_specs, out_specs, out_shape, scratch=(), sem, name):
    gs = pltpu.PrefetchScalarGridSpec(num_scalar_prefetch=1, grid=grid, in_specs=in_specs,
                                      out_specs=out_specs, scratch_shapes=list(scratch))
    return pl.pallas_call(kernel, grid_spec=gs, out_shape=out_shape,
                          compiler_params=_params(sem), name=name)(lidx, *args)


def _mod_row(i, tm):
    start = i * tm
    return jnp.where(start < N_CTX, 0, 1 + (start - N_CTX) // L_LAT)


def _modnorm(x, g, sc, sh):
    y = x * lax.rsqrt(jnp.mean(x * x, -1, keepdims=True) + NORM_EPS)
    return (y * g) * (1.0 + sc) + sh


def _softplus(x):
    return jnp.maximum(x, 0.0) + jnp.log1p(jnp.exp(-jnp.abs(x)))


def _dot(a, b):
    return jnp.dot(a, b, preferred_element_type=F32)


def _dot_nt(a, b):
    return lax.dot_general(a, b, (((1,), (1,)), ((), ())), preferred_element_type=F32)


def _split(x):
    hi = x.astype(BF16)
    lo = (x - hi.astype(F32)).astype(BF16)
    return hi, lo


def _dot3(ah, al, bh, bl):
    return _dot(ah, bh) + (_dot(al, bh) + _dot(ah, bl))


def _mod_kernel(c_ref, w_ref, b_ref, o_ref):
    c = c_ref[...]
    s = c * jax.nn.sigmoid(c)
    o_ref[...] = _dot(s.astype(BF16), w_ref[...].astype(BF16)) + b_ref[...]


def _modulation(cvec8, w_mod, b_mod):
    tn = 1024
    return pl.pallas_call(
        _mod_kernel,
        grid=(DEPTH, N_MOD * D // tn),
        in_specs=[pl.BlockSpec((8, D), lambda l, j: (0, 0)),
                  pl.BlockSpec((None, D, tn), lambda l, j: (l, 0, j)),
                  pl.BlockSpec((None, 1, tn), lambda l, j: (l, 0, j))],
        out_specs=pl.BlockSpec((None, 8, tn), lambda l, j: (l, 0, j)),
        out_shape=jax.ShapeDtypeStruct((DEPTH, 8, N_MOD * D), F32),
        compiler_params=_params(("parallel", "parallel")), name="modulation",
    )(cvec8, w_mod, b_mod.reshape(DEPTH, 1, N_MOD * D))


def _in_proj_kernel(l_ref, x_ref, g_ref, sh_ref, sc_ref, w_ref, o_ref, h_ref, *, tm):
    @pl.when(pl.program_id(1) == 0)
    def _():
        row = _mod_row(pl.program_id(0), tm)
        h = _modnorm(x_ref[...], g_ref[...], sc_ref[pl.ds(row, 1), :], sh_ref[pl.ds(row, 1), :])
        h_ref[...] = h.astype(BF16)

    o_ref[...] = _dot(h_ref[...], w_ref[...])


def _in_proj(lidx, x, ln_g, mod, w_ext):
    tm, tn = 1024, 512
    return _call(
        functools.partial(_in_proj_kernel, tm=tm), lidx, (x, ln_g, mod, mod, w_ext),
        grid=(N_TOK // tm, N_EXT // tn),
        in_specs=[pl.BlockSpec((tm, D), lambda i, j, l: (i, 0)),
                  pl.BlockSpec((None, 1, D), lambda i, j, l: (l[0], 0, 0)),
                  pl.BlockSpec((None, 8, D), lambda i, j, l: (l[0], 0, 0)),
                  pl.BlockSpec((None, 8, D), lambda i, j, l: (l[0], 0, 1)),
                  pl.BlockSpec((None, D, tn), lambda i, j, l: (l[0], 0, j))],
        out_specs=pl.BlockSpec((tm, tn), lambda i, j, l: (i, j)),
        out_shape=jax.ShapeDtypeStruct((N_TOK, N_EXT), F32),
        scratch=[pltpu.VMEM((tm, D), BF16)],
        sem=("parallel", "arbitrary"), name="in_proj")


def _attn_ctx_kernel(l_ref, q_ref, k_ref, v_ref, o_ref):
    scale = DH ** -0.5
    outs = []
    for hh in range(2):
        sl = slice(hh * DH, (hh + 1) * DH)
        q = q_ref[:, sl].astype(BF16)
        k = k_ref[:, sl].astype(BF16)
        v = v_ref[:, sl].astype(BF16)
        s = _dot_nt(q, k) * scale
        p = jnp.exp(s - jnp.max(s, -1, keepdims=True))
        den = jnp.sum(p, -1, keepdims=True)
        outs.append(_dot(p.astype(BF16), v) / den)
    o_ref[...] = jnp.concatenate(outs, -1).astype(BF16)


def _attn_ctx(lidx, proj):
    nq = W // LANES
    return _call(
        _attn_ctx_kernel, lidx, (proj, proj, proj),
        grid=(B_CTX, nq),
        in_specs=[pl.BlockSpec((L_CTX, LANES), lambda b, p, l: (b, p)),
                  pl.BlockSpec((L_CTX, LANES), lambda b, p, l: (b, nq + p)),
                  pl.BlockSpec((L_CTX, LANES), lambda b, p, l: (b, 2 * nq + p))],
        out_specs=pl.BlockSpec((L_CTX, LANES), lambda b, p, l: (b, p)),
        out_shape=jax.ShapeDtypeStruct((N_CTX, W), BF16),
        sem=("parallel", "parallel"), name="attn_ctx")


def _attn_lat_kernel(l_ref, q_ref, k_ref, v_ref, kc_ref, vc_ref, bias_ref, o_ref):
    scale = DH ** -0.5

    def row_block(r, carry):
        r0 = jnp.clip(r - WIN_R // 2, 0, ROWS - WIN_R)
        d = r - r0
        qrow = pl.multiple_of(r * GRID_W, GRID_W)
        krow = pl.multiple_of(r0 * GRID_W, GRID_W)
        for hh in range(2):
            sl = slice(hh * DH, (hh + 1) * DH)
            q = q_ref[pl.ds(qrow, GRID_W), sl].astype(BF16)
            kl = k_ref[pl.ds(krow, NL), sl].astype(BF16)
            vl = v_ref[pl.ds(krow, NL), sl].astype(BF16)
            kc = kc_ref[:, sl].astype(BF16)
            vc = vc_ref[:, sl].astype(BF16)
            s_loc = _dot_nt(q, kl) * scale + bias_ref[hh, d]
            s_ctx = _dot_nt(q, kc) * scale
            m = jnp.maximum(jnp.max(s_loc, -1, keepdims=True), jnp.max(s_ctx, -1, keepdims=True))
            p_loc = jnp.exp(s_loc - m)
            p_ctx = jnp.exp(s_ctx - m)
            den = jnp.sum(p_loc, -1, keepdims=True) + jnp.sum(p_ctx, -1, keepdims=True)
            o = (_dot(p_loc.astype(BF16), vl) + _dot(p_ctx.astype(BF16), vc)) / den
            o_ref[pl.ds(qrow, GRID_W), sl] = o.astype(BF16)
        return carry

    lax.fori_loop(0, ROWS, row_block, 0)


def _attn_lat(lidx, proj, cache_k, cache_v, bias_tab):
    nq = W // LANES
    rb = N_CTX // L_LAT
    return _call(
        _attn_lat_kernel, lidx, (proj, proj, proj, cache_k, cache_v, bias_tab),
        grid=(B_LAT, nq),
        in_specs=[pl.BlockSpec((L_LAT, LANES), lambda b, p, l: (rb + b, p)),
                  pl.BlockSpec((L_LAT, LANES), lambda b, p, l: (rb + b, nq + p)),
                  pl.BlockSpec((L_LAT, LANES), lambda b, p, l: (rb + b, 2 * nq + p)),
                  pl.BlockSpec((None, None, L_CTX, LANES), lambda b, p, l: (b, l[0], 0, p)),
                  pl.BlockSpec((None, None, L_CTX, LANES), lambda b, p, l: (b, l[0], 0, p)),
                  pl.BlockSpec((None, 2, WIN_R, GRID_W, NL), lambda b, p, l: (l[0], p, 0, 0, 0))],
        out_specs=pl.BlockSpec((L_LAT, LANES), lambda b, p, l: (b, p)),
        out_shape=jax.ShapeDtypeStruct((N_LAT, W), BF16),
        sem=("parallel", "parallel"), name="attn_lat")


def _bias_table(rpb):
    dd = np.arange(WIN_R)[:, None]
    jj = np.arange(WIN_R)[None, :]
    dr = jj - dd + (WIN_R - 1)
    cq = np.arange(GRID_W)
    c0 = np.clip(cq - WIN_C // 2, 0, GRID_W - WIN_C)
    ck = np.arange(GRID_W)
    col_ok = (ck[None, :] >= c0[:, None]) & (ck[None, :] < c0[:, None] + WIN_C)
    dc = np.clip(ck[None, :] - cq[:, None], -(WIN_C - 1), WIN_C - 1) + (WIN_C - 1)
    tab = rpb[:, :, dr[:, None, :, None], dc[None, :, None, :]]
    tab = jnp.where(col_ok[None, None, None, :, None, :], tab, NEG_INF)
    return tab.reshape(DEPTH, H, WIN_R, GRID_W, NL).astype(F32)


def _conv3(x, w_ref, b_ref):
    n = x.shape[0]
    row = lax.broadcasted_iota(jnp.int32, x.shape, 0)
    prev = jnp.where(row == 0, 0.0, pltpu.roll(x, 1, 0))
    nxt = jnp.where(row == n - 1, 0.0, pltpu.roll(x, n - 1, 0))
    return prev * w_ref[0:1, :] + x * w_ref[1:2, :] + nxt * w_ref[2:3, :] + b_ref[...]


def _wkv_prep_kernel(l_ref, r_ref, k_ref, v_ref, lora_ref, cwr_ref, cwk_ref, cwv_ref,
                     cbr_ref, cbk_ref, cbv_ref, w0_ref, w2_ref, a0_ref, a2_ref, g2_ref,
                     ro_ref, ko_ref, vo_ref, d0_ref, d1_ref, a0o_ref, a1o_ref, go_ref):
    ro_ref[...] = _conv3(r_ref[...], cwr_ref, cbr_ref)
    ko_ref[...] = _conv3(k_ref[...], cwk_ref, cbk_ref)
    vo_ref[...] = _conv3(v_ref[...], cwv_ref, cbv_ref)
    lora = lora_ref[...]
    for e, (d_ref, ao_ref) in enumerate(((d0_ref, a0o_ref), (d1_ref, a1o_ref))):
        lw = jnp.tanh(lora[:, e * LORA:(e + 1) * LORA]).astype(BF16)
        la = lora[:, (2 + e) * LORA:(3 + e) * LORA].astype(BF16)
        w_log = w0_ref[e:e + 1, :] + _dot(lw, w2_ref[e].astype(BF16))
        d_ref[...] = jnp.exp(-jnp.exp(-_softplus(-w_log) - 0.5))
        ao_ref[...] = jax.nn.sigmoid(a0_ref[e:e + 1, :] + _dot(la, a2_ref[e].astype(BF16)))
    lg = jax.nn.sigmoid(lora[:, 4 * LORA:4 * LORA + LORA_G]).astype(BF16)
    go_ref[...] = _dot(lg, g2_ref[...].astype(BF16))


def _wkv_prep(lidx, proj, p, *, nb, seq, rb):
    ct = 256 if seq <= 256 else 128
    nc = W // ct
    off = 3 * W // ct
    n = nb * seq

    def col(c):
        return pl.BlockSpec((seq, ct), lambda b, j, l: (rb + b, off + c * nc + j))

    def cw(c):
        return pl.BlockSpec((None, 3, ct), lambda b, j, l: (l[0], 0, c * nc + j))

    def cb(c):
        return pl.BlockSpec((None, 1, ct), lambda b, j, l: (l[0], 0, c * nc + j))

    out = pl.BlockSpec((seq, ct), lambda b, j, l: (b, j))
    return _call(
        _wkv_prep_kernel, lidx,
        (proj, proj, proj, proj, p['wkv_conv_w'], p['wkv_conv_w'], p['wkv_conv_w'],
         p['wkv_conv_b'], p['wkv_conv_b'], p['wkv_conv_b'],
         p['wkv_w0'], p['wkv_w2'], p['wkv_a0'], p['wkv_a2'], p['wkv_g2']),
        grid=(nb, nc),
        in_specs=[col(0), col(1), col(2),
                  pl.BlockSpec((seq, N_LORA), lambda b, j, l: (rb + b, N_IN // N_LORA)),
                  cw(0), cw(1), cw(2), cb(0), cb(1), cb(2),
                  pl.BlockSpec((None, 2, ct), lambda b, j, l: (l[0], 0, j)),
                  pl.BlockSpec((None, 2, LORA, ct), lambda b, j, l: (l[0], 0, 0, j)),
                  pl.BlockSpec((None, 2, ct), lambda b, j, l: (l[0], 0, j)),
                  pl.BlockSpec((None, 2, LORA, ct), lambda b, j, l: (l[0], 0, 0, j)),
                  pl.BlockSpec((None, LORA_G, ct), lambda b, j, l: (l[0], 0, j))],
        out_specs=[out] * 8,
        out_shape=[jax.ShapeDtypeStruct((n, W), F32)] * 8,
        sem=("parallel", "parallel"), name=f"wkv_prep_{seq}")


def _wkv_kernel(l_ref, r_ref, k_ref, v_ref, w_ref, a_ref, kkp_ref, kap_ref, s0_ref,
                y_ref, s_ref, *, tb):
    @pl.when(pl.program_id(1) == 0)
    def _():
        s_ref[...] = s0_ref[...]

    kkp = kkp_ref[...]
    kap = kap_ref[...]

    def step(t, carry):
        kt = k_ref[t]
        at = a_ref[t]
        kk = kt * kkp
        kk = kk * lax.rsqrt(jnp.sum(kk * kk, 0, keepdims=True) + 1e-12)
        kd = kt * (1.0 + (at - 1.0) * kap)
        bb = kk * at
        vt = v_ref[t]
        parts = [jnp.zeros((DH, LANES), F32) for _ in range(4)]
        for k in range(DH):
            parts[k % 4] = parts[k % 4] + s_ref[k] * kk[k:k + 1, :]
        sa = -((parts[0] + parts[1]) + (parts[2] + parts[3]))
        ys = [jnp.zeros((DH, LANES), F32) for _ in range(4)]
        for k in range(DH):
            sk = (s_ref[k] * w_ref[t, k:k + 1, :] + sa * bb[k:k + 1, :]) + vt * kd[k:k + 1, :]
            s_ref[k] = sk
            ys[k % 4] = ys[k % 4] + sk * r_ref[t, k:k + 1, :]
        y_ref[t] = (ys[0] + ys[1]) + (ys[2] + ys[3])
        return carry

    lax.fori_loop(0, tb, step, 0)


def _wkv(lidx, r, k, v, w, a, kkp, kap, s0, *, seq):
    tb = 32
    chains = r.shape[-1]
    ng = chains // LANES
    step_spec = pl.BlockSpec((tb, DH, LANES), lambda g, t, l: (t, 0, g))
    par_spec = pl.BlockSpec((None, DH, LANES), lambda g, t, l: (l[0], 0, g))
    st_spec = pl.BlockSpec((DH, DH, LANES), lambda g, t, l: (0, 0, g))
    return _call(
        functools.partial(_wkv_kernel, tb=tb), lidx, (r, k, v, w, a, kkp, kap, s0),
        grid=(ng, seq // tb),
        in_specs=[step_spec] * 5 + [par_spec, par_spec, st_spec],
        out_specs=[step_spec, st_spec],
        out_shape=[jax.ShapeDtypeStruct((seq, DH, chains), F32),
                   jax.ShapeDtypeStruct((DH, DH, chains), F32)],
        sem=("parallel", "arbitrary"), name=f"wkv_{seq}")


def _wkv_post_kernel(l_ref, ya_ref, yb_ref, r_ref, k_ref, v_ref, rk_ref, gg_ref, gb_ref, o_ref):
    y = ya_ref[...] + yb_ref[...]
    mu = jnp.mean(y, 1, keepdims=True)
    yc = y - mu
    var = jnp.mean(yc * yc, 1, keepdims=True)
    yn = yc * lax.rsqrt(var + GN_EPS) * gg_ref[...] + gb_ref[...]
    bonus = jnp.sum(r_ref[...] * k_ref[...] * rk_ref[...], 1, keepdims=True) * v_ref[...]
    o_ref[...] = yn + bonus


def _wkv_post(lidx, ya, yb, r, k, v, rk, gg, gb, *, seq):
    tb = 32
    lanes = ya.shape[-1]
    lb = min(lanes, LANES)
    step_spec = pl.BlockSpec((tb, DH, lb), lambda t, g, l: (t, 0, g))
    par_spec = pl.BlockSpec((None, DH, lb), lambda t, g, l: (l[0], 0, g))
    return _call(
        _wkv_post_kernel, lidx, (ya, yb, r, k, v, rk, gg, gb),
        grid=(seq // tb, lanes // lb),
        in_specs=[step_spec] * 5 + [par_spec] * 3,
        out_specs=step_spec,
        out_shape=jax.ShapeDtypeStruct((seq, DH, lanes), F32),
        sem=("parallel", "parallel"), name=f"wkv_post_{seq}")


def _to_chains(x, nb, seq, flip):
    x = x.reshape(nb, seq, H, DH)
    if flip:
        x = jnp.flip(x, 1)
    return jnp.transpose(x, (1, 3, 0, 2))


def _dir_chains(x0, x1, nb, seq):
    c = jnp.stack([_to_chains(x0, nb, seq, False), _to_chains(x1, nb, seq, True)], 3)
    return c.reshape(seq, DH, nb * 2 * H)


def _head_param(p, reps):
    t = jnp.transpose(p.reshape(DEPTH, H, DH), (0, 2, 1))
    return jnp.tile(t, (1, 1, reps))


def _rwkv_branch(lidx, proj, p, s0_chain, *, nb, seq, rb):
    r, k, v, d0, d1, a0, a1, g = _wkv_prep(lidx, proj, p, nb=nb, seq=seq, rb=rb)
    rc = _dir_chains(r, r, nb, seq)
    kc = _dir_chains(k, k, nb, seq)
    vc = _dir_chains(v, v, nb, seq)
    wc = _dir_chains(d0, d1, nb, seq)
    ac = _dir_chains(a0, a1, nb, seq)
    y, s_fin = _wkv(lidx, rc, kc, vc, wc, ac, p['kk_chain'][:, :, :nb * 2 * H],
                    p['ka_chain'][:, :, :nb * 2 * H], s0_chain, seq=seq)
    y = y.reshape(seq, DH, nb, 2, H)
    ya = y[:, :, :, 0].reshape(seq, DH, nb * H)
    yb = jnp.flip(y[:, :, :, 1], 0).reshape(seq, DH, nb * H)
    one = lambda t: _to_chains(t, nb, seq, False).reshape(seq, DH, nb * H)
    o = _wkv_post(lidx, ya, yb, one(r), one(k), one(v), p['rk_chain'][:, :, :nb * H],
                  p['gg_chain'][:, :, :nb * H], p['gb_chain'][:, :, :nb * H], seq=seq)
    o = jnp.transpose(o.reshape(seq, DH, nb, H), (2, 0, 3, 1)).reshape(nb * seq, W)
    return o, g, s_fin


def _pos_features(seq):
    t = np.linspace(0.0, 1.0, seq, dtype=np.float32)[:, None]
    w = 2.0 * np.pi * np.arange(seq, dtype=np.float32)[:, None] / seq
    f = np.linspace(1e-4, POS_BANDS - 1, POS_BANDS, dtype=np.float32)[None, :]
    z = np.concatenate([t, np.cos(f * w), -np.sin(f * w)], -1).astype(np.float32)
    zp = np.zeros((seq, FH_PAD), np.float32)
    zp[:, :z.shape[1]] = z
    dist = (np.abs(np.arange(seq) - seq // 2).astype(np.float32) / seq)[:, None]
    deltas = np.abs(np.linspace(math.log(1e-2) / 1.5, math.log(1e-2) / 0.3, W,
                                dtype=np.float32))[None, :]
    return zp, dist, deltas


def _filter_kernel(zp_ref, dist_ref, del_ref, f1_ref, b1_ref, f2_ref, b2_ref, fr_ref, f3_ref, o_ref):
    hi = lax.Precision.HIGHEST
    fr = fr_ref[...]
    t = jnp.sin(fr * (jnp.dot(zp_ref[...], f1_ref[...], precision=hi,
                              preferred_element_type=F32) + b1_ref[...]))
    t = jnp.sin(fr * (jnp.dot(t, f2_ref[...], precision=hi, preferred_element_type=F32) + b2_ref[...]))
    filt = jnp.dot(t, f3_ref[...], precision=hi, preferred_element_type=F32)
    filt = filt * jnp.exp(-dist_ref[...] * del_ref[...])
    o_ref[...] = filt / (jnp.sum(jnp.abs(filt), 0, keepdims=True) + 1e-6)


def _hyena_filter(p, seq):
    ct = 256
    zp, dist, deltas = _pos_features(seq)
    full = lambda shape: pl.BlockSpec((None,) + shape, lambda l, j: (l,) + (0,) * len(shape))
    return pl.pallas_call(
        _filter_kernel,
        grid=(DEPTH, W // ct),
        in_specs=[pl.BlockSpec((seq, FH_PAD), lambda l, j: (0, 0)),
                  pl.BlockSpec((seq, 1), lambda l, j: (0, 0)),
                  pl.BlockSpec((1, ct), lambda l, j: (0, j)),
                  full((FH_PAD, FH_PAD)), full((1, FH_PAD)), full((FH_PAD, FH_PAD)),
                  full((1, FH_PAD)), full((1, FH_PAD)),
                  pl.BlockSpec((None, FH_PAD, ct), lambda l, j: (l, 0, j))],
        out_specs=pl.BlockSpec((None, seq, ct), lambda l, j: (l, 0, j)),
        out_shape=jax.ShapeDtypeStruct((DEPTH, seq, W), F32),
        compiler_params=_params(("parallel", "parallel")), name=f"hyena_filter_{seq}",
    )(jnp.asarray(zp), jnp.asarray(dist), jnp.asarray(deltas),
      p['hy_f1p'], p['hy_fb1p'], p['hy_f2p'], p['hy_fb2p'], p['hy_freqp'], p['hy_f3p'])


def _dft_mats(seq):
    n = 2 * seq
    k = jnp.arange(seq, dtype=jnp.int32)
    t = jnp.arange(seq, dtype=jnp.int32)
    ph = ((2 * k[:, None] + 1) * t[None, :]) % (2 * n)
    ang = ph.astype(F32) * np.float32(np.pi / n)
    fwd = jnp.concatenate([jnp.cos(ang), jnp.sin(ang)], 0)
    m = t + seq // 2
    ph2 = ((2 * k[None, :] + 1) * m[:, None]) % (2 * n)
    ang2 = ph2.astype(F32) * np.float32(np.pi / n)
    inv = jnp.concatenate([jnp.cos(ang2), jnp.sin(ang2)], 1) * np.float32(2.0 / n)
    return _split(fwd) + _split(inv)


def _spectrum_kernel(ah_ref, al_ref, b_ref, o_ref):
    bh, bl = _split(b_ref[...])
    o_ref[...] = _dot3(ah_ref[...], al_ref[...], bh, bl)


def _spectrum(fwd_hi, fwd_lo, filt, seq):
    tmm, tn = 256, 512
    return pl.pallas_call(
        _spectrum_kernel,
        grid=(DEPTH, 2 * seq // tmm, W // tn),
        in_specs=[pl.BlockSpec((tmm, seq), lambda l, i, j: (i, 0)),
                  pl.BlockSpec((tmm, seq), lambda l, i, j: (i, 0)),
                  pl.BlockSpec((None, seq, tn), lambda l, i, j: (l, 0, j))],
        out_specs=pl.BlockSpec((None, tmm, tn), lambda l, i, j: (l, i, j)),
        out_shape=jax.ShapeDtypeStruct((DEPTH, 2 * seq, W), F32),
        compiler_params=_params(("parallel", "parallel", "parallel")), name=f"hyena_spectrum_{seq}",
    )(fwd_hi, fwd_lo, filt)


def _hyena_kernel(l_ref, x0_ref, x1_ref, vv_ref, cw0_ref, cw1_ref, cw2_ref, cb0_ref, cb1_ref, cb2_ref,
                  fch_ref, fcl_ref, fsh_ref, fsl_ref, ich_ref, icl_ref, ish_ref, isl_ref,
                  hc_ref, hs_ref, d_ref, o_ref, z_ref, zh_ref, zl_ref, acc_ref):
    fb = pl.program_id(2)

    @pl.when(fb == 0)
    def _():
        z = _conv3(vv_ref[...], cw2_ref, cb2_ref) * _conv3(x1_ref[...], cw1_ref, cb1_ref)
        z_ref[...] = z
        zh, zl = _split(z)
        zh_ref[...] = zh
        zl_ref[...] = zl
        acc_ref[...] = jnp.zeros_like(acc_ref)

    zh = zh_ref[...]
    zl = zl_ref[...]
    zc = _dot3(fch_ref[...], fcl_ref[...], zh, zl)
    zs = _dot3(fsh_ref[...], fsl_ref[...], zh, zl)
    hc = hc_ref[...]
    hs = hs_ref[...]
    pr = zc * hc - zs * hs
    pq = zc * hs + zs * hc
    prh, prl = _split(pr)
    pqh, pql = _split(pq)
    acc_ref[...] += _dot3(ich_ref[...], icl_ref[...], prh, prl) + _dot3(ish_ref[...], isl_ref[...], pqh, pql)

    @pl.when(fb == pl.num_programs(2) - 1)
    def _():
        z = z_ref[...]
        y = acc_ref[...] + z * d_ref[...]
        o_ref[...] = (_conv3(x0_ref[...], cw0_ref, cb0_ref) * y).astype(BF16)


def _hyena(lidx, proj, p, mats, spec, *, nb, seq, rb):
    ct = 256
    fq = min(seq, 256)
    nc = W // ct
    nf = seq // fq
    off = 6 * W // ct
    fwd_hi, fwd_lo, inv_hi, inv_lo = mats

    def col(c):
        return pl.BlockSpec((seq, ct), lambda b, j, f, l: (rb + b, off + c * nc + j))

    def cw(c):
        return pl.BlockSpec((None, 3, ct), lambda b, j, f, l: (l[0], 0, c * nc + j))

    def cb(c):
        return pl.BlockSpec((None, 1, ct), lambda b, j, f, l: (l[0], 0, c * nc + j))

    fcos = pl.BlockSpec((fq, seq), lambda b, j, f, l: (f, 0))
    fsin = pl.BlockSpec((fq, seq), lambda b, j, f, l: (nf + f, 0))
    icos = pl.BlockSpec((seq, fq), lambda b, j, f, l: (0, f))
    isin = pl.BlockSpec((seq, fq), lambda b, j, f, l: (0, nf + f))
    return _call(
        _hyena_kernel, lidx,
        (proj, proj, proj, p['hy_conv_w'], p['hy_conv_w'], p['hy_conv_w'],
         p['hy_conv_b'], p['hy_conv_b'], p['hy_conv_b'],
         fwd_hi, fwd_lo, fwd_hi, fwd_lo, inv_hi, inv_lo, inv_hi, inv_lo, spec, spec, p['hy_d']),
        grid=(nb, nc, nf),
        in_specs=[col(0), col(1), col(2), cw(0), cw(1), cw(2), cb(0), cb(1), cb(2),
                  fcos, fcos, fsin, fsin, icos, icos, isin, isin,
                  pl.BlockSpec((None, fq, ct), lambda b, j, f, l: (l[0], f, j)),
                  pl.BlockSpec((None, fq, ct), lambda b, j, f, l: (l[0], nf + f, j)),
                  pl.BlockSpec((None, 1, ct), lambda b, j, f, l: (l[0], 0, j))],
        out_specs=pl.BlockSpec((seq, ct), lambda b, j, f, l: (b, j)),
        out_shape=jax.ShapeDtypeStruct((nb * seq, W), BF16),
        scratch=[pltpu.VMEM((seq, ct), F32), pltpu.VMEM((seq, ct), BF16),
                 pltpu.VMEM((seq, ct), BF16), pltpu.VMEM((seq, ct), F32)],
        sem=("parallel", "parallel", "arbitrary"), name=f"hyena_{seq}")


def _merge_kernel(l_ref, oa_ref, op_ref, g_ref, oc_ref, ga_ref, gr_ref, gc_ref,
                  wa_ref, wr_ref, wc_ref, o_ref, or_ref):
    @pl.when(pl.program_id(1) == 0)
    def _():
        or_ref[...] = (op_ref[...] * g_ref[...]).astype(BF16)

    m = (jax.nn.sigmoid(ga_ref[...]) * _dot(oa_ref[...], wa_ref[...])
         + jax.nn.sigmoid(gr_ref[...]) * _dot(or_ref[...], wr_ref[...])
         + jax.nn.sigmoid(gc_ref[...]) * _dot(oc_ref[...], wc_ref[...]))
    o_ref[...] = m.astype(BF16)


def _merge(lidx, o_a, o_pre, g, o_c, proj, p):
    tm, tn = 1024, 512
    goff = GL_OFF // tn
    nd = D // tn
    row = lambda: pl.BlockSpec((tm, W), lambda i, j, l: (i, 0))
    gate = lambda c: pl.BlockSpec((tm, tn), lambda i, j, l: (i, goff + c * nd + j))
    wsp = lambda: pl.BlockSpec((None, W, tn), lambda i, j, l: (l[0], 0, j))
    return _call(
        _merge_kernel, lidx, (o_a, o_pre, g, o_c, proj, proj, proj, p['w_pa'], p['w_pr'], p['w_pc']),
        grid=(N_TOK // tm, nd),
        in_specs=[row(), row(), row(), row(), gate(0), gate(1), gate(2), wsp(), wsp(), wsp()],
        out_specs=pl.BlockSpec((tm, tn), lambda i, j, l: (i, j)),
        out_shape=jax.ShapeDtypeStruct((N_TOK, D), BF16),
        scratch=[pltpu.VMEM((tm, W), BF16)],
        sem=("parallel", "arbitrary"), name="merge")


def _out_proj_kernel(l_ref, m_ref, w_ref, x_ref, ga_ref, o_ref, *, tm):
    row = _mod_row(pl.program_id(0), tm)
    o_ref[...] = x_ref[...] + ga_ref[pl.ds(row, 1), :] * _dot(m_ref[...], w_ref[...])


def _out_proj(lidx, merged, x, mod, w_out):
    tm, tn = 1024, 512
    nd = D // tn
    return _call(
        functools.partial(_out_proj_kernel, tm=tm), lidx, (merged, w_out, x, mod),
        grid=(N_TOK // tm, nd),
        in_specs=[pl.BlockSpec((tm, D), lambda i, j, l: (i, 0)),
                  pl.BlockSpec((None, D, tn), lambda i, j, l: (l[0], 0, j)),
                  pl.BlockSpec((tm, tn), lambda i, j, l: (i, j)),
                  pl.BlockSpec((None, 8, tn), lambda i, j, l: (l[0], 0, 2 * nd + j))],
        out_specs=pl.BlockSpec((tm, tn), lambda i, j, l: (i, j)),
        out_shape=jax.ShapeDtypeStruct((N_TOK, D), F32),
        sem=("parallel", "parallel"), name="out_proj")


def _ffn_kernel(l_ref, x_ref, g_ref, sh_ref, sc_ref, ga_ref, w1_ref, b1_ref, w2_ref, b2_ref,
                o_ref, h_ref, acc_ref, *, tm):
    j = pl.program_id(1)
    row = _mod_row(pl.program_id(0), tm)

    @pl.when(j == 0)
    def _():
        h = _modnorm(x_ref[...], g_ref[...], sc_ref[pl.ds(row, 1), :], sh_ref[pl.ds(row, 1), :])
        h_ref[...] = h.astype(BF16)
        acc_ref[...] = jnp.zeros_like(acc_ref)

    a = _dot(h_ref[...], w1_ref[...]) + b1_ref[...]
    a = jnp.square(jnp.maximum(a, 0.0))
    acc_ref[...] += _dot(a.astype(BF16), w2_ref[...])

    @pl.when(j == pl.num_programs(1) - 1)
    def _():
        o_ref[...] = x_ref[...] + ga_ref[pl.ds(row, 1), :] * (acc_ref[...] + b2_ref[...])


def _ffn(lidx, x, ln_g, mod, p):
    tm, tf = 512, 512
    return _call(
        functools.partial(_ffn_kernel, tm=tm), lidx,
        (x, ln_g, mod, mod, mod, p['w_ff1'], p['b_ff1'], p['w_ff2'], p['b_ff2']),
        grid=(N_TOK // tm, D_FF // tf),
        in_specs=[pl.BlockSpec((tm, D), lambda i, j, l: (i, 0)),
                  pl.BlockSpec((None, 1, D), lambda i, j, l: (l[0], 0, 0)),
                  pl.BlockSpec((None, 8, D), lambda i, j, l: (l[0], 0, 3)),
                  pl.BlockSpec((None, 8, D), lambda i, j, l: (l[0], 0, 4)),
                  pl.BlockSpec((None, 8, D), lambda i, j, l: (l[0], 0, 5)),
                  pl.BlockSpec((None, D, tf), lambda i, j, l: (l[0], 0, j)),
                  pl.BlockSpec((None, 1, tf), lambda i, j, l: (l[0], 0, j)),
                  pl.BlockSpec((None, tf, D), lambda i, j, l: (l[0], j, 0)),
                  pl.BlockSpec((None, 1, D), lambda i, j, l: (l[0], 0, 0))],
        out_specs=pl.BlockSpec((tm, D), lambda i, j, l: (i, 0)),
        out_shape=jax.ShapeDtypeStruct((N_TOK, D), F32),
        scratch=[pltpu.VMEM((tm, D), BF16), pltpu.VMEM((tm, D), F32)],
        sem=("parallel", "arbitrary"), name="ffn")


def _final_norm_kernel(x_ref, g_ref, o_ref):
    x = x_ref[...]
    o_ref[...] = x * lax.rsqrt(jnp.mean(x * x, -1, keepdims=True) + NORM_EPS) * g_ref[...]


def _final_norm(x, g):
    tm = 1024
    return pl.pallas_call(
        _final_norm_kernel,
        grid=(N_TOK // tm,),
        in_specs=[pl.BlockSpec((tm, D), lambda i: (i, 0)), pl.BlockSpec((1, D), lambda i: (0, 0))],
        out_specs=pl.BlockSpec((tm, D), lambda i: (i, 0)),
        out_shape=jax.ShapeDtypeStruct((N_TOK, D), F32),
        compiler_params=_params(("parallel",)), name="final_norm",
    )(x, g.reshape(1, D))


def kernel(x_prompt, x_sample, cache_k, cache_v, state_wkv, c, c_ctx, ln1_g, ln2_g, w_mod, b_mod, w_in, rpb, wkv_conv_w, wkv_conv_b, wkv_w0, wkv_w1, wkv_w2, wkv_a0, wkv_a1, wkv_a2, wkv_g1, wkv_g2, wkv_k_k, wkv_k_a, wkv_r_k, wkv_gn_g, wkv_gn_b, hy_conv_w, hy_conv_b, hy_f1, hy_fb1, hy_f2, hy_fb2, hy_freq, hy_f3, hy_d, w_pa, w_pr, w_pc, w_out, w_ff1, b_ff1, w_ff2, b_ff2, final_g):
    x = jnp.concatenate([x_prompt.reshape(N_CTX, D), x_sample.reshape(N_LAT, D)], 0)
    cvec = jnp.zeros((8, D), F32).at[0].set(c_ctx).at[1:1 + B_LAT].set(c)
    mod = _modulation(cvec, w_mod, b_mod)

    pad_c = lambda a, n: jnp.pad(a, [(0, 0)] * (a.ndim - 1) + [(0, n - a.shape[-1])])
    pad_r = lambda a, n: jnp.pad(a, [(0, 0)] * (a.ndim - 2) + [(0, n - a.shape[-2]), (0, 0)])
    p = {
        'wkv_conv_w': wkv_conv_w, 'wkv_conv_b': wkv_conv_b.reshape(DEPTH, 1, 3 * W),
        'wkv_w0': wkv_w0, 'wkv_w2': wkv_w2, 'wkv_a0': wkv_a0, 'wkv_a2': wkv_a2, 'wkv_g2': wkv_g2,
        'kk_chain': _head_param(wkv_k_k, 2 * B_CTX), 'ka_chain': _head_param(wkv_k_a, 2 * B_CTX),
        'rk_chain': _head_param(wkv_r_k.reshape(DEPTH, W), B_CTX),
        'gg_chain': _head_param(wkv_gn_g, B_CTX), 'gb_chain': _head_param(wkv_gn_b, B_CTX),
        'hy_conv_w': hy_conv_w, 'hy_conv_b': hy_conv_b.reshape(DEPTH, 1, 3 * W),
        'hy_d': hy_d.reshape(DEPTH, 1, W),
        'hy_f1p': pad_c(pad_r(hy_f1, FH_PAD), FH_PAD), 'hy_fb1p': pad_c(hy_fb1, FH_PAD).reshape(DEPTH, 1, FH_PAD),
        'hy_f2p': pad_c(pad_r(hy_f2, FH_PAD), FH_PAD), 'hy_fb2p': pad_c(hy_fb2, FH_PAD).reshape(DEPTH, 1, FH_PAD),
        'hy_freqp': pad_c(hy_freq, FH_PAD).reshape(DEPTH, 1, FH_PAD), 'hy_f3p': pad_r(hy_f3, FH_PAD),
        'w_pa': w_pa.astype(BF16), 'w_pr': w_pr.astype(BF16), 'w_pc': w_pc.astype(BF16),
        'w_ff1': w_ff1.astype(BF16), 'b_ff1': b_ff1.reshape(DEPTH, 1, D_FF),
        'w_ff2': w_ff2.astype(BF16), 'b_ff2': b_ff2.reshape(DEPTH, 1, D),
    }
    w_ext = jnp.concatenate(
        [w_in, wkv_w1[:, 0], wkv_w1[:, 1], wkv_a1[:, 0], wkv_a1[:, 1], wkv_g1,
         jnp.zeros((DEPTH, D, N_LORA - 4 * LORA - LORA_G), F32)], -1).astype(BF16)
    w_out_b = w_out.astype(BF16)
    ln1 = ln1_g.reshape(DEPTH, 1, D)
    ln2 = ln2_g.reshape(DEPTH, 1, D)
    bias_tab = _bias_table(rpb)
    ck = cache_k.reshape(B_LAT, DEPTH, L_CTX, W)
    cv = cache_v.reshape(B_LAT, DEPTH, L_CTX, W)
    s0_lat = jnp.transpose(state_wkv, (1, 5, 4, 0, 2, 3)).reshape(DEPTH, DH, DH, B_LAT * 2 * H)
    s0_ctx = jnp.zeros((DH, DH, B_CTX * 2 * H), F32)

    hy = {}
    for seq in (L_CTX, L_LAT):
        mats = _dft_mats(seq)
        filt = _hyena_filter(p, seq)
        hy[seq] = (mats, _spectrum(mats[0], mats[1], filt, seq))

    def layer(x, l):
        lidx = jnp.reshape(l, (1,)).astype(jnp.int32)
        proj = _in_proj(lidx, x, ln1, mod, w_ext)
        oa = jnp.concatenate([_attn_ctx(lidx, proj), _attn_lat(lidx, proj, ck, cv, bias_tab)], 0)
        or_c, g_c, s_ctx = _rwkv_branch(lidx, proj, p, s0_ctx, nb=B_CTX, seq=L_CTX, rb=0)
        s0 = lax.dynamic_index_in_dim(s0_lat, l, 0, keepdims=False)
        or_l, g_l, _ = _rwkv_branch(lidx, proj, p, s0, nb=B_LAT, seq=L_LAT, rb=N_CTX // L_LAT)
        oc = jnp.concatenate(
            [_hyena(lidx, proj, p, *hy[L_CTX], nb=B_CTX, seq=L_CTX, rb=0),
             _hyena(lidx, proj, p, *hy[L_LAT], nb=B_LAT, seq=L_LAT, rb=N_CTX // L_LAT)], 0)
        merged = _merge(lidx, oa, jnp.concatenate([or_c, or_l], 0), jnp.concatenate([g_c, g_l], 0),
                        oc, proj, p)
        x = _out_proj(lidx, merged, x, mod, w_out_b)
        x = _ffn(lidx, x, ln2, mod, p)
        k_new = proj[:N_CTX, W:2 * W]
        v_new = proj[:N_CTX, 2 * W:3 * W]
        return x, (k_new, v_new, s_ctx)

    x, (ks, vs, ss) = lax.scan(layer, x, jnp.arange(DEPTH, dtype=jnp.int32))
    y = _final_norm(x, final_g)
    y_prompt = y[:N_CTX].reshape(B_CTX, L_CTX, D)
    y_sample = y[N_CTX:].reshape(B_LAT, L_LAT, D)
    new_k = jnp.transpose(ks.reshape(DEPTH, B_CTX, L_CTX, H, DH), (1, 0, 2, 3, 4))
    new_v = jnp.transpose(vs.reshape(DEPTH, B_CTX, L_CTX, H, DH), (1, 0, 2, 3, 4))
    new_s = jnp.transpose(ss.reshape(DEPTH, DH, DH, B_CTX, 2, H), (3, 0, 4, 5, 2, 1))
    return (y_prompt, y_sample, new_k, new_v, new_s)
```

```python
import functools
import math

import numpy as np
import jax
import jax.numpy as jnp
from jax import lax
from jax.experimental import pallas as pl
from jax.experimental.pallas import tpu as pltpu

F32 = jnp.float32
BF16 = jnp.bfloat16

D = 2048
DEPTH = 4
B_CTX, L_CTX = 32, 256
B_LAT, L_LAT = 4, 2048
N_CTX = B_CTX * L_CTX
N_LAT = B_LAT * L_LAT
N_TOK = N_CTX + N_LAT
H = 16
DH = 64
W = H * DH
GRID_W = 64
ROWS = L_LAT // GRID_W
WIN_R, WIN_C = 8, 16
NL = WIN_R * GRID_W
LORA = 64
LORA_G = 128
N_LORA = 512
N_A = 6 * W + 3 * D
HY_OFF = 3 * W
GL_OFF = 6 * W
N_T = 3 * W + N_LORA
D_FF = 4 * D
N_MOD = 6
NORM_EPS = 1e-6
GN_EPS = 64e-5
NEG_INF = -1e30
POS_BANDS = 16
FH_PAD = 128
LANES = 128
GB = LANES // (2 * H)
TCH = 128
VMEM_LIMIT = 56 * 1024 * 1024


def _params(sem):
    return pltpu.CompilerParams(dimension_semantics=sem, vmem_limit_bytes=VMEM_LIMIT)


def _call(kernel, lidx, args, *, grid, in_specs, out_specs, out_shape, scratch=(), sem, name,
          aliases=None):
    gs = pltpu.PrefetchScalarGridSpec(num_scalar_prefetch=1, grid=grid, in_specs=in_specs,
                                      out_specs=out_specs, scratch_shapes=list(scratch))
    return pl.pallas_call(kernel, grid_spec=gs, out_shape=out_shape,
                          compiler_params=_params(sem), name=name,
                          input_output_aliases=aliases or {})(lidx, *args)


_ANY = pl.BlockSpec(memory_space=pl.ANY)


def _mod_row(i, tm):
    start = i * tm
    return jnp.where(start < N_CTX, 0, 1 + (start - N_CTX) // L_LAT)


def _modnorm(x, g, sc, sh):
    y = x * lax.rsqrt(jnp.mean(x * x, -1, keepdims=True) + NORM_EPS)
    return (y * g) * (1.0 + sc) + sh


def _softplus(x):
    return jnp.maximum(x, 0.0) + jnp.log1p(jnp.exp(-jnp.abs(x)))


def _dot(a, b):
    return jnp.dot(a, b, preferred_element_type=F32)


def _dot_nt(a, b):
    return lax.dot_general(a, b, (((1,), (1,)), ((), ())), preferred_element_type=F32)


def _dot_tn(a, b):
    return lax.dot_general(a, b, (((0,), (0,)), ((), ())), preferred_element_type=F32)


def _split(x):
    hi = x.astype(BF16)
    lo = (x - hi.astype(F32)).astype(BF16)
    return hi, lo


def _dot3(ah, al, bh, bl):
    return _dot(ah, bh) + (_dot(al, bh) + _dot(ah, bl))


def _mod_kernel(c_ref, w_ref, b_ref, o_ref):
    c = c_ref[...]
    s = c * jax.nn.sigmoid(c)
    o_ref[...] = _dot(s.astype(BF16), w_ref[...].astype(BF16)) + b_ref[...]


def _modulation(cvec8, w_mod, b_mod):
    tn = 1024
    return pl.pallas_call(
        _mod_kernel,
        grid=(DEPTH, N_MOD * D // tn),
        in_specs=[pl.BlockSpec((8, D), lambda l, j: (0, 0)),
                  pl.BlockSpec((None, D, tn), lambda l, j: (l, 0, j)),
                  pl.BlockSpec((None, 1, tn), lambda l, j: (l, 0, j))],
        out_specs=pl.BlockSpec((None, 8, tn), lambda l, j: (l, 0, j)),
        out_shape=jax.ShapeDtypeStruct((DEPTH, 8, N_MOD * D), F32),
        compiler_params=_params(("parallel", "parallel")), name="modulation",
    )(cvec8, w_mod, b_mod.reshape(DEPTH, 1, N_MOD * D))


def _in_proj_kernel(l_ref, x_ref, g_ref, sh_ref, sc_ref, w_ref, o_ref, h_ref, *, tm, transposed):
    @pl.when(pl.program_id(1) == 0)
    def _():
        row = _mod_row(pl.program_id(0), tm)
        h = _modnorm(x_ref[...], g_ref[...], sc_ref[pl.ds(row, 1), :], sh_ref[pl.ds(row, 1), :])
        h_ref[...] = h.astype(BF16)

    if transposed:
        o_ref[...] = _dot_nt(w_ref[...], h_ref[...])
    else:
        o_ref[...] = _dot(h_ref[...], w_ref[...])


def _in_proj(lidx, x, ln_g, mod, w, *, transposed):
    tm, tn = 1024, 512
    if transposed:
        n_out = w.shape[1]
        w_spec = pl.BlockSpec((None, tn, D), lambda i, j, l: (l[0], j, 0))
        o_spec = pl.BlockSpec((tn, tm), lambda i, j, l: (j, i))
        o_shape = jax.ShapeDtypeStruct((n_out, N_TOK), F32)
    else:
        n_out = w.shape[2]
        w_spec = pl.BlockSpec((None, D, tn), lambda i, j, l: (l[0], 0, j))
        o_spec = pl.BlockSpec((tm, tn), lambda i, j, l: (i, j))
        o_shape = jax.ShapeDtypeStruct((N_TOK, n_out), F32)
    return _call(
        functools.partial(_in_proj_kernel, tm=tm, transposed=transposed), lidx, (x, ln_g, mod, mod, w),
        grid=(N_TOK // tm, n_out // tn),
        in_specs=[pl.BlockSpec((tm, D), lambda i, j, l: (i, 0)),
                  pl.BlockSpec((None, 1, D), lambda i, j, l: (l[0], 0, 0)),
                  pl.BlockSpec((None, 8, D), lambda i, j, l: (l[0], 0, 0)),
                  pl.BlockSpec((None, 8, D), lambda i, j, l: (l[0], 0, 1)),
                  w_spec],
        out_specs=o_spec, out_shape=o_shape,
        scratch=[pltpu.VMEM((tm, D), BF16)],
        sem=("parallel", "arbitrary"), name="in_proj_t" if transposed else "in_proj")


def _attn_ctx_kernel(l_ref, q_ref, k_ref, v_ref, o_ref):
    scale = DH ** -0.5
    outs = []
    for hh in range(2):
        sl = slice(hh * DH, (hh + 1) * DH)
        q = q_ref[:, sl].astype(BF16)
        k = k_ref[:, sl].astype(BF16)
        v = v_ref[:, sl].astype(BF16)
        s = _dot_nt(q, k) * scale
        p = jnp.exp(s - jnp.max(s, -1, keepdims=True))
        den = jnp.sum(p, -1, keepdims=True)
        outs.append(_dot(p.astype(BF16), v) / den)
    o_ref[...] = jnp.concatenate(outs, -1).astype(BF16)


def _attn_ctx(lidx, proj):
    nq = W // LANES
    return _call(
        _attn_ctx_kernel, lidx, (proj, proj, proj),
        grid=(B_CTX, nq),
        in_specs=[pl.BlockSpec((L_CTX, LANES), lambda b, p, l: (b, p)),
                  pl.BlockSpec((L_CTX, LANES), lambda b, p, l: (b, nq + p)),
                  pl.BlockSpec((L_CTX, LANES), lambda b, p, l: (b, 2 * nq + p))],
        out_specs=pl.BlockSpec((L_CTX, LANES), lambda b, p, l: (b, p)),
        out_shape=jax.ShapeDtypeStruct((N_TOK, W), BF16),
        sem=("parallel", "parallel"), name="attn_ctx")


def _attn_lat_kernel(l_ref, q_ref, k_ref, v_ref, kc_ref, vc_ref, bias_ref, alias_ref, o_ref):
    scale = DH ** -0.5

    def row_block(r, carry):
        r0 = jnp.clip(r - WIN_R // 2, 0, ROWS - WIN_R)
        d = r - r0
        qrow = pl.multiple_of(r * GRID_W, GRID_W)
        krow = pl.multiple_of(r0 * GRID_W, GRID_W)
        for hh in range(2):
            sl = slice(hh * DH, (hh + 1) * DH)
            q = q_ref[pl.ds(qrow, GRID_W), sl].astype(BF16)
            kl = k_ref[pl.ds(krow, NL), sl].astype(BF16)
            vl = v_ref[pl.ds(krow, NL), sl].astype(BF16)
            kc = kc_ref[:, sl].astype(BF16)
            vc = vc_ref[:, sl].astype(BF16)
            s_loc = _dot_nt(q, kl) * scale + bias_ref[hh, d]
            s_ctx = _dot_nt(q, kc) * scale
            m = jnp.maximum(jnp.max(s_loc, -1, keepdims=True), jnp.max(s_ctx, -1, keepdims=True))
            p_loc = jnp.exp(s_loc - m)
            p_ctx = jnp.exp(s_ctx - m)
            den = jnp.sum(p_loc, -1, keepdims=True) + jnp.sum(p_ctx, -1, keepdims=True)
            o = (_dot(p_loc.astype(BF16), vl) + _dot(p_ctx.astype(BF16), vc)) / den
            o_ref[pl.ds(qrow, GRID_W), sl] = o.astype(BF16)
        return carry

    lax.fori_loop(0, ROWS, row_block, 0, unroll=2)


def _attn_lat(lidx, proj, cache_k, cache_v, bias_tab, o_ctx):
    nq = W // LANES
    rb = N_CTX // L_LAT
    return _call(
        _attn_lat_kernel, lidx, (proj, proj, proj, cache_k, cache_v, bias_tab, o_ctx),
        grid=(B_LAT, nq),
        in_specs=[pl.BlockSpec((L_LAT, LANES), lambda b, p, l: (rb + b, p)),
                  pl.BlockSpec((L_LAT, LANES), lambda b, p, l: (rb + b, nq + p)),
                  pl.BlockSpec((L_LAT, LANES), lambda b, p, l: (rb + b, 2 * nq + p)),
                  pl.BlockSpec((None, None, L_CTX, LANES), lambda b, p, l: (b, l[0], 0, p)),
                  pl.BlockSpec((None, None, L_CTX, LANES), lambda b, p, l: (b, l[0], 0, p)),
                  pl.BlockSpec((None, 2, WIN_R, GRID_W, NL), lambda b, p, l: (l[0], p, 0, 0, 0)),
                  _ANY],
        out_specs=pl.BlockSpec((L_LAT, LANES), lambda b, p, l: (rb + b, p)),
        out_shape=jax.ShapeDtypeStruct((N_TOK, W), BF16),
        sem=("parallel", "parallel"), name="attn_lat", aliases={7: 0})


def _bias_table(rpb):
    cq = np.arange(GRID_W)
    c0 = np.clip(cq - WIN_C // 2, 0, GRID_W - WIN_C)
    ck = np.arange(GRID_W)
    col_ok = (ck[None, :] >= c0[:, None]) & (ck[None, :] < c0[:, None] + WIN_C)
    dc = np.clip(ck[None, :] - cq[:, None], -(WIN_C - 1), WIN_C - 1) + (WIN_C - 1)
    onehot = (dc.reshape(-1)[None, :] == np.arange(2 * WIN_C - 1)[:, None]).astype(np.float32)
    cols = jnp.einsum('lhrc,cx->lhrx', rpb, jnp.asarray(onehot), precision=lax.Precision.HIGHEST)
    cols = cols.reshape(DEPTH, H, 2 * WIN_R - 1, GRID_W, GRID_W)
    tab = jnp.stack([cols[:, :, WIN_R - 1 - d:2 * WIN_R - 1 - d] for d in range(WIN_R)], 2)
    tab = jnp.transpose(tab, (0, 1, 2, 4, 3, 5))
    tab = jnp.where(col_ok[None, None, None, :, None, :], tab, NEG_INF)
    return tab.reshape(DEPTH, H, WIN_R, GRID_W, NL).astype(F32)


def _conv3(x, w_ref, b_ref):
    n = x.shape[0]
    row = lax.broadcasted_iota(jnp.int32, x.shape, 0)
    prev = jnp.where(row == 0, 0.0, pltpu.roll(x, 1, 0))
    nxt = jnp.where(row == n - 1, 0.0, pltpu.roll(x, n - 1, 0))
    return prev * w_ref[0:1, :] + x * w_ref[1:2, :] + nxt * w_ref[2:3, :] + b_ref[...]


def _conv3_t(x, p):
    n = x.shape[1]
    lane = lax.broadcasted_iota(jnp.int32, x.shape, 1)
    prev = jnp.where(lane == 0, 0.0, pltpu.roll(x, 1, 1))
    nxt = jnp.where(lane == n - 1, 0.0, pltpu.roll(x, n - 1, 1))
    return prev * p[:, 0:1] + x * p[:, 1:2] + nxt * p[:, 2:3] + p[:, 3:4]


def _wkv_prep_kernel(l_ref, r_ref, k_ref, v_ref, lora_ref, cp_ref, rp_ref, w2_ref, a2_ref, g2_ref,
                     ro_ref, ko_ref, vo_ref, d0_ref, d1_ref, a0o_ref, a1o_ref, go_ref):
    ro_ref[...] = _conv3_t(r_ref[...], cp_ref[0])
    ko_ref[...] = _conv3_t(k_ref[...], cp_ref[1])
    vo_ref[...] = _conv3_t(v_ref[...], cp_ref[2])
    rp = rp_ref[...]
    for e, (d_ref, ao_ref) in enumerate(((d0_ref, a0o_ref), (d1_ref, a1o_ref))):
        lw = jnp.tanh(lora_ref[e * LORA:(e + 1) * LORA, :]).astype(BF16)
        la = lora_ref[(2 + e) * LORA:(3 + e) * LORA, :].astype(BF16)
        w_log = rp[:, e:e + 1] + _dot(w2_ref[e].astype(BF16), lw)
        d_ref[...] = jnp.exp(-jnp.exp(-_softplus(-w_log) - 0.5))
        ao_ref[...] = jax.nn.sigmoid(rp[:, 2 + e:3 + e] + _dot(a2_ref[e].astype(BF16), la))
    lg = jax.nn.sigmoid(lora_ref[4 * LORA:4 * LORA + LORA_G, :]).astype(BF16)
    go_ref[...] = _dot(g2_ref[...].astype(BF16), lg)


def _wkv_prep(lidx, proj_t, p, *, nb, seq, rb):
    ct = 256 if seq <= 256 else 128
    nc = W // ct

    def row(c):
        return pl.BlockSpec((ct, seq), lambda b, j, l: (c * nc + j, rb + b))

    out = pl.BlockSpec((None, ct, seq), lambda b, j, l: (b, j, 0))
    return _call(
        _wkv_prep_kernel, lidx,
        (proj_t, proj_t, proj_t, proj_t, p['wkv_conv'], p['wkv_rowp'], p['wkv_w2t'], p['wkv_a2t'],
         p['wkv_g2t']),
        grid=(nb, nc),
        in_specs=[row(0), row(1), row(2),
                  pl.BlockSpec((N_LORA, seq), lambda b, j, l: (3 * W // N_LORA, rb + b)),
                  pl.BlockSpec((None, 3, ct, 4), lambda b, j, l: (l[0], 0, j, 0)),
                  pl.BlockSpec((None, ct, 4), lambda b, j, l: (l[0], j, 0)),
                  pl.BlockSpec((None, 2, ct, LORA), lambda b, j, l: (l[0], 0, j, 0)),
                  pl.BlockSpec((None, 2, ct, LORA), lambda b, j, l: (l[0], 0, j, 0)),
                  pl.BlockSpec((None, ct, LORA_G), lambda b, j, l: (l[0], j, 0))],
        out_specs=[out] * 8,
        out_shape=[jax.ShapeDtypeStruct((nb, W, seq), F32)] * 8,
        sem=("parallel", "parallel"), name=f"wkv_prep_{seq}")


def _to_chains_kernel(l_ref, *refs):
    xs, o_ref = refs[:2 * GB], refs[2 * GB]
    for k in range(DH):
        m = jnp.concatenate([x[k * H:(k + 1) * H, :] for x in xs], 0)
        o_ref[:, k, :] = m.T


def _to_chains(lidx, x0, x1, *, nb, seq):
    def src(i):
        return pl.BlockSpec((None, W, TCH), lambda g, t, l: (GB * g + i // 2, 0, t))

    args = tuple(x0 if i % 2 == 0 else x1 for i in range(2 * GB))
    return _call(
        _to_chains_kernel, lidx, args,
        grid=(nb // GB, seq // TCH),
        in_specs=[src(i) for i in range(2 * GB)],
        out_specs=pl.BlockSpec((TCH, DH, LANES), lambda g, t, l: (t, 0, g)),
        out_shape=jax.ShapeDtypeStruct((seq, DH, nb * 2 * H), F32),
        sem=("parallel", "parallel"), name=f"to_chains_{seq}")


def _wkv_kernel(l_ref, rf_ref, kf_ref, vf_ref, wf_ref, af_ref, rb_ref, kb_ref, vb_ref, wb_ref, ab_ref,
                kkp_ref, kap_ref, s0_ref, yf_ref, yb_ref, s_ref, st_ref, *, tb):
    @pl.when(pl.program_id(1) == 0)
    def _():
        s_ref[...] = s0_ref[...]

    kkp = kkp_ref[...]
    kap = kap_ref[...]
    lane = lax.broadcasted_iota(jnp.int32, (DH, LANES), 1)
    bwd = (lane // H) % 2 == 1
    kc = 16
    zero = jnp.zeros((DH, LANES), F32)

    def step(t, carry):
        tr = tb - 1 - t
        pick = lambda f_ref, b_ref: jnp.where(bwd, b_ref[tr], f_ref[t])
        kt = pick(kf_ref, kb_ref)
        at = pick(af_ref, ab_ref)
        vt = pick(vf_ref, vb_ref)
        kk = kt * kkp
        kk = kk * lax.rsqrt(jnp.sum(kk * kk, 0, keepdims=True) + 1e-12)
        st_ref[0] = kk
        st_ref[1] = kk * at
        st_ref[2] = kt * (1.0 + (at - 1.0) * kap)
        st_ref[3] = pick(wf_ref, wb_ref)
        st_ref[4] = pick(rf_ref, rb_ref)

        def sa_body(c, acc):
            a0, a1 = acc
            for j in range(kc):
                k = c * kc + j
                term = s_ref[k] * st_ref[0, pl.ds(k, 1), :]
                if j % 2 == 0:
                    a0 = a0 + term
                else:
                    a1 = a1 + term
            return a0, a1

        a0, a1 = lax.fori_loop(0, DH // kc, sa_body, (zero, zero))
        sa = -(a0 + a1)

        def up_body(c, acc):
            y0, y1 = acc
            for j in range(kc):
                k = c * kc + j
                sk = ((s_ref[k] * st_ref[3, pl.ds(k, 1), :] + sa * st_ref[1, pl.ds(k, 1), :])
                      + vt * st_ref[2, pl.ds(k, 1), :])
                s_ref[k] = sk
                term = sk * st_ref[4, pl.ds(k, 1), :]
                if j % 2 == 0:
                    y0 = y0 + term
                else:
                    y1 = y1 + term
            return y0, y1

        y0, y1 = lax.fori_loop(0, DH // kc, up_body, (zero, zero))
        y = y0 + y1
        yf_ref[t] = y
        yb_ref[tr] = y
        return carry

    lax.fori_loop(0, tb, step, 0)


def _wkv(lidx, r, k, v, w, a, kkp, kap, s0, *, seq):
    tb = 32
    chains = r.shape[-1]
    nt = seq // tb
    f_spec = pl.BlockSpec((tb, DH, LANES), lambda g, t, l: (t, 0, g))
    b_spec = pl.BlockSpec((tb, DH, LANES), lambda g, t, l: (nt - 1 - t, 0, g))
    par_spec = pl.BlockSpec((None, DH, LANES), lambda g, t, l: (l[0], 0, g))
    st_spec = pl.BlockSpec((DH, DH, LANES), lambda g, t, l: (0, 0, g))
    y_shape = jax.ShapeDtypeStruct((seq, DH, chains), F32)
    return _call(
        functools.partial(_wkv_kernel, tb=tb), lidx, (r, k, v, w, a, r, k, v, w, a, kkp, kap, s0),
        grid=(chains // LANES, nt),
        in_specs=[f_spec] * 5 + [b_spec] * 5 + [par_spec, par_spec, st_spec],
        out_specs=[f_spec, b_spec, st_spec],
        out_shape=[y_shape, y_shape, jax.ShapeDtypeStruct((DH, DH, chains), F32)],
        scratch=[pltpu.VMEM((5, DH, LANES), F32)],
        sem=("parallel", "arbitrary"), name=f"wkv_{seq}")


def _from_chains_kernel(l_ref, yf_ref, yb_ref, o_ref):
    for v in range(DH):
        tf = yf_ref[:, v, :].T
        tb = yb_ref[:, v, :].T
        for b in range(GB):
            lo = b * 2 * H
            o_ref[b, v * H:(v + 1) * H, :] = tf[lo:lo + H] + tb[lo + H:lo + 2 * H]


def _from_chains(lidx, yf, yb, *, nb, seq):
    y_spec = pl.BlockSpec((TCH, DH, LANES), lambda g, t, l: (t, 0, g))
    return _call(
        _from_chains_kernel, lidx, (yf, yb),
        grid=(nb // GB, seq // TCH),
        in_specs=[y_spec, y_spec],
        out_specs=pl.BlockSpec((GB, W, TCH), lambda g, t, l: (g, 0, t)),
        out_shape=jax.ShapeDtypeStruct((nb, W, seq), F32),
        sem=("parallel", "parallel"), name=f"from_chains_{seq}")


def _wkv_post_kernel(l_ref, y_ref, r_ref, k_ref, v_ref, g_ref, gate_ref, par_ref, w_ref, *rest):
    o_ref = rest[-1]
    t = y_ref.shape[-1]
    par = par_ref[...]
    y = y_ref[...].reshape(DH, H, t)
    mu = jnp.mean(y, 0, keepdims=True)
    yc = y - mu
    var = jnp.mean(yc * yc, 0, keepdims=True)
    yn = (yc * lax.rsqrt(var + GN_EPS)).reshape(W, t) * par[:, 1:2] + par[:, 2:3]
    rk = (r_ref[...] * k_ref[...] * par[:, 0:1]).reshape(DH, H, t)
    bonus = jnp.broadcast_to(jnp.sum(rk, 0, keepdims=True), (DH, H, t)).reshape(W, t) * v_ref[...]
    o = ((yn + bonus) * g_ref[...]).astype(BF16)
    o_ref[...] = jax.nn.sigmoid(gate_ref[...]) * _dot_tn(o, w_ref[...])


def _wkv_post(lidx, ysum, r, k, v, g, proj, p, prev, *, nb, seq, rb):
    tt = 256
    nt = seq // tt
    t_spec = pl.BlockSpec((None, W, tt), lambda b, t, l: (b, 0, t))
    tok = lambda b, t: rb + b * nt + t
    args = (ysum, r, k, v, g, proj, p['wkv_post'], p['w_pr'])
    in_specs = [t_spec] * 5 + [
        pl.BlockSpec((tt, D), lambda b, t, l: (tok(b, t), (GL_OFF + D) // D)),
        pl.BlockSpec((None, W, 4), lambda b, t, l: (l[0], 0, 0)),
        pl.BlockSpec((None, W, D), lambda b, t, l: (l[0], 0, 0))]
    aliases = None
    if prev is not None:
        args = args + (prev,)
        in_specs = in_specs + [_ANY]
        aliases = {len(args): 0}
    return _call(
        _wkv_post_kernel, lidx, args,
        grid=(nb, nt), in_specs=in_specs,
        out_specs=pl.BlockSpec((tt, D), lambda b, t, l: (tok(b, t), 0)),
        out_shape=jax.ShapeDtypeStruct((N_TOK, D), F32),
        sem=("parallel", "parallel"), name=f"wkv_post_{seq}", aliases=aliases)


def _head_param(p, reps):
    t = jnp.transpose(p.reshape(DEPTH, H, DH), (0, 2, 1))
    return jnp.tile(t, (1, 1, reps))


def _rwkv_branch(lidx, proj, proj_t, p, s0_chain, prev, *, nb, seq, rb_t, rb_tok):
    r, k, v, d0, d1, a0, a1, g = _wkv_prep(lidx, proj_t, p, nb=nb, seq=seq, rb=rb_t)
    tc = functools.partial(_to_chains, lidx, nb=nb, seq=seq)
    chains = nb * 2 * H
    yf, yb, s_fin = _wkv(lidx, tc(r, r), tc(k, k), tc(v, v), tc(d0, d1), tc(a0, a1),
                         p['kk_chain'][:, :, :chains], p['ka_chain'][:, :, :chains], s0_chain, seq=seq)
    ysum = _from_chains(lidx, yf, yb, nb=nb, seq=seq)
    m_r = _wkv_post(lidx, ysum, r, k, v, g, proj, p, prev, nb=nb, seq=seq, rb=rb_tok)
    return m_r, s_fin


def _pos_features(seq):
    t = np.linspace(0.0, 1.0, seq, dtype=np.float32)[:, None]
    w = 2.0 * np.pi * np.arange(seq, dtype=np.float32)[:, None] / seq
    f = np.linspace(1e-4, POS_BANDS - 1, POS_BANDS, dtype=np.float32)[None, :]
    z = np.concatenate([t, np.cos(f * w), -np.sin(f * w)], -1).astype(np.float32)
    zp = np.zeros((seq, FH_PAD), np.float32)
    zp[:, :z.shape[1]] = z
    dist = (np.abs(np.arange(seq) - seq // 2).astype(np.float32) / seq)[:, None]
    deltas = np.abs(np.linspace(math.log(1e-2) / 1.5, math.log(1e-2) / 0.3, W,
                                dtype=np.float32))[None, :]
    return zp, dist, deltas


def _filter_kernel(zp_ref, dist_ref, del_ref, f1_ref, b1_ref, f2_ref, b2_ref, fr_ref, f3_ref, o_ref):
    hi = lax.Precision.HIGHEST
    fr = fr_ref[...]
    t = jnp.sin(fr * (jnp.dot(zp_ref[...], f1_ref[...], precision=hi,
                              preferred_element_type=F32) + b1_ref[...]))
    t = jnp.sin(fr * (jnp.dot(t, f2_ref[...], precision=hi, preferred_element_type=F32) + b2_ref[...]))
    filt = jnp.dot(t, f3_ref[...], precision=hi, preferred_element_type=F32)
    filt = filt * jnp.exp(-dist_ref[...] * del_ref[...])
    o_ref[...] = filt / (jnp.sum(jnp.abs(filt), 0, keepdims=True) + 1e-6)


def _hyena_filter(p, seq):
    ct = 256
    zp, dist, deltas = _pos_features(seq)
    full = lambda shape: pl.BlockSpec((None,) + shape, lambda l, j: (l,) + (0,) * len(shape))
    return pl.pallas_call(
        _filter_kernel,
        grid=(DEPTH, W // ct),
        in_specs=[pl.BlockSpec((seq, FH_PAD), lambda l, j: (0, 0)),
                  pl.BlockSpec((seq, 1), lambda l, j: (0, 0)),
                  pl.BlockSpec((1, ct), lambda l, j: (0, j)),
                  full((FH_PAD, FH_PAD)), full((1, FH_PAD)), full((FH_PAD, FH_PAD)),
                  full((1, FH_PAD)), full((1, FH_PAD)),
                  pl.BlockSpec((None, FH_PAD, ct), lambda l, j: (l, 0, j))],
        out_specs=pl.BlockSpec((None, seq, ct), lambda l, j: (l, 0, j)),
        out_shape=jax.ShapeDtypeStruct((DEPTH, seq, W), F32),
        compiler_params=_params(("parallel", "parallel")), name=f"hyena_filter_{seq}",
    )(jnp.asarray(zp), jnp.asarray(dist), jnp.asarray(deltas),
      p['hy_f1p'], p['hy_fb1p'], p['hy_f2p'], p['hy_fb2p'], p['hy_freqp'], p['hy_f3p'])


def _dft_mats(seq):
    n = 2 * seq
    k = jnp.arange(seq, dtype=jnp.int32)
    t = jnp.arange(seq, dtype=jnp.int32)
    ph = ((2 * k[:, None] + 1) * t[None, :]) % (2 * n)
    ang = ph.astype(F32) * np.float32(np.pi / n)
    fwd = jnp.concatenate([jnp.cos(ang), jnp.sin(ang)], 0)
    m = t + seq // 2
    ph2 = ((2 * k[None, :] + 1) * m[:, None]) % (2 * n)
    ang2 = ph2.astype(F32) * np.float32(np.pi / n)
    inv = jnp.concatenate([jnp.cos(ang2), jnp.sin(ang2)], 1) * np.float32(2.0 / n)
    return _split(fwd) + _split(inv)


def _spectrum_kernel(ah_ref, al_ref, b_ref, o_ref):
    bh, bl = _split(b_ref[...])
    o_ref[...] = _dot3(ah_ref[...], al_ref[...], bh, bl)


def _spectrum(fwd_hi, fwd_lo, filt, seq):
    tmm, tn = 256, 512
    return pl.pallas_call(
        _spectrum_kernel,
        grid=(DEPTH, 2 * seq // tmm, W // tn),
        in_specs=[pl.BlockSpec((tmm, seq), lambda l, i, j: (i, 0)),
                  pl.BlockSpec((tmm, seq), lambda l, i, j: (i, 0)),
                  pl.BlockSpec((None, seq, tn), lambda l, i, j: (l, 0, j))],
        out_specs=pl.BlockSpec((None, tmm, tn), lambda l, i, j: (l, i, j)),
        out_shape=jax.ShapeDtypeStruct((DEPTH, 2 * seq, W), F32),
        compiler_params=_params(("parallel", "parallel", "parallel")), name=f"hyena_spectrum_{seq}",
    )(fwd_hi, fwd_lo, filt)


def _hyena_kernel(l_ref, x0_ref, x1_ref, vv_ref, cw0_ref, cw1_ref, cw2_ref, cb0_ref, cb1_ref, cb2_ref,
                  fch_ref, fcl_ref, fsh_ref, fsl_ref, ich_ref, icl_ref, ish_ref, isl_ref,
                  hc_ref, hs_ref, d_ref, *rest):
    o_ref, z_ref, zh_ref, zl_ref, acc_ref = rest[-5:]
    fb = pl.program_id(2)

    @pl.when(fb == 0)
    def _():
        z = _conv3(vv_ref[...], cw2_ref, cb2_ref) * _conv3(x1_ref[...], cw1_ref, cb1_ref)
        z_ref[...] = z
        zh, zl = _split(z)
        zh_ref[...] = zh
        zl_ref[...] = zl
        acc_ref[...] = jnp.zeros_like(acc_ref)

    zh = zh_ref[...]
    zl = zl_ref[...]
    zc = _dot3(fch_ref[...], fcl_ref[...], zh, zl)
    zs = _dot3(fsh_ref[...], fsl_ref[...], zh, zl)
    hc = hc_ref[...]
    hs = hs_ref[...]
    pr = zc * hc - zs * hs
    pq = zc * hs + zs * hc
    prh, prl = _split(pr)
    pqh, pql = _split(pq)
    acc_ref[...] += _dot3(ich_ref[...], icl_ref[...], prh, prl) + _dot3(ish_ref[...], isl_ref[...], pqh, pql)

    @pl.when(fb == pl.num_programs(2) - 1)
    def _():
        z = z_ref[...]
        y = acc_ref[...] + z * d_ref[...]
        o_ref[...] = (_conv3(x0_ref[...], cw0_ref, cb0_ref) * y).astype(BF16)


def _hyena(lidx, proj, p, mats, spec, prev, *, nb, seq, rb):
    ct = 256
    fq = min(seq, 256)
    nc = W // ct
    nf = seq // fq
    off = HY_OFF // ct
    fwd_hi, fwd_lo, inv_hi, inv_lo = mats

    def col(c):
        return pl.BlockSpec((seq, ct), lambda b, j, f, l: (rb + b, off + c * nc + j))

    def cw(c):
        return pl.BlockSpec((None, 3, ct), lambda b, j, f, l: (l[0], 0, c * nc + j))

    def cb(c):
        return pl.BlockSpec((None, 1, ct), lambda b, j, f, l: (l[0], 0, c * nc + j))

    fcos = pl.BlockSpec((fq, seq), lambda b, j, f, l: (f, 0))
    fsin = pl.BlockSpec((fq, seq), lambda b, j, f, l: (nf + f, 0))
    icos = pl.BlockSpec((seq, fq), lambda b, j, f, l: (0, f))
    isin = pl.BlockSpec((seq, fq), lambda b, j, f, l: (0, nf + f))
    args = (proj, proj, proj, p['hy_conv_w'], p['hy_conv_w'], p['hy_conv_w'],
            p['hy_conv_b'], p['hy_conv_b'], p['hy_conv_b'],
            fwd_hi, fwd_lo, fwd_hi, fwd_lo, inv_hi, inv_lo, inv_hi, inv_lo, spec, spec, p['hy_d'])
    in_specs = [col(0), col(1), col(2), cw(0), cw(1), cw(2), cb(0), cb(1), cb(2),
                fcos, fcos, fsin, fsin, icos, icos, isin, isin,
                pl.BlockSpec((None, fq, ct), lambda b, j, f, l: (l[0], f, j)),
                pl.BlockSpec((None, fq, ct), lambda b, j, f, l: (l[0], nf + f, j)),
                pl.BlockSpec((None, 1, ct), lambda b, j, f, l: (l[0], 0, j))]
    aliases = None
    if prev is not None:
        args = args + (prev,)
        in_specs = in_specs + [_ANY]
        aliases = {len(args): 0}
    return _call(
        _hyena_kernel, lidx, args,
        grid=(nb, nc, nf), in_specs=in_specs,
        out_specs=pl.BlockSpec((seq, ct), lambda b, j, f, l: (rb + b, j)),
        out_shape=jax.ShapeDtypeStruct((N_TOK, W), BF16),
        scratch=[pltpu.VMEM((seq, ct), F32), pltpu.VMEM((seq, ct), BF16),
                 pltpu.VMEM((seq, ct), BF16), pltpu.VMEM((seq, ct), F32)],
        sem=("parallel", "parallel", "arbitrary"), name=f"hyena_{seq}", aliases=aliases)


def _merge_kernel(l_ref, oa_ref, oc_ref, mr_ref, ga_ref, gc_ref, wa_ref, wc_ref, o_ref):
    m = (jax.nn.sigmoid(ga_ref[...]) * _dot(oa_ref[...], wa_ref[...]) + mr_ref[...]
         + jax.nn.sigmoid(gc_ref[...]) * _dot(oc_ref[...], wc_ref[...]))
    o_ref[...] = m.astype(BF16)


def _merge(lidx, o_a, o_c, m_r, proj, p):
    tm, tn = 1024, 512
    goff = GL_OFF // tn
    nd = D // tn
    row = lambda: pl.BlockSpec((tm, W), lambda i, j, l: (i, 0))
    gate = lambda c: pl.BlockSpec((tm, tn), lambda i, j, l: (i, goff + c * nd + j))
    wsp = lambda: pl.BlockSpec((None, W, tn), lambda i, j, l: (l[0], 0, j))
    return _call(
        _merge_kernel, lidx, (o_a, o_c, m_r, proj, proj, p['w_pa'], p['w_pc']),
        grid=(N_TOK // tm, nd),
        in_specs=[row(), row(), pl.BlockSpec((tm, tn), lambda i, j, l: (i, j)),
                  gate(0), gate(2), wsp(), wsp()],
        out_specs=pl.BlockSpec((tm, tn), lambda i, j, l: (i, j)),
        out_shape=jax.ShapeDtypeStruct((N_TOK, D), BF16),
        sem=("parallel", "parallel"), name="merge")


def _out_proj_kernel(l_ref, m_ref, w_ref, x_ref, ga_ref, o_ref, *, tm):
    row = _mod_row(pl.program_id(0), tm)
    o_ref[...] = x_ref[...] + ga_ref[pl.ds(row, 1), :] * _dot(m_ref[...], w_ref[...])


def _out_proj(lidx, merged, x, mod, w_out):
    tm, tn = 1024, 512
    nd = D // tn
    return _call(
        functools.partial(_out_proj_kernel, tm=tm), lidx, (merged, w_out, x, mod),
        grid=(N_TOK // tm, nd),
        in_specs=[pl.BlockSpec((tm, D), lambda i, j, l: (i, 0)),
                  pl.BlockSpec((None, D, tn), lambda i, j, l: (l[0], 0, j)),
                  pl.BlockSpec((tm, tn), lambda i, j, l: (i, j)),
                  pl.BlockSpec((None, 8, tn), lambda i, j, l: (l[0], 0, 2 * nd + j))],
        out_specs=pl.BlockSpec((tm, tn), lambda i, j, l: (i, j)),
        out_shape=jax.ShapeDtypeStruct((N_TOK, D), F32),
        sem=("parallel", "parallel"), name="out_proj")


def _ffn_kernel(l_ref, x_ref, g_ref, sh_ref, sc_ref, ga_ref, w1_ref, b1_ref, w2_ref, b2_ref,
                o_ref, h_ref, acc_ref, *, tm):
    j = pl.program_id(1)
    row = _mod_row(pl.program_id(0), tm)

    @pl.when(j == 0)
    def _():
        h = _modnorm(x_ref[...], g_ref[...], sc_ref[pl.ds(row, 1), :], sh_ref[pl.ds(row, 1), :])
        h_ref[...] = h.astype(BF16)
        acc_ref[...] = jnp.zeros_like(acc_ref)

    a = _dot(h_ref[...], w1_ref[...]) + b1_ref[...]
    a = jnp.square(jnp.maximum(a, 0.0))
    acc_ref[...] += _dot(a.astype(BF16), w2_ref[...])

    @pl.when(j == pl.num_programs(1) - 1)
    def _():
        o_ref[...] = x_ref[...] + ga_ref[pl.ds(row, 1), :] * (acc_ref[...] + b2_ref[...])


def _ffn(lidx, x, ln_g, mod, p):
    tm, tf = 512, 512
    return _call(
        functools.partial(_ffn_kernel, tm=tm), lidx,
        (x, ln_g, mod, mod, mod, p['w_ff1'], p['b_ff1'], p['w_ff2'], p['b_ff2']),
        grid=(N_TOK // tm, D_FF // tf),
        in_specs=[pl.BlockSpec((tm, D), lambda i, j, l: (i, 0)),
                  pl.BlockSpec((None, 1, D), lambda i, j, l: (l[0], 0, 0)),
                  pl.BlockSpec((None, 8, D), lambda i, j, l: (l[0], 0, 3)),
                  pl.BlockSpec((None, 8, D), lambda i, j, l: (l[0], 0, 4)),
                  pl.BlockSpec((None, 8, D), lambda i, j, l: (l[0], 0, 5)),
                  pl.BlockSpec((None, D, tf), lambda i, j, l: (l[0], 0, j)),
                  pl.BlockSpec((None, 1, tf), lambda i, j, l: (l[0], 0, j)),
                  pl.BlockSpec((None, tf, D), lambda i, j, l: (l[0], j, 0)),
                  pl.BlockSpec((None, 1, D), lambda i, j, l: (l[0], 0, 0))],
        out_specs=pl.BlockSpec((tm, D), lambda i, j, l: (i, 0)),
        out_shape=jax.ShapeDtypeStruct((N_TOK, D), F32),
        scratch=[pltpu.VMEM((tm, D), BF16), pltpu.VMEM((tm, D), F32)],
        sem=("parallel", "arbitrary"), name="ffn")


def _final_norm_kernel(x_ref, g_ref, o_ref):
    x = x_ref[...]
    o_ref[...] = x * lax.rsqrt(jnp.mean(x * x, -1, keepdims=True) + NORM_EPS) * g_ref[...]


def _final_norm(x, g):
    tm = 1024
    return pl.pallas_call(
        _final_norm_kernel,
        grid=(N_TOK // tm,),
        in_specs=[pl.BlockSpec((tm, D), lambda i: (i, 0)), pl.BlockSpec((1, D), lambda i: (0, 0))],
        out_specs=pl.BlockSpec((tm, D), lambda i: (i, 0)),
        out_shape=jax.ShapeDtypeStruct((N_TOK, D), F32),
        compiler_params=_params(("parallel",)), name="final_norm",
    )(x, g.reshape(1, D))


def kernel(x_prompt, x_sample, cache_k, cache_v, state_wkv, c, c_ctx, ln1_g, ln2_g, w_mod, b_mod, w_in, rpb, wkv_conv_w, wkv_conv_b, wkv_w0, wkv_w1, wkv_w2, wkv_a0, wkv_a1, wkv_a2, wkv_g1, wkv_g2, wkv_k_k, wkv_k_a, wkv_r_k, wkv_gn_g, wkv_gn_b, hy_conv_w, hy_conv_b, hy_f1, hy_fb1, hy_f2, hy_fb2, hy_freq, hy_f3, hy_d, w_pa, w_pr, w_pc, w_out, w_ff1, b_ff1, w_ff2, b_ff2, final_g):
    x = jnp.concatenate([x_prompt.reshape(N_CTX, D), x_sample.reshape(N_LAT, D)], 0)
    cvec = jnp.zeros((8, D), F32).at[0].set(c_ctx).at[1:1 + B_LAT].set(c)
    mod = _modulation(cvec, w_mod, b_mod)

    perm = (np.arange(H)[None, :] * DH + np.arange(DH)[:, None]).reshape(-1)
    pad_c = lambda a, n: jnp.pad(a, [(0, 0)] * (a.ndim - 1) + [(0, n - a.shape[-1])])
    pad_r = lambda a, n: jnp.pad(a, [(0, 0)] * (a.ndim - 2) + [(0, n - a.shape[-2]), (0, 0)])
    conv = jnp.concatenate([wkv_conv_w, wkv_conv_b[:, None, :]], 1)
    conv = jnp.transpose(conv.reshape(DEPTH, 4, 3, W)[..., perm], (0, 2, 3, 1))
    rowp = jnp.stack([wkv_w0[:, 0], wkv_w0[:, 1], wkv_a0[:, 0], wkv_a0[:, 1]], -1)[:, perm]
    post = jnp.stack([wkv_r_k.reshape(DEPTH, W), wkv_gn_g, wkv_gn_b, jnp.zeros((DEPTH, W), F32)],
                     -1)[:, perm]
    p = {
        'wkv_conv': conv, 'wkv_rowp': rowp, 'wkv_post': post,
        'wkv_w2t': jnp.swapaxes(wkv_w2[..., perm], -1, -2),
        'wkv_a2t': jnp.swapaxes(wkv_a2[..., perm], -1, -2),
        'wkv_g2t': jnp.swapaxes(wkv_g2[..., perm], -1, -2),
        'kk_chain': _head_param(wkv_k_k, 2 * B_CTX), 'ka_chain': _head_param(wkv_k_a, 2 * B_CTX),
        'hy_conv_w': hy_conv_w, 'hy_conv_b': hy_conv_b.reshape(DEPTH, 1, 3 * W),
        'hy_d': hy_d.reshape(DEPTH, 1, W),
        'hy_f1p': pad_c(pad_r(hy_f1, FH_PAD), FH_PAD), 'hy_fb1p': pad_c(hy_fb1, FH_PAD).reshape(DEPTH, 1, FH_PAD),
        'hy_f2p': pad_c(pad_r(hy_f2, FH_PAD), FH_PAD), 'hy_fb2p': pad_c(hy_fb2, FH_PAD).reshape(DEPTH, 1, FH_PAD),
        'hy_freqp': pad_c(hy_freq, FH_PAD).reshape(DEPTH, 1, FH_PAD), 'hy_f3p': pad_r(hy_f3, FH_PAD),
        'w_pa': w_pa.astype(BF16), 'w_pr': w_pr[:, perm].astype(BF16), 'w_pc': w_pc.astype(BF16),
        'w_ff1': w_ff1.astype(BF16), 'b_ff1': b_ff1.reshape(DEPTH, 1, D_FF),
        'w_ff2': w_ff2.astype(BF16), 'b_ff2': b_ff2.reshape(DEPTH, 1, D),
    }
    w_a = jnp.concatenate([w_in[..., :3 * W], w_in[..., 6 * W:]], -1).astype(BF16)
    rkv = w_in[..., 3 * W:6 * W].reshape(DEPTH, D, 3, W)[..., perm].reshape(DEPTH, D, 3 * W)
    w_t = jnp.concatenate(
        [rkv, wkv_w1[:, 0], wkv_w1[:, 1], wkv_a1[:, 0], wkv_a1[:, 1], wkv_g1,
         jnp.zeros((DEPTH, D, N_LORA - 4 * LORA - LORA_G), F32)], -1)
    w_t = jnp.swapaxes(w_t, 1, 2).astype(BF16)
    w_out_b = w_out.astype(BF16)
    ln1 = ln1_g.reshape(DEPTH, 1, D)
    ln2 = ln2_g.reshape(DEPTH, 1, D)
    bias_tab = _bias_table(rpb)
    ck = cache_k.reshape(B_LAT, DEPTH, L_CTX, W)
    cv = cache_v.reshape(B_LAT, DEPTH, L_CTX, W)
    s0_lat = jnp.transpose(state_wkv, (1, 5, 4, 0, 2, 3)).reshape(DEPTH, DH, DH, B_LAT * 2 * H)
    s0_ctx = jnp.zeros((DH, DH, B_CTX * 2 * H), F32)

    hy = {}
    for seq in (L_CTX, L_LAT):
        mats = _dft_mats(seq)
        filt = _hyena_filter(p, seq)
        hy[seq] = (mats, _spectrum(mats[0], mats[1], filt, seq))

    def layer(x, l):
        lidx = jnp.reshape(l, (1,)).astype(jnp.int32)
        proj = _in_proj(lidx, x, ln1, mod, w_a, transposed=False)
        proj_t = _in_proj(lidx, x, ln1, mod, w_t, transposed=True)
        oa = _attn_lat(lidx, proj, ck, cv, bias_tab, _attn_ctx(lidx, proj))
        m_r, s_ctx = _rwkv_branch(lidx, proj, proj_t, p, s0_ctx, None, nb=B_CTX, seq=L_CTX,
                                  rb_t=0, rb_tok=0)
        s0 = lax.dynamic_index_in_dim(s0_lat, l, 0, keepdims=False)
        m_r, _ = _rwkv_branch(lidx, proj, proj_t, p, s0, m_r, nb=B_LAT, seq=L_LAT,
                              rb_t=N_CTX // L_LAT, rb_tok=N_CTX // 256)
        oc = _hyena(lidx, proj, p, *hy[L_CTX], None, nb=B_CTX, seq=L_CTX, rb=0)
        oc = _hyena(lidx, proj, p, *hy[L_LAT], oc, nb=B_LAT, seq=L_LAT, rb=N_CTX // L_LAT)
        merged = _merge(lidx, oa, oc, m_r, proj, p)
        x = _out_proj(lidx, merged, x, mod, w_out_b)
        x = _ffn(lidx, x, ln2, mod, p)
        k_new = proj[:N_CTX, W:2 * W]
        v_new = proj[:N_CTX, 2 * W:3 * W]
        return x, (k_new, v_new, s_ctx)

    x, (ks, vs, ss) = lax.scan(layer, x, jnp.arange(DEPTH, dtype=jnp.int32))
    y = _final_norm(x, final_g)
    y_prompt = y[:N_CTX].reshape(B_CTX, L_CTX, D)
    y_sample = y[N_CTX:].reshape(B_LAT, L_LAT, D)
    new_k = jnp.transpose(ks.reshape(DEPTH, B_CTX, L_CTX, H, DH), (1, 0, 2, 3, 4))
    new_v = jnp.transpose(vs.reshape(DEPTH, B_CTX, L_CTX, H, DH), (1, 0, 2, 3, 4))
    new_s = jnp.transpose(ss.reshape(DEPTH, DH, DH, B_CTX, 2, H), (3, 0, 4, 5, 2, 1))
    return (y_prompt, y_sample, new_k, new_v, new_s)
```

```python
import functools
import math

import numpy as np
import jax
import jax.numpy as jnp
from jax import lax
from jax.experimental import pallas as pl
from jax.experimental.pallas import tpu as pltpu

F32 = jnp.float32
BF16 = jnp.bfloat16

D = 2048
DEPTH = 4
B_CTX, L_CTX = 32, 256
B_LAT, L_LAT = 4, 2048
N_CTX = B_CTX * L_CTX
N_LAT = B_LAT * L_LAT
N_TOK = N_CTX + N_LAT
H = 16
DH = 64
W = H * DH
GRID_W = 64
ROWS = L_LAT // GRID_W
WIN_R, WIN_C = 8, 16
NL = WIN_R * GRID_W
LORA = 64
LORA_G = 128
N_LORA = 512
N_A = 6 * W + 3 * D
HY_OFF = 3 * W
GL_OFF = 6 * W
N_T = 3 * W + N_LORA
D_FF = 4 * D
N_MOD = 6
NORM_EPS = 1e-6
GN_EPS = 64e-5
NEG_INF = -1e30
POS_BANDS = 16
FH_PAD = 128
LANES = 128
GB = LANES // (2 * H)
TCH = 128
VMEM_LIMIT = 56 * 1024 * 1024


def _params(sem):
    return pltpu.CompilerParams(dimension_semantics=sem, vmem_limit_bytes=VMEM_LIMIT)


def _call(kernel, lidx, args, *, grid, in_specs, out_specs, out_shape, scratch=(), sem, name,
          aliases=None):
    gs = pltpu.PrefetchScalarGridSpec(num_scalar_prefetch=1, grid=grid, in_specs=in_specs,
                                      out_specs=out_specs, scratch_shapes=list(scratch))
    return pl.pallas_call(kernel, grid_spec=gs, out_shape=out_shape,
                          compiler_params=_params(sem), name=name,
                          input_output_aliases=aliases or {})(lidx, *args)


_ANY = pl.BlockSpec(memory_space=pl.ANY)


def _mod_row(i, tm):
    start = i * tm
    return jnp.where(start < N_CTX, 0, 1 + (start - N_CTX) // L_LAT)


def _modnorm(x, g, sc, sh):
    y = x * lax.rsqrt(jnp.mean(x * x, -1, keepdims=True) + NORM_EPS)
    return (y * g) * (1.0 + sc) + sh


def _softplus(x):
    return jnp.maximum(x, 0.0) + jnp.log1p(jnp.exp(-jnp.abs(x)))


def _dot(a, b):
    return jnp.dot(a, b, preferred_element_type=F32)


def _dot_nt(a, b):
    return lax.dot_general(a, b, (((1,), (1,)), ((), ())), preferred_element_type=F32)


def _dot_tn(a, b):
    return lax.dot_general(a, b, (((0,), (0,)), ((), ())), preferred_element_type=F32)


def _split(x):
    hi = x.astype(BF16)
    lo = (x - hi.astype(F32)).astype(BF16)
    return hi, lo


def _dot3(ah, al, bh, bl):
    return _dot(ah, bh) + (_dot(al, bh) + _dot(ah, bl))


def _mod_kernel(c_ref, w_ref, b_ref, o_ref):
    c = c_ref[...]
    s = c * jax.nn.sigmoid(c)
    o_ref[...] = _dot(s.astype(BF16), w_ref[...].astype(BF16)) + b_ref[...]


def _modulation(cvec8, w_mod, b_mod):
    tn = 1024
    return pl.pallas_call(
        _mod_kernel,
        grid=(DEPTH, N_MOD * D // tn),
        in_specs=[pl.BlockSpec((8, D), lambda l, j: (0, 0)),
                  pl.BlockSpec((None, D, tn), lambda l, j: (l, 0, j)),
                  pl.BlockSpec((None, 1, tn), lambda l, j: (l, 0, j))],
        out_specs=pl.BlockSpec((None, 8, tn), lambda l, j: (l, 0, j)),
        out_shape=jax.ShapeDtypeStruct((DEPTH, 8, N_MOD * D), F32),
        compiler_params=_params(("parallel", "parallel")), name="modulation",
    )(cvec8, w_mod, b_mod.reshape(DEPTH, 1, N_MOD * D))


def _in_proj_kernel(l_ref, x_ref, g_ref, sh_ref, sc_ref, w_ref, o_ref, h_ref, *, tm, transposed):
    @pl.when(pl.program_id(1) == 0)
    def _():
        row = _mod_row(pl.program_id(0), tm)
        h = _modnorm(x_ref[...], g_ref[...], sc_ref[pl.ds(row, 1), :], sh_ref[pl.ds(row, 1), :])
        h_ref[...] = h.astype(BF16)

    if transposed:
        o_ref[...] = _dot_nt(w_ref[...], h_ref[...])
    else:
        o_ref[...] = _dot(h_ref[...], w_ref[...])


def _in_proj(lidx, x, ln_g, mod, w, *, transposed):
    tm = 1024
    tn = 896 if transposed else 512
    if transposed:
        n_out = w.shape[1]
        w_spec = pl.BlockSpec((None, tn, D), lambda i, j, l: (l[0], j, 0))
        o_spec = pl.BlockSpec((tn, tm), lambda i, j, l: (j, i))
        o_shape = jax.ShapeDtypeStruct((n_out, N_TOK), F32)
    else:
        n_out = w.shape[2]
        w_spec = pl.BlockSpec((None, D, tn), lambda i, j, l: (l[0], 0, j))
        o_spec = pl.BlockSpec((tm, tn), lambda i, j, l: (i, j))
        o_shape = jax.ShapeDtypeStruct((N_TOK, n_out), F32)
    return _call(
        functools.partial(_in_proj_kernel, tm=tm, transposed=transposed), lidx, (x, ln_g, mod, mod, w),
        grid=(N_TOK // tm, n_out // tn),
        in_specs=[pl.BlockSpec((tm, D), lambda i, j, l: (i, 0)),
                  pl.BlockSpec((None, 1, D), lambda i, j, l: (l[0], 0, 0)),
                  pl.BlockSpec((None, 8, D), lambda i, j, l: (l[0], 0, 0)),
                  pl.BlockSpec((None, 8, D), lambda i, j, l: (l[0], 0, 1)),
                  w_spec],
        out_specs=o_spec, out_shape=o_shape,
        scratch=[pltpu.VMEM((tm, D), BF16)],
        sem=("parallel", "arbitrary"), name="in_proj_t" if transposed else "in_proj")


ATT_CH = 8


def _attn_ctx_kernel(l_ref, q_ref, k_ref, v_ref, o_ref, s_ref):
    scale = DH ** -0.5
    for hh in range(ATT_CH):
        sl = slice(hh * DH, (hh + 1) * DH)
        q = (q_ref[:, sl] * scale).astype(BF16)
        s_ref[hh] = _dot_nt(q, k_ref[:, sl].astype(BF16))
    outs = []
    for hh in range(ATT_CH):
        sl = slice(hh * DH, (hh + 1) * DH)
        s = s_ref[hh]
        p = jnp.exp(s - jnp.max(s, -1, keepdims=True))
        den = jnp.sum(p, -1, keepdims=True)
        outs.append(_dot(p.astype(BF16), v_ref[:, sl].astype(BF16)) / den)
    o_ref[...] = jnp.concatenate(outs, -1).astype(BF16)


def _attn_ctx(lidx, proj):
    wb = ATT_CH * DH
    nq = W // wb
    return _call(
        _attn_ctx_kernel, lidx, (proj, proj, proj),
        grid=(B_CTX, nq),
        in_specs=[pl.BlockSpec((L_CTX, wb), lambda b, p, l: (b, p)),
                  pl.BlockSpec((L_CTX, wb), lambda b, p, l: (b, nq + p)),
                  pl.BlockSpec((L_CTX, wb), lambda b, p, l: (b, 2 * nq + p))],
        out_specs=pl.BlockSpec((L_CTX, wb), lambda b, p, l: (b, p)),
        out_shape=jax.ShapeDtypeStruct((N_TOK, W), BF16),
        scratch=[pltpu.VMEM((ATT_CH, L_CTX, L_CTX), F32)],
        sem=("parallel", "parallel"), name="attn_ctx")


ATT_RB = 8


def _attn_lat_kernel(l_ref, q_ref, k_ref, v_ref, kc_ref, vc_ref, bias_ref, alias_ref, o_ref,
                     kb_ref, vb_ref, kcb_ref, vcb_ref, s_ref):
    scale = DH ** -0.5
    for hh in range(2):
        sl = slice(hh * DH, (hh + 1) * DH)
        kb_ref[hh] = k_ref[:, sl].astype(BF16)
        vb_ref[hh] = v_ref[:, sl].astype(BF16)
        kcb_ref[hh] = kc_ref[:, sl].astype(BF16)
        vcb_ref[hh] = vc_ref[:, sl].astype(BF16)

    def window(r):
        r0 = jnp.clip(r - WIN_R // 2, 0, ROWS - WIN_R)
        return r - r0, pl.multiple_of(r * GRID_W, GRID_W), pl.multiple_of(r0 * GRID_W, GRID_W)

    def row_block(rr, carry):
        for i in range(ATT_RB):
            d, qrow, krow = window(rr * ATT_RB + i)
            for hh in range(2):
                sl = slice(hh * DH, (hh + 1) * DH)
                q = (q_ref[pl.ds(qrow, GRID_W), sl] * scale).astype(BF16)
                s_ref[2 * i + hh, :, :NL] = _dot_nt(q, kb_ref[hh, pl.ds(krow, NL), :]) + bias_ref[hh, d]
                s_ref[2 * i + hh, :, NL:] = _dot_nt(q, kcb_ref[hh])
        for i in range(ATT_RB):
            d, qrow, krow = window(rr * ATT_RB + i)
            outs = []
            for hh in range(2):
                s = s_ref[2 * i + hh]
                p = jnp.exp(s - jnp.max(s, -1, keepdims=True))
                den = jnp.sum(p, -1, keepdims=True)
                pb = p.astype(BF16)
                o = _dot(pb[:, :NL], vb_ref[hh, pl.ds(krow, NL), :]) + _dot(pb[:, NL:], vcb_ref[hh])
                outs.append(o / den)
            o_ref[pl.ds(qrow, GRID_W), :] = jnp.concatenate(outs, -1).astype(BF16)
        return carry

    lax.fori_loop(0, ROWS // ATT_RB, row_block, 0)


def _attn_lat(lidx, proj, cache_k, cache_v, bias_tab, o_ctx):
    nq = W // LANES
    rb = N_CTX // L_LAT
    return _call(
        _attn_lat_kernel, lidx, (proj, proj, proj, cache_k, cache_v, bias_tab, o_ctx),
        grid=(B_LAT, nq),
        in_specs=[pl.BlockSpec((L_LAT, LANES), lambda b, p, l: (rb + b, p)),
                  pl.BlockSpec((L_LAT, LANES), lambda b, p, l: (rb + b, nq + p)),
                  pl.BlockSpec((L_LAT, LANES), lambda b, p, l: (rb + b, 2 * nq + p)),
                  pl.BlockSpec((None, None, L_CTX, LANES), lambda b, p, l: (b, l[0], 0, p)),
                  pl.BlockSpec((None, None, L_CTX, LANES), lambda b, p, l: (b, l[0], 0, p)),
                  pl.BlockSpec((None, 2, WIN_R, GRID_W, NL), lambda b, p, l: (l[0], p, 0, 0, 0)),
                  _ANY],
        out_specs=pl.BlockSpec((L_LAT, LANES), lambda b, p, l: (rb + b, p)),
        out_shape=jax.ShapeDtypeStruct((N_TOK, W), BF16),
        scratch=[pltpu.VMEM((2, L_LAT, DH), BF16), pltpu.VMEM((2, L_LAT, DH), BF16),
                 pltpu.VMEM((2, L_CTX, DH), BF16), pltpu.VMEM((2, L_CTX, DH), BF16),
                 pltpu.VMEM((2 * ATT_RB, GRID_W, NL + L_CTX), F32)],
        sem=("parallel", "parallel"), name="attn_lat", aliases={7: 0})


def _bias_table(rpb):
    cq = np.arange(GRID_W)
    c0 = np.clip(cq - WIN_C // 2, 0, GRID_W - WIN_C)
    ck = np.arange(GRID_W)
    col_ok = (ck[None, :] >= c0[:, None]) & (ck[None, :] < c0[:, None] + WIN_C)
    dc = np.clip(ck[None, :] - cq[:, None], -(WIN_C - 1), WIN_C - 1) + (WIN_C - 1)
    onehot = (dc.reshape(-1)[None, :] == np.arange(2 * WIN_C - 1)[:, None]).astype(np.float32)
    cols = jnp.einsum('lhrc,cx->lhrx', rpb, jnp.asarray(onehot), precision=lax.Precision.HIGHEST)
    cols = cols.reshape(DEPTH, H, 2 * WIN_R - 1, GRID_W, GRID_W)
    tab = jnp.stack([cols[:, :, WIN_R - 1 - d:2 * WIN_R - 1 - d] for d in range(WIN_R)], 2)
    tab = jnp.transpose(tab, (0, 1, 2, 4, 3, 5))
    tab = jnp.where(col_ok[None, None, None, :, None, :], tab, NEG_INF)
    return tab.reshape(DEPTH, H, WIN_R, GRID_W, NL).astype(F32)


def _conv3(x, w_ref, b_ref):
    n = x.shape[0]
    row = lax.broadcasted_iota(jnp.int32, x.shape, 0)
    prev = jnp.where(row == 0, 0.0, pltpu.roll(x, 1, 0))
    nxt = jnp.where(row == n - 1, 0.0, pltpu.roll(x, n - 1, 0))
    return prev * w_ref[0:1, :] + x * w_ref[1:2, :] + nxt * w_ref[2:3, :] + b_ref[...]


def _conv3_t(x, p):
    n = x.shape[1]
    lane = lax.broadcasted_iota(jnp.int32, x.shape, 1)
    prev = jnp.where(lane == 0, 0.0, pltpu.roll(x, 1, 1))
    nxt = jnp.where(lane == n - 1, 0.0, pltpu.roll(x, n - 1, 1))
    return prev * p[:, 0:1] + x * p[:, 1:2] + nxt * p[:, 2:3] + p[:, 3:4]


def _wkv_prep_kernel(l_ref, r_ref, k_ref, v_ref, lora_ref, cp_ref, rp_ref, w2_ref, a2_ref, g2_ref,
                     ro_ref, ko_ref, vo_ref, d0_ref, d1_ref, a0o_ref, a1o_ref, go_ref):
    ro_ref[...] = _conv3_t(r_ref[...], cp_ref[0])
    ko_ref[...] = _conv3_t(k_ref[...], cp_ref[1])
    vo_ref[...] = _conv3_t(v_ref[...], cp_ref[2])
    rp = rp_ref[...]
    for e, (d_ref, ao_ref) in enumerate(((d0_ref, a0o_ref), (d1_ref, a1o_ref))):
        lw = jnp.tanh(lora_ref[e * LORA:(e + 1) * LORA, :]).astype(BF16)
        la = lora_ref[(2 + e) * LORA:(3 + e) * LORA, :].astype(BF16)
        w_log = rp[:, e:e + 1] + _dot(w2_ref[e].astype(BF16), lw)
        d_ref[...] = jnp.exp(-jnp.exp(-_softplus(-w_log) - 0.5))
        ao_ref[...] = jax.nn.sigmoid(rp[:, 2 + e:3 + e] + _dot(a2_ref[e].astype(BF16), la))
    lg = jax.nn.sigmoid(lora_ref[4 * LORA:4 * LORA + LORA_G, :]).astype(BF16)
    go_ref[...] = _dot(g2_ref[...].astype(BF16), lg)


def _wkv_prep(lidx, proj_t, p, *, nb, seq, rb):
    ct = 512 if seq <= 256 else 128
    nc = W // ct

    def row(c):
        return pl.BlockSpec((ct, seq), lambda b, j, l: (c * nc + j, rb + b))

    out = pl.BlockSpec((None, ct, seq), lambda b, j, l: (b, j, 0))
    return _call(
        _wkv_prep_kernel, lidx,
        (proj_t, proj_t, proj_t, proj_t, p['wkv_conv'], p['wkv_rowp'], p['wkv_w2t'], p['wkv_a2t'],
         p['wkv_g2t']),
        grid=(nb, nc),
        in_specs=[row(0), row(1), row(2),
                  pl.BlockSpec((N_LORA, seq), lambda b, j, l: (3 * W // N_LORA, rb + b)),
                  pl.BlockSpec((None, 3, ct, 4), lambda b, j, l: (l[0], 0, j, 0)),
                  pl.BlockSpec((None, ct, 4), lambda b, j, l: (l[0], j, 0)),
                  pl.BlockSpec((None, 2, ct, LORA), lambda b, j, l: (l[0], 0, j, 0)),
                  pl.BlockSpec((None, 2, ct, LORA), lambda b, j, l: (l[0], 0, j, 0)),
                  pl.BlockSpec((None, ct, LORA_G), lambda b, j, l: (l[0], j, 0))],
        out_specs=[out] * 8,
        out_shape=[jax.ShapeDtypeStruct((nb, W, seq), F32)] * 8,
        sem=("parallel", "parallel"), name=f"wkv_prep_{seq}")


def _to_chains_kernel(l_ref, *refs):
    xs, o_ref = refs[:2 * GB], refs[2 * GB]
    for k in range(DH):
        m = jnp.concatenate([x[k * H:(k + 1) * H, :] for x in xs], 0)
        o_ref[:, k, :] = m.T


def _to_chains(lidx, x0, x1, *, nb, seq):
    def src(i):
        return pl.BlockSpec((None, W, TCH), lambda g, t, l: (GB * g + i // 2, 0, t))

    args = tuple(x0 if i % 2 == 0 else x1 for i in range(2 * GB))
    return _call(
        _to_chains_kernel, lidx, args,
        grid=(nb // GB, seq // TCH),
        in_specs=[src(i) for i in range(2 * GB)],
        out_specs=pl.BlockSpec((TCH, DH, LANES), lambda g, t, l: (t, 0, g)),
        out_shape=jax.ShapeDtypeStruct((seq, DH, nb * 2 * H), F32),
        sem=("parallel", "parallel"), name=f"to_chains_{seq}")


def _wkv_kernel(l_ref, rf_ref, kf_ref, vf_ref, wf_ref, af_ref, rb_ref, kb_ref, vb_ref, wb_ref, ab_ref,
                kkp_ref, kap_ref, s0_ref, yf_ref, yb_ref, s_ref, st_ref, *, tb):
    @pl.when(pl.program_id(1) == 0)
    def _():
        s_ref[...] = s0_ref[...]

    kkp = kkp_ref[...]
    kap = kap_ref[...]
    lane = lax.broadcasted_iota(jnp.int32, (DH, LANES), 1)
    bwd = (lane // H) % 2 == 1
    kc = 16
    zero = jnp.zeros((DH, LANES), F32)

    def step(t, carry):
        tr = tb - 1 - t
        pick = lambda f_ref, b_ref: jnp.where(bwd, b_ref[tr], f_ref[t])
        kt = pick(kf_ref, kb_ref)
        at = pick(af_ref, ab_ref)
        vt = pick(vf_ref, vb_ref)
        kk = kt * kkp
        kk = kk * lax.rsqrt(jnp.sum(kk * kk, 0, keepdims=True) + 1e-12)
        st_ref[0] = kk
        st_ref[1] = kk * at
        st_ref[2] = kt * (1.0 + (at - 1.0) * kap)
        st_ref[3] = pick(wf_ref, wb_ref)
        st_ref[4] = pick(rf_ref, rb_ref)

        def sa_body(c, acc):
            a0, a1 = acc
            for j in range(kc):
                k = c * kc + j
                term = s_ref[k] * st_ref[0, pl.ds(k, 1), :]
                if j % 2 == 0:
                    a0 = a0 + term
                else:
                    a1 = a1 + term
            return a0, a1

        a0, a1 = lax.fori_loop(0, DH // kc, sa_body, (zero, zero))
        sa = -(a0 + a1)

        def up_body(c, acc):
            y0, y1 = acc
            for j in range(kc):
                k = c * kc + j
                sk = ((s_ref[k] * st_ref[3, pl.ds(k, 1), :] + sa * st_ref[1, pl.ds(k, 1), :])
                      + vt * st_ref[2, pl.ds(k, 1), :])
                s_ref[k] = sk
                term = sk * st_ref[4, pl.ds(k, 1), :]
                if j % 2 == 0:
                    y0 = y0 + term
                else:
                    y1 = y1 + term
            return y0, y1

        y0, y1 = lax.fori_loop(0, DH // kc, up_body, (zero, zero))
        y = y0 + y1
        yf_ref[t] = y
        yb_ref[tr] = y
        return carry

    lax.fori_loop(0, tb, step, 0)


def _wkv(lidx, r, k, v, w, a, kkp, kap, s0, *, seq):
    tb = 32
    chains = r.shape[-1]
    nt = seq // tb
    f_spec = pl.BlockSpec((tb, DH, LANES), lambda g, t, l: (t, 0, g))
    b_spec = pl.BlockSpec((tb, DH, LANES), lambda g, t, l: (nt - 1 - t, 0, g))
    par_spec = pl.BlockSpec((None, DH, LANES), lambda g, t, l: (l[0], 0, g))
    st_spec = pl.BlockSpec((DH, DH, LANES), lambda g, t, l: (0, 0, g))
    y_shape = jax.ShapeDtypeStruct((seq, DH, chains), F32)
    return _call(
        functools.partial(_wkv_kernel, tb=tb), lidx, (r, k, v, w, a, r, k, v, w, a, kkp, kap, s0),
        grid=(chains // LANES, nt),
        in_specs=[f_spec] * 5 + [b_spec] * 5 + [par_spec, par_spec, st_spec],
        out_specs=[f_spec, b_spec, st_spec],
        out_shape=[y_shape, y_shape, jax.ShapeDtypeStruct((DH, DH, chains), F32)],
        scratch=[pltpu.VMEM((5, DH, LANES), F32)],
        sem=("parallel", "arbitrary"), name=f"wkv_{seq}")


def _from_chains_kernel(l_ref, yf_ref, yb_ref, o_ref):
    for v in range(DH):
        tf = yf_ref[:, v, :].T
        tb = yb_ref[:, v, :].T
        for b in range(GB):
            lo = b * 2 * H
            o_ref[b, v * H:(v + 1) * H, :] = tf[lo:lo + H] + tb[lo + H:lo + 2 * H]


def _from_chains(lidx, yf, yb, *, nb, seq):
    y_spec = pl.BlockSpec((TCH, DH, LANES), lambda g, t, l: (t, 0, g))
    return _call(
        _from_chains_kernel, lidx, (yf, yb),
        grid=(nb // GB, seq // TCH),
        in_specs=[y_spec, y_spec],
        out_specs=pl.BlockSpec((GB, W, TCH), lambda g, t, l: (g, 0, t)),
        out_shape=jax.ShapeDtypeStruct((nb, W, seq), F32),
        sem=("parallel", "parallel"), name=f"from_chains_{seq}")


def _wkv_post_kernel(l_ref, y_ref, r_ref, k_ref, v_ref, g_ref, gate_ref, par_ref, w_ref, *rest):
    o_ref = rest[-1]
    t = y_ref.shape[-1]
    par = par_ref[...]
    y = y_ref[...].reshape(DH, H, t)
    mu = jnp.mean(y, 0, keepdims=True)
    yc = y - mu
    var = jnp.mean(yc * yc, 0, keepdims=True)
    yn = (yc * lax.rsqrt(var + GN_EPS)).reshape(W, t) * par[:, 1:2] + par[:, 2:3]
    rk = (r_ref[...] * k_ref[...] * par[:, 0:1]).reshape(DH, H, t)
    bonus = jnp.broadcast_to(jnp.sum(rk, 0, keepdims=True), (DH, H, t)).reshape(W, t) * v_ref[...]
    o = ((yn + bonus) * g_ref[...]).astype(BF16)
    o_ref[...] = jax.nn.sigmoid(gate_ref[...]) * _dot_tn(o, w_ref[...])


def _wkv_post(lidx, ysum, r, k, v, g, proj, p, prev, *, nb, seq, rb):
    tt = 256
    nt = seq // tt
    t_spec = pl.BlockSpec((None, W, tt), lambda b, t, l: (b, 0, t))
    tok = lambda b, t: rb + b * nt + t
    args = (ysum, r, k, v, g, proj, p['wkv_post'], p['w_pr'])
    in_specs = [t_spec] * 5 + [
        pl.BlockSpec((tt, D), lambda b, t, l: (tok(b, t), (GL_OFF + D) // D)),
        pl.BlockSpec((None, W, 4), lambda b, t, l: (l[0], 0, 0)),
        pl.BlockSpec((None, W, D), lambda b, t, l: (l[0], 0, 0))]
    aliases = None
    if prev is not None:
        args = args + (prev,)
        in_specs = in_specs + [_ANY]
        aliases = {len(args): 0}
    return _call(
        _wkv_post_kernel, lidx, args,
        grid=(nb, nt), in_specs=in_specs,
        out_specs=pl.BlockSpec((tt, D), lambda b, t, l: (tok(b, t), 0)),
        out_shape=jax.ShapeDtypeStruct((N_TOK, D), F32),
        sem=("parallel", "parallel"), name=f"wkv_post_{seq}", aliases=aliases)


def _head_param(p, reps):
    t = jnp.transpose(p.reshape(DEPTH, H, DH), (0, 2, 1))
    return jnp.tile(t, (1, 1, reps))


def _rwkv_branch(lidx, proj, proj_t, p, s0_chain, prev, *, nb, seq, rb_t, rb_tok):
    r, k, v, d0, d1, a0, a1, g = _wkv_prep(lidx, proj_t, p, nb=nb, seq=seq, rb=rb_t)
    tc = functools.partial(_to_chains, lidx, nb=nb, seq=seq)
    chains = nb * 2 * H
    yf, yb, s_fin = _wkv(lidx, tc(r, r), tc(k, k), tc(v, v), tc(d0, d1), tc(a0, a1),
                         p['kk_chain'][:, :, :chains], p['ka_chain'][:, :, :chains], s0_chain, seq=seq)
    ysum = _from_chains(lidx, yf, yb, nb=nb, seq=seq)
    m_r = _wkv_post(lidx, ysum, r, k, v, g, proj, p, prev, nb=nb, seq=seq, rb=rb_tok)
    return m_r, s_fin


def _pos_features(seq):
    t = np.linspace(0.0, 1.0, seq, dtype=np.float32)[:, None]
    w = 2.0 * np.pi * np.arange(seq, dtype=np.float32)[:, None] / seq
    f = np.linspace(1e-4, POS_BANDS - 1, POS_BANDS, dtype=np.float32)[None, :]
    z = np.concatenate([t, np.cos(f * w), -np.sin(f * w)], -1).astype(np.float32)
    zp = np.zeros((seq, FH_PAD), np.float32)
    zp[:, :z.shape[1]] = z
    dist = (np.abs(np.arange(seq) - seq // 2).astype(np.float32) / seq)[:, None]
    deltas = np.abs(np.linspace(math.log(1e-2) / 1.5, math.log(1e-2) / 0.3, W,
                                dtype=np.float32))[None, :]
    return zp, dist, deltas


def _filter_kernel(zp_ref, dist_ref, del_ref, f1_ref, b1_ref, f2_ref, b2_ref, fr_ref, f3_ref, o_ref):
    hi = lax.Precision.HIGHEST
    fr = fr_ref[...]
    t = jnp.sin(fr * (jnp.dot(zp_ref[...], f1_ref[...], precision=hi,
                              preferred_element_type=F32) + b1_ref[...]))
    t = jnp.sin(fr * (jnp.dot(t, f2_ref[...], precision=hi, preferred_element_type=F32) + b2_ref[...]))
    filt = jnp.dot(t, f3_ref[...], precision=hi, preferred_element_type=F32)
    filt = filt * jnp.exp(-dist_ref[...] * del_ref[...])
    o_ref[...] = filt / (jnp.sum(jnp.abs(filt), 0, keepdims=True) + 1e-6)


def _hyena_filter(p, seq):
    ct = 256
    zp, dist, deltas = _pos_features(seq)
    full = lambda shape: pl.BlockSpec((None,) + shape, lambda l, j: (l,) + (0,) * len(shape))
    return pl.pallas_call(
        _filter_kernel,
        grid=(DEPTH, W // ct),
        in_specs=[pl.BlockSpec((seq, FH_PAD), lambda l, j: (0, 0)),
                  pl.BlockSpec((seq, 1), lambda l, j: (0, 0)),
                  pl.BlockSpec((1, ct), lambda l, j: (0, j)),
                  full((FH_PAD, FH_PAD)), full((1, FH_PAD)), full((FH_PAD, FH_PAD)),
                  full((1, FH_PAD)), full((1, FH_PAD)),
                  pl.BlockSpec((None, FH_PAD, ct), lambda l, j: (l, 0, j))],
        out_specs=pl.BlockSpec((None, seq, ct), lambda l, j: (l, 0, j)),
        out_shape=jax.ShapeDtypeStruct((DEPTH, seq, W), F32),
        compiler_params=_params(("parallel", "parallel")), name=f"hyena_filter_{seq}",
    )(jnp.asarray(zp), jnp.asarray(dist), jnp.asarray(deltas),
      p['hy_f1p'], p['hy_fb1p'], p['hy_f2p'], p['hy_fb2p'], p['hy_freqp'], p['hy_f3p'])


def _freq_block(seq):
    return min(seq, 256)


def _dft_mats(seq):
    n = 2 * seq
    fq = _freq_block(seq)
    k = jnp.arange(seq, dtype=jnp.int32)
    t = jnp.arange(seq, dtype=jnp.int32)
    ph = ((2 * k[:, None] + 1) * t[None, :]) % (2 * n)
    ang = ph.astype(F32) * np.float32(np.pi / n)
    fwd = jnp.stack([jnp.cos(ang).reshape(seq // fq, fq, seq),
                     jnp.sin(ang).reshape(seq // fq, fq, seq)], 1).reshape(2 * seq, seq)
    m = t + seq // 2
    ph2 = ((2 * k[None, :] + 1) * m[:, None]) % (2 * n)
    ang2 = ph2.astype(F32) * np.float32(np.pi / n)
    inv = jnp.stack([jnp.cos(ang2).reshape(seq, seq // fq, fq),
                     jnp.sin(ang2).reshape(seq, seq // fq, fq)], 2).reshape(seq, 2 * seq)
    inv = inv * np.float32(2.0 / n)
    fwd_hi, fwd_lo = _split(fwd)
    return fwd_hi, fwd_lo, inv.astype(BF16)


def _spectrum_kernel(ah_ref, al_ref, b_ref, o_ref):
    bh, bl = _split(b_ref[...])
    o_ref[...] = _dot3(ah_ref[...], al_ref[...], bh, bl)


def _spectrum(fwd_hi, fwd_lo, filt, seq):
    tmm, tn = 256, 512
    return pl.pallas_call(
        _spectrum_kernel,
        grid=(DEPTH, 2 * seq // tmm, W // tn),
        in_specs=[pl.BlockSpec((tmm, seq), lambda l, i, j: (i, 0)),
                  pl.BlockSpec((tmm, seq), lambda l, i, j: (i, 0)),
                  pl.BlockSpec((None, seq, tn), lambda l, i, j: (l, 0, j))],
        out_specs=pl.BlockSpec((None, tmm, tn), lambda l, i, j: (l, i, j)),
        out_shape=jax.ShapeDtypeStruct((DEPTH, 2 * seq, W), F32),
        compiler_params=_params(("parallel", "parallel", "parallel")), name=f"hyena_spectrum_{seq}",
    )(fwd_hi, fwd_lo, filt)


def _hyena_kernel(l_ref, x0_ref, x1_ref, vv_ref, cw0_ref, cw1_ref, cw2_ref, cb0_ref, cb1_ref, cb2_ref,
                  fwd_ref, inv_ref, h_ref, d_ref, *rest):
    o_ref, z_ref, zb_ref, acc_ref = rest[-4:]
    fb = pl.program_id(2)
    fq = h_ref.shape[0] // 2

    @pl.when(fb == 0)
    def _():
        z = _conv3(vv_ref[...], cw2_ref, cb2_ref) * _conv3(x1_ref[...], cw1_ref, cb1_ref)
        z_ref[...] = z
        zb_ref[...] = z.astype(BF16)
        acc_ref[...] = jnp.zeros_like(acc_ref)

    zf = _dot(fwd_ref[...], zb_ref[...])
    zc, zs = zf[:fq], zf[fq:]
    hc, hs = h_ref[:fq, :], h_ref[fq:, :]
    pr = zc * hc - zs * hs
    pq = zc * hs + zs * hc
    acc_ref[...] += _dot(inv_ref[...], jnp.concatenate([pr, pq], 0).astype(BF16))

    @pl.when(fb == pl.num_programs(2) - 1)
    def _():
        y = acc_ref[...] + z_ref[...] * d_ref[...]
        o_ref[...] = (_conv3(x0_ref[...], cw0_ref, cb0_ref) * y).astype(BF16)


def _hyena(lidx, proj, p, mats, spec, prev, *, nb, seq, rb):
    ct = 512
    fq = _freq_block(seq)
    nc = W // ct
    off = HY_OFF // ct
    fwd_hi, _, inv_hi = mats

    def col(c):
        return pl.BlockSpec((seq, ct), lambda b, j, f, l: (rb + b, off + c * nc + j))

    def cw(c):
        return pl.BlockSpec((None, 3, ct), lambda b, j, f, l: (l[0], 0, c * nc + j))

    def cb(c):
        return pl.BlockSpec((None, 1, ct), lambda b, j, f, l: (l[0], 0, c * nc + j))

    args = (proj, proj, proj, p['hy_conv_w'], p['hy_conv_w'], p['hy_conv_w'],
            p['hy_conv_b'], p['hy_conv_b'], p['hy_conv_b'], fwd_hi, inv_hi, spec, p['hy_d'])
    in_specs = [col(0), col(1), col(2), cw(0), cw(1), cw(2), cb(0), cb(1), cb(2),
                pl.BlockSpec((2 * fq, seq), lambda b, j, f, l: (f, 0)),
                pl.BlockSpec((seq, 2 * fq), lambda b, j, f, l: (0, f)),
                pl.BlockSpec((None, 2 * fq, ct), lambda b, j, f, l: (l[0], f, j)),
                pl.BlockSpec((None, 1, ct), lambda b, j, f, l: (l[0], 0, j))]
    aliases = None
    if prev is not None:
        args = args + (prev,)
        in_specs = in_specs + [_ANY]
        aliases = {len(args): 0}
    return _call(
        _hyena_kernel, lidx, args,
        grid=(nb, nc, seq // fq), in_specs=in_specs,
        out_specs=pl.BlockSpec((seq, ct), lambda b, j, f, l: (rb + b, j)),
        out_shape=jax.ShapeDtypeStruct((N_TOK, W), BF16),
        scratch=[pltpu.VMEM((seq, ct), F32), pltpu.VMEM((seq, ct), BF16), pltpu.VMEM((seq, ct), F32)],
        sem=("parallel", "parallel", "arbitrary"), name=f"hyena_{seq}", aliases=aliases)


def _merge_kernel(l_ref, oa_ref, oc_ref, mr_ref, ga_ref, gc_ref, wa_ref, wc_ref, o_ref):
    m = (jax.nn.sigmoid(ga_ref[...]) * _dot(oa_ref[...], wa_ref[...]) + mr_ref[...]
         + jax.nn.sigmoid(gc_ref[...]) * _dot(oc_ref[...], wc_ref[...]))
    o_ref[...] = m.astype(BF16)


def _merge(lidx, o_a, o_c, m_r, proj, p):
    tm, tn = 1024, 512
    goff = GL_OFF // tn
    nd = D // tn
    row = lambda: pl.BlockSpec((tm, W), lambda i, j, l: (i, 0))
    gate = lambda c: pl.BlockSpec((tm, tn), lambda i, j, l: (i, goff + c * nd + j))
    wsp = lambda: pl.BlockSpec((None, W, tn), lambda i, j, l: (l[0], 0, j))
    return _call(
        _merge_kernel, lidx, (o_a, o_c, m_r, proj, proj, p['w_pa'], p['w_pc']),
        grid=(N_TOK // tm, nd),
        in_specs=[row(), row(), pl.BlockSpec((tm, tn), lambda i, j, l: (i, j)),
                  gate(0), gate(2), wsp(), wsp()],
        out_specs=pl.BlockSpec((tm, tn), lambda i, j, l: (i, j)),
        out_shape=jax.ShapeDtypeStruct((N_TOK, D), BF16),
        sem=("parallel", "parallel"), name="merge")


def _out_proj_kernel(l_ref, m_ref, w_ref, x_ref, ga_ref, o_ref, *, tm):
    row = _mod_row(pl.program_id(0), tm)
    o_ref[...] = x_ref[...] + ga_ref[pl.ds(row, 1), :] * _dot(m_ref[...], w_ref[...])


def _out_proj(lidx, merged, x, mod, w_out):
    tm, tn = 1024, 512
    nd = D // tn
    return _call(
        functools.partial(_out_proj_kernel, tm=tm), lidx, (merged, w_out, x, mod),
        grid=(N_TOK // tm, nd),
        in_specs=[pl.BlockSpec((tm, D), lambda i, j, l: (i, 0)),
                  pl.BlockSpec((None, D, tn), lambda i, j, l: (l[0], 0, j)),
                  pl.BlockSpec((tm, tn), lambda i, j, l: (i, j)),
                  pl.BlockSpec((None, 8, tn), lambda i, j, l: (l[0], 0, 2 * nd + j))],
        out_specs=pl.BlockSpec((tm, tn), lambda i, j, l: (i, j)),
        out_shape=jax.ShapeDtypeStruct((N_TOK, D), F32),
        sem=("parallel", "parallel"), name="out_proj")


def _ffn_kernel(l_ref, x_ref, g_ref, sh_ref, sc_ref, ga_ref, w1_ref, b1_ref, w2_ref, b2_ref,
                o_ref, h_ref, acc_ref, *, tm):
    j = pl.program_id(1)
    row = _mod_row(pl.program_id(0), tm)

    @pl.when(j == 0)
    def _():
        h = _modnorm(x_ref[...], g_ref[...], sc_ref[pl.ds(row, 1), :], sh_ref[pl.ds(row, 1), :])
        h_ref[...] = h.astype(BF16)
        acc_ref[...] = jnp.zeros_like(acc_ref)

    a = _dot(h_ref[...], w1_ref[...]) + b1_ref[...]
    a = jnp.square(jnp.maximum(a, 0.0))
    acc_ref[...] += _dot(a.astype(BF16), w2_ref[...])

    @pl.when(j == pl.num_programs(1) - 1)
    def _():
        o_ref[...] = x_ref[...] + ga_ref[pl.ds(row, 1), :] * (acc_ref[...] + b2_ref[...])


def _ffn(lidx, x, ln_g, mod, p):
    tm, tf = 512, 512
    return _call(
        functools.partial(_ffn_kernel, tm=tm), lidx,
        (x, ln_g, mod, mod, mod, p['w_ff1'], p['b_ff1'], p['w_ff2'], p['b_ff2']),
        grid=(N_TOK // tm, D_FF // tf),
        in_specs=[pl.BlockSpec((tm, D), lambda i, j, l: (i, 0)),
                  pl.BlockSpec((None, 1, D), lambda i, j, l: (l[0], 0, 0)),
                  pl.BlockSpec((None, 8, D), lambda i, j, l: (l[0], 0, 3)),
                  pl.BlockSpec((None, 8, D), lambda i, j, l: (l[0], 0, 4)),
                  pl.BlockSpec((None, 8, D), lambda i, j, l: (l[0], 0, 5)),
                  pl.BlockSpec((None, D, tf), lambda i, j, l: (l[0], 0, j)),
                  pl.BlockSpec((None, 1, tf), lambda i, j, l: (l[0], 0, j)),
                  pl.BlockSpec((None, tf, D), lambda i, j, l: (l[0], j, 0)),
                  pl.BlockSpec((None, 1, D), lambda i, j, l: (l[0], 0, 0))],
        out_specs=pl.BlockSpec((tm, D), lambda i, j, l: (i, 0)),
        out_shape=jax.ShapeDtypeStruct((N_TOK, D), F32),
        scratch=[pltpu.VMEM((tm, D), BF16), pltpu.VMEM((tm, D), F32)],
        sem=("parallel", "arbitrary"), name="ffn")


def _final_norm_kernel(x_ref, g_ref, o_ref):
    x = x_ref[...]
    o_ref[...] = x * lax.rsqrt(jnp.mean(x * x, -1, keepdims=True) + NORM_EPS) * g_ref[...]


def _final_norm(x, g):
    tm = 1024
    return pl.pallas_call(
        _final_norm_kernel,
        grid=(N_TOK // tm,),
        in_specs=[pl.BlockSpec((tm, D), lambda i: (i, 0)), pl.BlockSpec((1, D), lambda i: (0, 0))],
        out_specs=pl.BlockSpec((tm, D), lambda i: (i, 0)),
        out_shape=jax.ShapeDtypeStruct((N_TOK, D), F32),
        compiler_params=_params(("parallel",)), name="final_norm",
    )(x, g.reshape(1, D))


def kernel(x_prompt, x_sample, cache_k, cache_v, state_wkv, c, c_ctx, ln1_g, ln2_g, w_mod, b_mod, w_in, rpb, wkv_conv_w, wkv_conv_b, wkv_w0, wkv_w1, wkv_w2, wkv_a0, wkv_a1, wkv_a2, wkv_g1, wkv_g2, wkv_k_k, wkv_k_a, wkv_r_k, wkv_gn_g, wkv_gn_b, hy_conv_w, hy_conv_b, hy_f1, hy_fb1, hy_f2, hy_fb2, hy_freq, hy_f3, hy_d, w_pa, w_pr, w_pc, w_out, w_ff1, b_ff1, w_ff2, b_ff2, final_g):
    x = jnp.concatenate([x_prompt.reshape(N_CTX, D), x_sample.reshape(N_LAT, D)], 0)
    cvec = jnp.zeros((8, D), F32).at[0].set(c_ctx).at[1:1 + B_LAT].set(c)
    mod = _modulation(cvec, w_mod, b_mod)

    perm = (np.arange(H)[None, :] * DH + np.arange(DH)[:, None]).reshape(-1)
    pad_c = lambda a, n: jnp.pad(a, [(0, 0)] * (a.ndim - 1) + [(0, n - a.shape[-1])])
    pad_r = lambda a, n: jnp.pad(a, [(0, 0)] * (a.ndim - 2) + [(0, n - a.shape[-2]), (0, 0)])
    conv = jnp.concatenate([wkv_conv_w, wkv_conv_b[:, None, :]], 1)
    conv = jnp.transpose(conv.reshape(DEPTH, 4, 3, W)[..., perm], (0, 2, 3, 1))
    rowp = jnp.stack([wkv_w0[:, 0], wkv_w0[:, 1], wkv_a0[:, 0], wkv_a0[:, 1]], -1)[:, perm]
    post = jnp.stack([wkv_r_k.reshape(DEPTH, W), wkv_gn_g, wkv_gn_b, jnp.zeros((DEPTH, W), F32)],
                     -1)[:, perm]
    p = {
        'wkv_conv': conv, 'wkv_rowp': rowp, 'wkv_post': post,
        'wkv_w2t': jnp.swapaxes(wkv_w2[..., perm], -1, -2),
        'wkv_a2t': jnp.swapaxes(wkv_a2[..., perm], -1, -2),
        'wkv_g2t': jnp.swapaxes(wkv_g2[..., perm], -1, -2),
        'kk_chain': _head_param(wkv_k_k, 2 * B_CTX), 'ka_chain': _head_param(wkv_k_a, 2 * B_CTX),
        'hy_conv_w': hy_conv_w, 'hy_conv_b': hy_conv_b.reshape(DEPTH, 1, 3 * W),
        'hy_d': hy_d.reshape(DEPTH, 1, W),
        'hy_f1p': pad_c(pad_r(hy_f1, FH_PAD), FH_PAD), 'hy_fb1p': pad_c(hy_fb1, FH_PAD).reshape(DEPTH, 1, FH_PAD),
        'hy_f2p': pad_c(pad_r(hy_f2, FH_PAD), FH_PAD), 'hy_fb2p': pad_c(hy_fb2, FH_PAD).reshape(DEPTH, 1, FH_PAD),
        'hy_freqp': pad_c(hy_freq, FH_PAD).reshape(DEPTH, 1, FH_PAD), 'hy_f3p': pad_r(hy_f3, FH_PAD),
        'w_pa': w_pa.astype(BF16), 'w_pr': w_pr[:, perm].astype(BF16), 'w_pc': w_pc.astype(BF16),
        'w_ff1': w_ff1.astype(BF16), 'b_ff1': b_ff1.reshape(DEPTH, 1, D_FF),
        'w_ff2': w_ff2.astype(BF16), 'b_ff2': b_ff2.reshape(DEPTH, 1, D),
    }
    w_a = jnp.concatenate([w_in[..., :3 * W], w_in[..., 6 * W:]], -1).astype(BF16)
    rkv = w_in[..., 3 * W:6 * W].reshape(DEPTH, D, 3, W)[..., perm].reshape(DEPTH, D, 3 * W)
    w_t = jnp.concatenate(
        [rkv, wkv_w1[:, 0], wkv_w1[:, 1], wkv_a1[:, 0], wkv_a1[:, 1], wkv_g1,
         jnp.zeros((DEPTH, D, N_LORA - 4 * LORA - LORA_G), F32)], -1)
    w_t = jnp.swapaxes(w_t, 1, 2).astype(BF16)
    w_out_b = w_out.astype(BF16)
    ln1 = ln1_g.reshape(DEPTH, 1, D)
    ln2 = ln2_g.reshape(DEPTH, 1, D)
    bias_tab = _bias_table(rpb)
    ck = cache_k.reshape(B_LAT, DEPTH, L_CTX, W)
    cv = cache_v.reshape(B_LAT, DEPTH, L_CTX, W)
    s0_lat = jnp.transpose(state_wkv, (1, 5, 4, 0, 2, 3)).reshape(DEPTH, DH, DH, B_LAT * 2 * H)
    s0_ctx = jnp.zeros((DH, DH, B_CTX * 2 * H), F32)

    hy = {}
    for seq in (L_CTX, L_LAT):
        mats = _dft_mats(seq)
        filt = _hyena_filter(p, seq)
        hy[seq] = (mats, _spectrum(mats[0], mats[1], filt, seq))

    def layer(x, l):
        lidx = jnp.reshape(l, (1,)).astype(jnp.int32)
        proj = _in_proj(lidx, x, ln1, mod, w_a, transposed=False)
        proj_t = _in_proj(lidx, x, ln1, mod, w_t, transposed=True)
        oa = _attn_lat(lidx, proj, ck, cv, bias_tab, _attn_ctx(lidx, proj))
        m_r, s_ctx = _rwkv_branch(lidx, proj, proj_t, p, s0_ctx, None, nb=B_CTX, seq=L_CTX,
                                  rb_t=0, rb_tok=0)
        s0 = lax.dynamic_index_in_dim(s0_lat, l, 0, keepdims=False)
        m_r, _ = _rwkv_branch(lidx, proj, proj_t, p, s0, m_r, nb=B_LAT, seq=L_LAT,
                              rb_t=N_CTX // L_LAT, rb_tok=N_CTX // 256)
        oc = _hyena(lidx, proj, p, *hy[L_CTX], None, nb=B_CTX, seq=L_CTX, rb=0)
        oc = _hyena(lidx, proj, p, *hy[L_LAT], oc, nb=B_LAT, seq=L_LAT, rb=N_CTX // L_LAT)
        merged = _merge(lidx, oa, oc, m_r, proj, p)
        x = _out_proj(lidx, merged, x, mod, w_out_b)
        x = _ffn(lidx, x, ln2, mod, p)
        k_new = proj[:N_CTX, W:2 * W]
        v_new = proj[:N_CTX, 2 * W:3 * W]
        return x, (k_new, v_new, s_ctx)

    x, (ks, vs, ss) = lax.scan(layer, x, jnp.arange(DEPTH, dtype=jnp.int32))
    y = _final_norm(x, final_g)
    y_prompt = y[:N_CTX].reshape(B_CTX, L_CTX, D)
    y_sample = y[N_CTX:].reshape(B_LAT, L_LAT, D)
    new_k = jnp.transpose(ks.reshape(DEPTH, B_CTX, L_CTX, H, DH), (1, 0, 2, 3, 4))
    new_v = jnp.transpose(vs.reshape(DEPTH, B_CTX, L_CTX, H, DH), (1, 0, 2, 3, 4))
    new_s = jnp.transpose(ss.reshape(DEPTH, DH, DH, B_CTX, 2, H), (3, 0, 4, 5, 2, 1))
    return (y_prompt, y_sample, new_k, new_v, new_s)
```

```python
import functools
import math

import numpy as np
import jax
import jax.numpy as jnp
from jax import lax
from jax.experimental import pallas as pl
from jax.experimental.pallas import tpu as pltpu

F32 = jnp.float32
BF16 = jnp.bfloat16

D = 2048
DEPTH = 4
B_CTX, L_CTX = 32, 256
B_LAT, L_LAT = 4, 2048
N_CTX = B_CTX * L_CTX
N_LAT = B_LAT * L_LAT
N_TOK = N_CTX + N_LAT
H = 16
DH = 64
W = H * DH
GRID_W = 64
ROWS = L_LAT // GRID_W
WIN_R, WIN_C = 8, 16
NL = WIN_R * GRID_W
LORA = 64
LORA_G = 128
N_LORA = 512
N_A = 6 * W + 3 * D
HY_OFF = 3 * W
GL_OFF = 6 * W
N_T = 3 * W + N_LORA
D_FF = 4 * D
N_MOD = 6
NORM_EPS = 1e-6
GN_EPS = 64e-5
NEG_INF = -1e30
POS_BANDS = 16
FH_PAD = 128
LANES = 128
GB = LANES // (2 * H)
TCH = 128
VMEM_LIMIT = 56 * 1024 * 1024


def _params(sem):
    return pltpu.CompilerParams(dimension_semantics=sem, vmem_limit_bytes=VMEM_LIMIT)


def _call(kernel, lidx, args, *, grid, in_specs, out_specs, out_shape, scratch=(), sem, name,
          aliases=None):
    gs = pltpu.PrefetchScalarGridSpec(num_scalar_prefetch=1, grid=grid, in_specs=in_specs,
                                      out_specs=out_specs, scratch_shapes=list(scratch))
    return pl.pallas_call(kernel, grid_spec=gs, out_shape=out_shape,
                          compiler_params=_params(sem), name=name,
                          input_output_aliases=aliases or {})(lidx, *args)


_ANY = pl.BlockSpec(memory_space=pl.ANY)


def _mod_row(i, tm):
    start = i * tm
    return jnp.where(start < N_CTX, 0, 1 + (start - N_CTX) // L_LAT)


def _modnorm(x, g, sc, sh):
    y = x * lax.rsqrt(jnp.mean(x * x, -1, keepdims=True) + NORM_EPS)
    return (y * g) * (1.0 + sc) + sh


def _softplus(x):
    return jnp.maximum(x, 0.0) + jnp.log1p(jnp.exp(-jnp.abs(x)))


def _dot(a, b):
    return jnp.dot(a, b, preferred_element_type=F32)


def _dot_nt(a, b):
    return lax.dot_general(a, b, (((1,), (1,)), ((), ())), preferred_element_type=F32)


def _dot_tn(a, b):
    return lax.dot_general(a, b, (((0,), (0,)), ((), ())), preferred_element_type=F32)


def _split(x):
    hi = x.astype(BF16)
    lo = (x - hi.astype(F32)).astype(BF16)
    return hi, lo


def _dot3(ah, al, bh, bl):
    return _dot(ah, bh) + (_dot(al, bh) + _dot(ah, bl))


def _mod_kernel(c_ref, w_ref, b_ref, o_ref):
    c = c_ref[...]
    s = c * jax.nn.sigmoid(c)
    o_ref[...] = _dot(s.astype(BF16), w_ref[...].astype(BF16)) + b_ref[...]


def _modulation(cvec8, w_mod, b_mod):
    tn = 1024
    return pl.pallas_call(
        _mod_kernel,
        grid=(DEPTH, N_MOD * D // tn),
        in_specs=[pl.BlockSpec((8, D), lambda l, j: (0, 0)),
                  pl.BlockSpec((None, D, tn), lambda l, j: (l, 0, j)),
                  pl.BlockSpec((None, 1, tn), lambda l, j: (l, 0, j))],
        out_specs=pl.BlockSpec((None, 8, tn), lambda l, j: (l, 0, j)),
        out_shape=jax.ShapeDtypeStruct((DEPTH, 8, N_MOD * D), F32),
        compiler_params=_params(("parallel", "parallel")), name="modulation",
    )(cvec8, w_mod, b_mod.reshape(DEPTH, 1, N_MOD * D))


def _in_proj_kernel(l_ref, x_ref, g_ref, sh_ref, sc_ref, w_ref, o_ref, h_ref, *, tm, transposed):
    @pl.when(pl.program_id(1) == 0)
    def _():
        row = _mod_row(pl.program_id(0), tm)
        h = _modnorm(x_ref[...], g_ref[...], sc_ref[pl.ds(row, 1), :], sh_ref[pl.ds(row, 1), :])
        h_ref[...] = h.astype(BF16)

    if transposed:
        o_ref[...] = _dot_nt(w_ref[...], h_ref[...])
    else:
        o_ref[...] = _dot(h_ref[...], w_ref[...])


def _in_proj(lidx, x, ln_g, mod, w, *, transposed):
    tm = 1024
    tn = 896 if transposed else 512
    if transposed:
        n_out = w.shape[1]
        w_spec = pl.BlockSpec((None, tn, D), lambda i, j, l: (l[0], j, 0))
        o_spec = pl.BlockSpec((tn, tm), lambda i, j, l: (j, i))
        o_shape = jax.ShapeDtypeStruct((n_out, N_TOK), F32)
    else:
        n_out = w.shape[2]
        w_spec = pl.BlockSpec((None, D, tn), lambda i, j, l: (l[0], 0, j))
        o_spec = pl.BlockSpec((tm, tn), lambda i, j, l: (i, j))
        o_shape = jax.ShapeDtypeStruct((N_TOK, n_out), F32)
    return _call(
        functools.partial(_in_proj_kernel, tm=tm, transposed=transposed), lidx, (x, ln_g, mod, mod, w),
        grid=(N_TOK // tm, n_out // tn),
        in_specs=[pl.BlockSpec((tm, D), lambda i, j, l: (i, 0)),
                  pl.BlockSpec((None, 1, D), lambda i, j, l: (l[0], 0, 0)),
                  pl.BlockSpec((None, 8, D), lambda i, j, l: (l[0], 0, 0)),
                  pl.BlockSpec((None, 8, D), lambda i, j, l: (l[0], 0, 1)),
                  w_spec],
        out_specs=o_spec, out_shape=o_shape,
        scratch=[pltpu.VMEM((tm, D), BF16)],
        sem=("parallel", "arbitrary"), name="in_proj_t" if transposed else "in_proj")


ATT_CH = 8


def _attn_ctx_kernel(l_ref, q_ref, k_ref, v_ref, o_ref, s_ref):
    scale = DH ** -0.5
    for hh in range(ATT_CH):
        sl = slice(hh * DH, (hh + 1) * DH)
        q = (q_ref[:, sl] * scale).astype(BF16)
        s_ref[hh] = _dot_nt(q, k_ref[:, sl].astype(BF16))
    outs = []
    for hh in range(ATT_CH):
        sl = slice(hh * DH, (hh + 1) * DH)
        s = s_ref[hh]
        p = jnp.exp(s - jnp.max(s, -1, keepdims=True))
        den = jnp.sum(p, -1, keepdims=True)
        outs.append(_dot(p.astype(BF16), v_ref[:, sl].astype(BF16)) / den)
    o_ref[...] = jnp.concatenate(outs, -1).astype(BF16)


def _attn_ctx(lidx, proj):
    wb = ATT_CH * DH
    nq = W // wb
    return _call(
        _attn_ctx_kernel, lidx, (proj, proj, proj),
        grid=(B_CTX, nq),
        in_specs=[pl.BlockSpec((L_CTX, wb), lambda b, p, l: (b, p)),
                  pl.BlockSpec((L_CTX, wb), lambda b, p, l: (b, nq + p)),
                  pl.BlockSpec((L_CTX, wb), lambda b, p, l: (b, 2 * nq + p))],
        out_specs=pl.BlockSpec((L_CTX, wb), lambda b, p, l: (b, p)),
        out_shape=jax.ShapeDtypeStruct((N_TOK, W), BF16),
        scratch=[pltpu.VMEM((ATT_CH, L_CTX, L_CTX), F32)],
        sem=("parallel", "parallel"), name="attn_ctx")


ATT_RB = 8


def _attn_lat_kernel(l_ref, q_ref, k_ref, v_ref, kc_ref, vc_ref, bias_ref, alias_ref, o_ref,
                     kb_ref, vb_ref, kcb_ref, vcb_ref, s_ref):
    scale = DH ** -0.5
    for hh in range(2):
        sl = slice(hh * DH, (hh + 1) * DH)
        kb_ref[hh] = k_ref[:, sl].astype(BF16)
        vb_ref[hh] = v_ref[:, sl].astype(BF16)
        kcb_ref[hh] = kc_ref[:, sl].astype(BF16)
        vcb_ref[hh] = vc_ref[:, sl].astype(BF16)

    def window(r):
        r0 = jnp.clip(r - WIN_R // 2, 0, ROWS - WIN_R)
        return r - r0, pl.multiple_of(r * GRID_W, GRID_W), pl.multiple_of(r0 * GRID_W, GRID_W)

    def row_block(rr, carry):
        for i in range(ATT_RB):
            d, qrow, krow = window(rr * ATT_RB + i)
            for hh in range(2):
                sl = slice(hh * DH, (hh + 1) * DH)
                q = (q_ref[pl.ds(qrow, GRID_W), sl] * scale).astype(BF16)
                s_ref[2 * i + hh, :, :NL] = _dot_nt(q, kb_ref[hh, pl.ds(krow, NL), :]) + bias_ref[hh, d]
                s_ref[2 * i + hh, :, NL:] = _dot_nt(q, kcb_ref[hh])
        for i in range(ATT_RB):
            d, qrow, krow = window(rr * ATT_RB + i)
            outs = []
            for hh in range(2):
                s = s_ref[2 * i + hh]
                p = jnp.exp(s - jnp.max(s, -1, keepdims=True))
                den = jnp.sum(p, -1, keepdims=True)
                pb = p.astype(BF16)
                o = _dot(pb[:, :NL], vb_ref[hh, pl.ds(krow, NL), :]) + _dot(pb[:, NL:], vcb_ref[hh])
                outs.append(o / den)
            o_ref[pl.ds(qrow, GRID_W), :] = jnp.concatenate(outs, -1).astype(BF16)
        return carry

    lax.fori_loop(0, ROWS // ATT_RB, row_block, 0)


def _attn_lat(lidx, proj, cache_k, cache_v, bias_tab, o_ctx):
    nq = W // LANES
    rb = N_CTX // L_LAT
    return _call(
        _attn_lat_kernel, lidx, (proj, proj, proj, cache_k, cache_v, bias_tab, o_ctx),
        grid=(B_LAT, nq),
        in_specs=[pl.BlockSpec((L_LAT, LANES), lambda b, p, l: (rb + b, p)),
                  pl.BlockSpec((L_LAT, LANES), lambda b, p, l: (rb + b, nq + p)),
                  pl.BlockSpec((L_LAT, LANES), lambda b, p, l: (rb + b, 2 * nq + p)),
                  pl.BlockSpec((None, None, L_CTX, LANES), lambda b, p, l: (b, l[0], 0, p)),
                  pl.BlockSpec((None, None, L_CTX, LANES), lambda b, p, l: (b, l[0], 0, p)),
                  pl.BlockSpec((None, 2, WIN_R, GRID_W, NL), lambda b, p, l: (l[0], p, 0, 0, 0)),
                  _ANY],
        out_specs=pl.BlockSpec((L_LAT, LANES), lambda b, p, l: (rb + b, p)),
        out_shape=jax.ShapeDtypeStruct((N_TOK, W), BF16),
        scratch=[pltpu.VMEM((2, L_LAT, DH), BF16), pltpu.VMEM((2, L_LAT, DH), BF16),
                 pltpu.VMEM((2, L_CTX, DH), BF16), pltpu.VMEM((2, L_CTX, DH), BF16),
                 pltpu.VMEM((2 * ATT_RB, GRID_W, NL + L_CTX), F32)],
        sem=("parallel", "parallel"), name="attn_lat", aliases={7: 0})


def _bias_table(rpb):
    cq = np.arange(GRID_W)
    c0 = np.clip(cq - WIN_C // 2, 0, GRID_W - WIN_C)
    ck = np.arange(GRID_W)
    col_ok = (ck[None, :] >= c0[:, None]) & (ck[None, :] < c0[:, None] + WIN_C)
    dc = np.clip(ck[None, :] - cq[:, None], -(WIN_C - 1), WIN_C - 1) + (WIN_C - 1)
    onehot = (dc.reshape(-1)[None, :] == np.arange(2 * WIN_C - 1)[:, None]).astype(np.float32)
    cols = jnp.einsum('lhrc,cx->lhrx', rpb, jnp.asarray(onehot), precision=lax.Precision.HIGHEST)
    cols = cols.reshape(DEPTH, H, 2 * WIN_R - 1, GRID_W, GRID_W)
    tab = jnp.stack([cols[:, :, WIN_R - 1 - d:2 * WIN_R - 1 - d] for d in range(WIN_R)], 2)
    tab = jnp.transpose(tab, (0, 1, 2, 4, 3, 5))
    tab = jnp.where(col_ok[None, None, None, :, None, :], tab, NEG_INF)
    return tab.reshape(DEPTH, H, WIN_R, GRID_W, NL).astype(F32)


def _conv3(x, w_ref, b_ref):
    n = x.shape[0]
    row = lax.broadcasted_iota(jnp.int32, x.shape, 0)
    prev = jnp.where(row == 0, 0.0, pltpu.roll(x, 1, 0))
    nxt = jnp.where(row == n - 1, 0.0, pltpu.roll(x, n - 1, 0))
    return prev * w_ref[0:1, :] + x * w_ref[1:2, :] + nxt * w_ref[2:3, :] + b_ref[...]


def _conv3_t(x, p):
    n = x.shape[1]
    lane = lax.broadcasted_iota(jnp.int32, x.shape, 1)
    prev = jnp.where(lane == 0, 0.0, pltpu.roll(x, 1, 1))
    nxt = jnp.where(lane == n - 1, 0.0, pltpu.roll(x, n - 1, 1))
    return prev * p[:, 0:1] + x * p[:, 1:2] + nxt * p[:, 2:3] + p[:, 3:4]


def _wkv_prep_kernel(l_ref, r_ref, k_ref, v_ref, lora_ref, cp_ref, rp_ref, w2_ref, a2_ref, g2_ref,
                     ro_ref, ko_ref, vo_ref, d0_ref, d1_ref, a0o_ref, a1o_ref, go_ref):
    ro_ref[...] = _conv3_t(r_ref[...], cp_ref[0])
    ko_ref[...] = _conv3_t(k_ref[...], cp_ref[1])
    vo_ref[...] = _conv3_t(v_ref[...], cp_ref[2])
    rp = rp_ref[...]
    for e, (d_ref, ao_ref) in enumerate(((d0_ref, a0o_ref), (d1_ref, a1o_ref))):
        lw = jnp.tanh(lora_ref[e * LORA:(e + 1) * LORA, :]).astype(BF16)
        la = lora_ref[(2 + e) * LORA:(3 + e) * LORA, :].astype(BF16)
        w_log = rp[:, e:e + 1] + _dot(w2_ref[e].astype(BF16), lw)
        d_ref[...] = jnp.exp(-jnp.exp(-_softplus(-w_log) - 0.5))
        ao_ref[...] = jax.nn.sigmoid(rp[:, 2 + e:3 + e] + _dot(a2_ref[e].astype(BF16), la))
    lg = jax.nn.sigmoid(lora_ref[4 * LORA:4 * LORA + LORA_G, :]).astype(BF16)
    go_ref[...] = _dot(g2_ref[...].astype(BF16), lg)


def _wkv_prep(lidx, proj_t, p, *, nb, seq, rb):
    ct = 512 if seq <= 256 else 128
    nc = W // ct

    def row(c):
        return pl.BlockSpec((ct, seq), lambda b, j, l: (c * nc + j, rb + b))

    out = pl.BlockSpec((None, ct, seq), lambda b, j, l: (b, j, 0))
    return _call(
        _wkv_prep_kernel, lidx,
        (proj_t, proj_t, proj_t, proj_t, p['wkv_conv'], p['wkv_rowp'], p['wkv_w2t'], p['wkv_a2t'],
         p['wkv_g2t']),
        grid=(nb, nc),
        in_specs=[row(0), row(1), row(2),
                  pl.BlockSpec((N_LORA, seq), lambda b, j, l: (3 * W // N_LORA, rb + b)),
                  pl.BlockSpec((None, 3, ct, 4), lambda b, j, l: (l[0], 0, j, 0)),
                  pl.BlockSpec((None, ct, 4), lambda b, j, l: (l[0], j, 0)),
                  pl.BlockSpec((None, 2, ct, LORA), lambda b, j, l: (l[0], 0, j, 0)),
                  pl.BlockSpec((None, 2, ct, LORA), lambda b, j, l: (l[0], 0, j, 0)),
                  pl.BlockSpec((None, ct, LORA_G), lambda b, j, l: (l[0], j, 0))],
        out_specs=[out] * 8,
        out_shape=[jax.ShapeDtypeStruct((nb, W, seq), F32)] * 8,
        sem=("parallel", "parallel"), name=f"wkv_prep_{seq}")


def _to_chains_kernel(l_ref, *refs):
    xs, o_ref = refs[:2 * GB], refs[2 * GB]
    for k0 in range(0, DH, 8):
        tiles = []
        for k in range(k0, k0 + 8):
            m = jnp.concatenate([x[k * H:(k + 1) * H, :] for x in xs], 0)
            tiles.append(m.T)
        o_ref[:, k0:k0 + 8, :] = pltpu.einshape("ktc->tkc", jnp.stack(tiles, 0))


def _to_chains(lidx, x0, x1, *, nb, seq):
    def src(i):
        return pl.BlockSpec((None, W, TCH), lambda g, t, l: (GB * g + i // 2, 0, t))

    args = tuple(x0 if i % 2 == 0 else x1 for i in range(2 * GB))
    return _call(
        _to_chains_kernel, lidx, args,
        grid=(nb // GB, seq // TCH),
        in_specs=[src(i) for i in range(2 * GB)],
        out_specs=pl.BlockSpec((TCH, DH, LANES), lambda g, t, l: (t, 0, g)),
        out_shape=jax.ShapeDtypeStruct((seq, DH, nb * 2 * H), F32),
        sem=("parallel", "parallel"), name=f"to_chains_{seq}")


def _wkv_kernel(l_ref, rf_ref, kf_ref, vf_ref, wf_ref, af_ref, rb_ref, kb_ref, vb_ref, wb_ref, ab_ref,
                kkp_ref, kap_ref, s0_ref, yf_ref, yb_ref, s_ref, st_ref, *, tb):
    @pl.when(pl.program_id(1) == 0)
    def _():
        s_ref[...] = s0_ref[...]

    kkp = kkp_ref[...]
    kap = kap_ref[...]
    lane = lax.broadcasted_iota(jnp.int32, (DH, LANES), 1)
    bwd = (lane // H) % 2 == 1
    kc = 32
    zero = jnp.zeros((DH, LANES), F32)

    def step(t, carry):
        tr = tb - 1 - t
        pick = lambda f_ref, b_ref: jnp.where(bwd, b_ref[tr], f_ref[t])
        kt = pick(kf_ref, kb_ref)
        at = pick(af_ref, ab_ref)
        vt = pick(vf_ref, vb_ref)
        kk = kt * kkp
        kk = kk * lax.rsqrt(jnp.sum(kk * kk, 0, keepdims=True) + 1e-12)
        st_ref[0] = kk
        st_ref[1] = kk * at
        st_ref[2] = kt * (1.0 + (at - 1.0) * kap)
        st_ref[3] = pick(wf_ref, wb_ref)
        st_ref[4] = pick(rf_ref, rb_ref)

        def sa_body(c, acc):
            a0, a1 = acc
            for j in range(kc):
                k = c * kc + j
                term = s_ref[k] * st_ref[0, pl.ds(k, 1), :]
                if j % 2 == 0:
                    a0 = a0 + term
                else:
                    a1 = a1 + term
            return a0, a1

        a0, a1 = lax.fori_loop(0, DH // kc, sa_body, (zero, zero))
        sa = -(a0 + a1)

        def up_body(c, acc):
            y0, y1 = acc
            for j in range(kc):
                k = c * kc + j
                sk = ((s_ref[k] * st_ref[3, pl.ds(k, 1), :] + sa * st_ref[1, pl.ds(k, 1), :])
                      + vt * st_ref[2, pl.ds(k, 1), :])
                s_ref[k] = sk
                term = sk * st_ref[4, pl.ds(k, 1), :]
                if j % 2 == 0:
                    y0 = y0 + term
                else:
                    y1 = y1 + term
            return y0, y1

        y0, y1 = lax.fori_loop(0, DH // kc, up_body, (zero, zero))
        y = y0 + y1
        yf_ref[t] = y
        yb_ref[tr] = y
        return carry

    lax.fori_loop(0, tb, step, 0)


def _wkv(lidx, r, k, v, w, a, kkp, kap, s0, *, seq):
    tb = 32
    chains = r.shape[-1]
    nt = seq // tb
    f_spec = pl.BlockSpec((tb, DH, LANES), lambda g, t, l: (t, 0, g))
    b_spec = pl.BlockSpec((tb, DH, LANES), lambda g, t, l: (nt - 1 - t, 0, g))
    par_spec = pl.BlockSpec((None, DH, LANES), lambda g, t, l: (l[0], 0, g))
    st_spec = pl.BlockSpec((DH, DH, LANES), lambda g, t, l: (0, 0, g))
    y_shape = jax.ShapeDtypeStruct((seq, DH, chains), F32)
    return _call(
        functools.partial(_wkv_kernel, tb=tb), lidx, (r, k, v, w, a, r, k, v, w, a, kkp, kap, s0),
        grid=(chains // LANES, nt),
        in_specs=[f_spec] * 5 + [b_spec] * 5 + [par_spec, par_spec, st_spec],
        out_specs=[f_spec, b_spec, st_spec],
        out_shape=[y_shape, y_shape, jax.ShapeDtypeStruct((DH, DH, chains), F32)],
        scratch=[pltpu.VMEM((5, DH, LANES), F32)],
        sem=("parallel", "arbitrary"), name=f"wkv_{seq}")


def _from_chains_kernel(l_ref, yf_ref, yb_ref, o_ref):
    for v0 in range(0, DH, 8):
        f3 = pltpu.einshape("tvc->vtc", yf_ref[:, v0:v0 + 8, :])
        b3 = pltpu.einshape("tvc->vtc", yb_ref[:, v0:v0 + 8, :])
        for i in range(8):
            v = v0 + i
            tf = f3[i].T
            tb = b3[i].T
            for b in range(GB):
                lo = b * 2 * H
                o_ref[b, v * H:(v + 1) * H, :] = tf[lo:lo + H] + tb[lo + H:lo + 2 * H]


def _from_chains(lidx, yf, yb, *, nb, seq):
    y_spec = pl.BlockSpec((TCH, DH, LANES), lambda g, t, l: (t, 0, g))
    return _call(
        _from_chains_kernel, lidx, (yf, yb),
        grid=(nb // GB, seq // TCH),
        in_specs=[y_spec, y_spec],
        out_specs=pl.BlockSpec((GB, W, TCH), lambda g, t, l: (g, 0, t)),
        out_shape=jax.ShapeDtypeStruct((nb, W, seq), F32),
        sem=("parallel", "parallel"), name=f"from_chains_{seq}")


def _wkv_post_kernel(l_ref, y_ref, r_ref, k_ref, v_ref, g_ref, gate_ref, par_ref, w_ref, *rest):
    o_ref = rest[-1]
    t = y_ref.shape[-1]
    par = par_ref[...]
    y = y_ref[...].reshape(DH, H, t)
    mu = jnp.mean(y, 0, keepdims=True)
    yc = y - mu
    var = jnp.mean(yc * yc, 0, keepdims=True)
    yn = (yc * lax.rsqrt(var + GN_EPS)).reshape(W, t) * par[:, 1:2] + par[:, 2:3]
    rk = (r_ref[...] * k_ref[...] * par[:, 0:1]).reshape(DH, H, t)
    bonus = jnp.broadcast_to(jnp.sum(rk, 0, keepdims=True), (DH, H, t)).reshape(W, t) * v_ref[...]
    o = ((yn + bonus) * g_ref[...]).astype(BF16)
    o_ref[...] = jax.nn.sigmoid(gate_ref[...]) * _dot_tn(o, w_ref[...])


def _wkv_post(lidx, ysum, r, k, v, g, proj, p, prev, *, nb, seq, rb):
    tt = 256
    nt = seq // tt
    t_spec = pl.BlockSpec((None, W, tt), lambda b, t, l: (b, 0, t))
    tok = lambda b, t: rb + b * nt + t
    args = (ysum, r, k, v, g, proj, p['wkv_post'], p['w_pr'])
    in_specs = [t_spec] * 5 + [
        pl.BlockSpec((tt, D), lambda b, t, l: (tok(b, t), (GL_OFF + D) // D)),
        pl.BlockSpec((None, W, 4), lambda b, t, l: (l[0], 0, 0)),
        pl.BlockSpec((None, W, D), lambda b, t, l: (l[0], 0, 0))]
    aliases = None
    if prev is not None:
        args = args + (prev,)
        in_specs = in_specs + [_ANY]
        aliases = {len(args): 0}
    return _call(
        _wkv_post_kernel, lidx, args,
        grid=(nb, nt), in_specs=in_specs,
        out_specs=pl.BlockSpec((tt, D), lambda b, t, l: (tok(b, t), 0)),
        out_shape=jax.ShapeDtypeStruct((N_TOK, D), F32),
        sem=("parallel", "parallel"), name=f"wkv_post_{seq}", aliases=aliases)


def _head_param(p, reps):
    t = jnp.transpose(p.reshape(DEPTH, H, DH), (0, 2, 1))
    return jnp.tile(t, (1, 1, reps))


def _rwkv_branch(lidx, proj, proj_t, p, s0_chain, prev, *, nb, seq, rb_t, rb_tok):
    r, k, v, d0, d1, a0, a1, g = _wkv_prep(lidx, proj_t, p, nb=nb, seq=seq, rb=rb_t)
    tc = functools.partial(_to_chains, lidx, nb=nb, seq=seq)
    chains = nb * 2 * H
    yf, yb, s_fin = _wkv(lidx, tc(r, r), tc(k, k), tc(v, v), tc(d0, d1), tc(a0, a1),
                         p['kk_chain'][:, :, :chains], p['ka_chain'][:, :, :chains], s0_chain, seq=seq)
    ysum = _from_chains(lidx, yf, yb, nb=nb, seq=seq)
    m_r = _wkv_post(lidx, ysum, r, k, v, g, proj, p, prev, nb=nb, seq=seq, rb=rb_tok)
    return m_r, s_fin


def _pos_features(seq):
    t = np.linspace(0.0, 1.0, seq, dtype=np.float32)[:, None]
    w = 2.0 * np.pi * np.arange(seq, dtype=np.float32)[:, None] / seq
    f = np.linspace(1e-4, POS_BANDS - 1, POS_BANDS, dtype=np.float32)[None, :]
    z = np.concatenate([t, np.cos(f * w), -np.sin(f * w)], -1).astype(np.float32)
    zp = np.zeros((seq, FH_PAD), np.float32)
    zp[:, :z.shape[1]] = z
    dist = (np.abs(np.arange(seq) - seq // 2).astype(np.float32) / seq)[:, None]
    deltas = np.abs(np.linspace(math.log(1e-2) / 1.5, math.log(1e-2) / 0.3, W,
                                dtype=np.float32))[None, :]
    return zp, dist, deltas


def _filter_kernel(zp_ref, dist_ref, del_ref, f1_ref, b1_ref, f2_ref, b2_ref, fr_ref, f3_ref, o_ref):
    hi = lax.Precision.HIGHEST
    fr = fr_ref[...]
    t = jnp.sin(fr * (jnp.dot(zp_ref[...], f1_ref[...], precision=hi,
                              preferred_element_type=F32) + b1_ref[...]))
    t = jnp.sin(fr * (jnp.dot(t, f2_ref[...], precision=hi, preferred_element_type=F32) + b2_ref[...]))
    filt = jnp.dot(t, f3_ref[...], precision=hi, preferred_element_type=F32)
    filt = filt * jnp.exp(-dist_ref[...] * del_ref[...])
    o_ref[...] = filt / (jnp.sum(jnp.abs(filt), 0, keepdims=True) + 1e-6)


def _hyena_filter(p, seq):
    ct = 256
    zp, dist, deltas = _pos_features(seq)
    full = lambda shape: pl.BlockSpec((None,) + shape, lambda l, j: (l,) + (0,) * len(shape))
    return pl.pallas_call(
        _filter_kernel,
        grid=(DEPTH, W // ct),
        in_specs=[pl.BlockSpec((seq, FH_PAD), lambda l, j: (0, 0)),
                  pl.BlockSpec((seq, 1), lambda l, j: (0, 0)),
                  pl.BlockSpec((1, ct), lambda l, j: (0, j)),
                  full((FH_PAD, FH_PAD)), full((1, FH_PAD)), full((FH_PAD, FH_PAD)),
                  full((1, FH_PAD)), full((1, FH_PAD)),
                  pl.BlockSpec((None, FH_PAD, ct), lambda l, j: (l, 0, j))],
        out_specs=pl.BlockSpec((None, seq, ct), lambda l, j: (l, 0, j)),
        out_shape=jax.ShapeDtypeStruct((DEPTH, seq, W), F32),
        compiler_params=_params(("parallel", "parallel")), name=f"hyena_filter_{seq}",
    )(jnp.asarray(zp), jnp.asarray(dist), jnp.asarray(deltas),
      p['hy_f1p'], p['hy_fb1p'], p['hy_f2p'], p['hy_fb2p'], p['hy_freqp'], p['hy_f3p'])


def _freq_block(seq):
    return min(seq, 256)


def _dft_mats(seq):
    n = 2 * seq
    fq = _freq_block(seq)
    k = jnp.arange(seq, dtype=jnp.int32)
    t = jnp.arange(seq, dtype=jnp.int32)
    ph = ((2 * k[:, None] + 1) * t[None, :]) % (2 * n)
    ang = ph.astype(F32) * np.float32(np.pi / n)
    fwd = jnp.stack([jnp.cos(ang).reshape(seq // fq, fq, seq),
                     jnp.sin(ang).reshape(seq // fq, fq, seq)], 1).reshape(2 * seq, seq)
    m = t + seq // 2
    ph2 = ((2 * k[None, :] + 1) * m[:, None]) % (2 * n)
    ang2 = ph2.astype(F32) * np.float32(np.pi / n)
    inv = jnp.stack([jnp.cos(ang2).reshape(seq, seq // fq, fq),
                     jnp.sin(ang2).reshape(seq, seq // fq, fq)], 2).reshape(seq, 2 * seq)
    inv = inv * np.float32(2.0 / n)
    fwd_hi, fwd_lo = _split(fwd)
    return fwd_hi, fwd_lo, inv.astype(BF16)


def _spectrum_kernel(ah_ref, al_ref, b_ref, o_ref):
    bh, bl = _split(b_ref[...])
    o_ref[...] = _dot3(ah_ref[...], al_ref[...], bh, bl)


def _spectrum(fwd_hi, fwd_lo, filt, seq):
    tmm, tn = 256, 512
    return pl.pallas_call(
        _spectrum_kernel,
        grid=(DEPTH, 2 * seq // tmm, W // tn),
        in_specs=[pl.BlockSpec((tmm, seq), lambda l, i, j: (i, 0)),
                  pl.BlockSpec((tmm, seq), lambda l, i, j: (i, 0)),
                  pl.BlockSpec((None, seq, tn), lambda l, i, j: (l, 0, j))],
        out_specs=pl.BlockSpec((None, tmm, tn), lambda l, i, j: (l, i, j)),
        out_shape=jax.ShapeDtypeStruct((DEPTH, 2 * seq, W), F32),
        compiler_params=_params(("parallel", "parallel", "parallel")), name=f"hyena_spectrum_{seq}",
    )(fwd_hi, fwd_lo, filt)


def _hyena_kernel(l_ref, x0_ref, x1_ref, vv_ref, cw0_ref, cw1_ref, cw2_ref, cb0_ref, cb1_ref, cb2_ref,
                  fwd_ref, inv_ref, h_ref, d_ref, *rest):
    o_ref, z_ref, zb_ref, acc_ref = rest[-4:]
    fb = pl.program_id(2)
    fq = h_ref.shape[0] // 2

    @pl.when(fb == 0)
    def _():
        z = _conv3(vv_ref[...], cw2_ref, cb2_ref) * _conv3(x1_ref[...], cw1_ref, cb1_ref)
        z_ref[...] = z
        zb_ref[...] = z.astype(BF16)
        acc_ref[...] = jnp.zeros_like(acc_ref)

    zf = _dot(fwd_ref[...], zb_ref[...])
    zc, zs = zf[:fq], zf[fq:]
    hc, hs = h_ref[:fq, :], h_ref[fq:, :]
    pr = zc * hc - zs * hs
    pq = zc * hs + zs * hc
    acc_ref[...] += _dot(inv_ref[...], jnp.concatenate([pr, pq], 0).astype(BF16))

    @pl.when(fb == pl.num_programs(2) - 1)
    def _():
        y = acc_ref[...] + z_ref[...] * d_ref[...]
        o_ref[...] = (_conv3(x0_ref[...], cw0_ref, cb0_ref) * y).astype(BF16)


def _hyena(lidx, proj, p, mats, spec, prev, *, nb, seq, rb):
    ct = 512
    fq = _freq_block(seq)
    nc = W // ct
    off = HY_OFF // ct
    fwd_hi, _, inv_hi = mats

    def col(c):
        return pl.BlockSpec((seq, ct), lambda b, j, f, l: (rb + b, off + c * nc + j))

    def cw(c):
        return pl.BlockSpec((None, 3, ct), lambda b, j, f, l: (l[0], 0, c * nc + j))

    def cb(c):
        return pl.BlockSpec((None, 1, ct), lambda b, j, f, l: (l[0], 0, c * nc + j))

    args = (proj, proj, proj, p['hy_conv_w'], p['hy_conv_w'], p['hy_conv_w'],
            p['hy_conv_b'], p['hy_conv_b'], p['hy_conv_b'], fwd_hi, inv_hi, spec, p['hy_d'])
    in_specs = [col(0), col(1), col(2), cw(0), cw(1), cw(2), cb(0), cb(1), cb(2),
                pl.BlockSpec((2 * fq, seq), lambda b, j, f, l: (f, 0)),
                pl.BlockSpec((seq, 2 * fq), lambda b, j, f, l: (0, f)),
                pl.BlockSpec((None, 2 * fq, ct), lambda b, j, f, l: (l[0], f, j)),
                pl.BlockSpec((None, 1, ct), lambda b, j, f, l: (l[0], 0, j))]
    aliases = None
    if prev is not None:
        args = args + (prev,)
        in_specs = in_specs + [_ANY]
        aliases = {len(args): 0}
    return _call(
        _hyena_kernel, lidx, args,
        grid=(nb, nc, seq // fq), in_specs=in_specs,
        out_specs=pl.BlockSpec((seq, ct), lambda b, j, f, l: (rb + b, j)),
        out_shape=jax.ShapeDtypeStruct((N_TOK, W), BF16),
        scratch=[pltpu.VMEM((seq, ct), F32), pltpu.VMEM((seq, ct), BF16), pltpu.VMEM((seq, ct), F32)],
        sem=("parallel", "parallel", "arbitrary"), name=f"hyena_{seq}", aliases=aliases)


def _merge_kernel(l_ref, oa_ref, oc_ref, mr_ref, ga_ref, gc_ref, wa_ref, wc_ref, o_ref):
    m = (jax.nn.sigmoid(ga_ref[...]) * _dot(oa_ref[...], wa_ref[...]) + mr_ref[...]
         + jax.nn.sigmoid(gc_ref[...]) * _dot(oc_ref[...], wc_ref[...]))
    o_ref[...] = m.astype(BF16)


def _merge(lidx, o_a, o_c, m_r, proj, p):
    tm, tn = 1024, 512
    goff = GL_OFF // tn
    nd = D // tn
    row = lambda: pl.BlockSpec((tm, W), lambda i, j, l: (i, 0))
    gate = lambda c: pl.BlockSpec((tm, tn), lambda i, j, l: (i, goff + c * nd + j))
    wsp = lambda: pl.BlockSpec((None, W, tn), lambda i, j, l: (l[0], 0, j))
    return _call(
        _merge_kernel, lidx, (o_a, o_c, m_r, proj, proj, p['w_pa'], p['w_pc']),
        grid=(N_TOK // tm, nd),
        in_specs=[row(), row(), pl.BlockSpec((tm, tn), lambda i, j, l: (i, j)),
                  gate(0), gate(2), wsp(), wsp()],
        out_specs=pl.BlockSpec((tm, tn), lambda i, j, l: (i, j)),
        out_shape=jax.ShapeDtypeStruct((N_TOK, D), BF16),
        sem=("parallel", "parallel"), name="merge")


def _out_proj_kernel(l_ref, m_ref, w_ref, x_ref, ga_ref, o_ref, *, tm):
    row = _mod_row(pl.program_id(0), tm)
    o_ref[...] = x_ref[...] + ga_ref[pl.ds(row, 1), :] * _dot(m_ref[...], w_ref[...])


def _out_proj(lidx, merged, x, mod, w_out):
    tm, tn = 1024, 512
    nd = D // tn
    return _call(
        functools.partial(_out_proj_kernel, tm=tm), lidx, (merged, w_out, x, mod),
        grid=(N_TOK // tm, nd),
        in_specs=[pl.BlockSpec((tm, D), lambda i, j, l: (i, 0)),
                  pl.BlockSpec((None, D, tn), lambda i, j, l: (l[0], 0, j)),
                  pl.BlockSpec((tm, tn), lambda i, j, l: (i, j)),
                  pl.BlockSpec((None, 8, tn), lambda i, j, l: (l[0], 0, 2 * nd + j))],
        out_specs=pl.BlockSpec((tm, tn), lambda i, j, l: (i, j)),
        out_shape=jax.ShapeDtypeStruct((N_TOK, D), F32),
        sem=("parallel", "parallel"), name="out_proj")


def _ffn_kernel(l_ref, x_ref, g_ref, sh_ref, sc_ref, ga_ref, w1_ref, b1_ref, w2_ref, b2_ref,
                o_ref, h_ref, acc_ref, *, tm):
    j = pl.program_id(1)
    row = _mod_row(pl.program_id(0), tm)

    @pl.when(j == 0)
    def _():
        h = _modnorm(x_ref[...], g_ref[...], sc_ref[pl.ds(row, 1), :], sh_ref[pl.ds(row, 1), :])
        h_ref[...] = h.astype(BF16)
        acc_ref[...] = jnp.zeros_like(acc_ref)

    a = _dot(h_ref[...], w1_ref[...]) + b1_ref[...]
    a = jnp.square(jnp.maximum(a, 0.0))
    acc_ref[...] += _dot(a.astype(BF16), w2_ref[...])

    @pl.when(j == pl.num_programs(1) - 1)
    def _():
        o_ref[...] = x_ref[...] + ga_ref[pl.ds(row, 1), :] * (acc_ref[...] + b2_ref[...])


def _ffn(lidx, x, ln_g, mod, p):
    tm, tf = 512, 512
    return _call(
        functools.partial(_ffn_kernel, tm=tm), lidx,
        (x, ln_g, mod, mod, mod, p['w_ff1'], p['b_ff1'], p['w_ff2'], p['b_ff2']),
        grid=(N_TOK // tm, D_FF // tf),
        in_specs=[pl.BlockSpec((tm, D), lambda i, j, l: (i, 0)),
                  pl.BlockSpec((None, 1, D), lambda i, j, l: (l[0], 0, 0)),
                  pl.BlockSpec((None, 8, D), lambda i, j, l: (l[0], 0, 3)),
                  pl.BlockSpec((None, 8, D), lambda i, j, l: (l[0], 0, 4)),
                  pl.BlockSpec((None, 8, D), lambda i, j, l: (l[0], 0, 5)),
                  pl.BlockSpec((None, D, tf), lambda i, j, l: (l[0], 0, j)),
                  pl.BlockSpec((None, 1, tf), lambda i, j, l: (l[0], 0, j)),
                  pl.BlockSpec((None, tf, D), lambda i, j, l: (l[0], j, 0)),
                  pl.BlockSpec((None, 1, D), lambda i, j, l: (l[0], 0, 0))],
        out_specs=pl.BlockSpec((tm, D), lambda i, j, l: (i, 0)),
        out_shape=jax.ShapeDtypeStruct((N_TOK, D), F32),
        scratch=[pltpu.VMEM((tm, D), BF16), pltpu.VMEM((tm, D), F32)],
        sem=("parallel", "arbitrary"), name="ffn")


def _final_norm_kernel(x_ref, g_ref, o_ref):
    x = x_ref[...]
    o_ref[...] = x * lax.rsqrt(jnp.mean(x * x, -1, keepdims=True) + NORM_EPS) * g_ref[...]


def _final_norm(x, g):
    tm = 1024
    return pl.pallas_call(
        _final_norm_kernel,
        grid=(N_TOK // tm,),
        in_specs=[pl.BlockSpec((tm, D), lambda i: (i, 0)), pl.BlockSpec((1, D), lambda i: (0, 0))],
        out_specs=pl.BlockSpec((tm, D), lambda i: (i, 0)),
        out_shape=jax.ShapeDtypeStruct((N_TOK, D), F32),
        compiler_params=_params(("parallel",)), name="final_norm",
    )(x, g.reshape(1, D))


def kernel(x_prompt, x_sample, cache_k, cache_v, state_wkv, c, c_ctx, ln1_g, ln2_g, w_mod, b_mod, w_in, rpb, wkv_conv_w, wkv_conv_b, wkv_w0, wkv_w1, wkv_w2, wkv_a0, wkv_a1, wkv_a2, wkv_g1, wkv_g2, wkv_k_k, wkv_k_a, wkv_r_k, wkv_gn_g, wkv_gn_b, hy_conv_w, hy_conv_b, hy_f1, hy_fb1, hy_f2, hy_fb2, hy_freq, hy_f3, hy_d, w_pa, w_pr, w_pc, w_out, w_ff1, b_ff1, w_ff2, b_ff2, final_g):
    x = jnp.concatenate([x_prompt.reshape(N_CTX, D), x_sample.reshape(N_LAT, D)], 0)
    cvec = jnp.zeros((8, D), F32).at[0].set(c_ctx).at[1:1 + B_LAT].set(c)
    mod = _modulation(cvec, w_mod, b_mod)

    perm = (np.arange(H)[None, :] * DH + np.arange(DH)[:, None]).reshape(-1)
    pad_c = lambda a, n: jnp.pad(a, [(0, 0)] * (a.ndim - 1) + [(0, n - a.shape[-1])])
    pad_r = lambda a, n: jnp.pad(a, [(0, 0)] * (a.ndim - 2) + [(0, n - a.shape[-2]), (0, 0)])
    conv = jnp.concatenate([wkv_conv_w, wkv_conv_b[:, None, :]], 1)
    conv = jnp.transpose(conv.reshape(DEPTH, 4, 3, W)[..., perm], (0, 2, 3, 1))
    rowp = jnp.stack([wkv_w0[:, 0], wkv_w0[:, 1], wkv_a0[:, 0], wkv_a0[:, 1]], -1)[:, perm]
    post = jnp.stack([wkv_r_k.reshape(DEPTH, W), wkv_gn_g, wkv_gn_b, jnp.zeros((DEPTH, W), F32)],
                     -1)[:, perm]
    p = {
        'wkv_conv': conv, 'wkv_rowp': rowp, 'wkv_post': post,
        'wkv_w2t': jnp.swapaxes(wkv_w2[..., perm], -1, -2),
        'wkv_a2t': jnp.swapaxes(wkv_a2[..., perm], -1, -2),
        'wkv_g2t': jnp.swapaxes(wkv_g2[..., perm], -1, -2),
        'kk_chain': _head_param(wkv_k_k, 2 * B_CTX), 'ka_chain': _head_param(wkv_k_a, 2 * B_CTX),
        'hy_conv_w': hy_conv_w, 'hy_conv_b': hy_conv_b.reshape(DEPTH, 1, 3 * W),
        'hy_d': hy_d.reshape(DEPTH, 1, W),
        'hy_f1p': pad_c(pad_r(hy_f1, FH_PAD), FH_PAD), 'hy_fb1p': pad_c(hy_fb1, FH_PAD).reshape(DEPTH, 1, FH_PAD),
        'hy_f2p': pad_c(pad_r(hy_f2, FH_PAD), FH_PAD), 'hy_fb2p': pad_c(hy_fb2, FH_PAD).reshape(DEPTH, 1, FH_PAD),
        'hy_freqp': pad_c(hy_freq, FH_PAD).reshape(DEPTH, 1, FH_PAD), 'hy_f3p': pad_r(hy_f3, FH_PAD),
        'w_pa': w_pa.astype(BF16), 'w_pr': w_pr[:, perm].astype(BF16), 'w_pc': w_pc.astype(BF16),
        'w_ff1': w_ff1.astype(BF16), 'b_ff1': b_ff1.reshape(DEPTH, 1, D_FF),
        'w_ff2': w_ff2.astype(BF16), 'b_ff2': b_ff2.reshape(DEPTH, 1, D),
    }
    w_a = jnp.concatenate([w_in[..., :3 * W], w_in[..., 6 * W:]], -1).astype(BF16)
    rkv = w_in[..., 3 * W:6 * W].reshape(DEPTH, D, 3, W)[..., perm].reshape(DEPTH, D, 3 * W)
    w_t = jnp.concatenate(
        [rkv, wkv_w1[:, 0], wkv_w1[:, 1], wkv_a1[:, 0], wkv_a1[:, 1], wkv_g1,
         jnp.zeros((DEPTH, D, N_LORA - 4 * LORA - LORA_G), F32)], -1)
    w_t = jnp.swapaxes(w_t, 1, 2).astype(BF16)
    w_out_b = w_out.astype(BF16)
    ln1 = ln1_g.reshape(DEPTH, 1, D)
    ln2 = ln2_g.reshape(DEPTH, 1, D)
    bias_tab = _bias_table(rpb)
    ck = cache_k.reshape(B_LAT, DEPTH, L_CTX, W)
    cv = cache_v.reshape(B_LAT, DEPTH, L_CTX, W)
    s0_lat = jnp.transpose(state_wkv, (1, 5, 4, 0, 2, 3)).reshape(DEPTH, DH, DH, B_LAT * 2 * H)
    s0_ctx = jnp.zeros((DH, DH, B_CTX * 2 * H), F32)

    hy = {}
    for seq in (L_CTX, L_LAT):
        mats = _dft_mats(seq)
        filt = _hyena_filter(p, seq)
        hy[seq] = (mats, _spectrum(mats[0], mats[1], filt, seq))

    def layer(x, l):
        lidx = jnp.reshape(l, (1,)).astype(jnp.int32)
        proj = _in_proj(lidx, x, ln1, mod, w_a, transposed=False)
        proj_t = _in_proj(lidx, x, ln1, mod, w_t, transposed=True)
        oa = _attn_lat(lidx, proj, ck, cv, bias_tab, _attn_ctx(lidx, proj))
        m_r, s_ctx = _rwkv_branch(lidx, proj, proj_t, p, s0_ctx, None, nb=B_CTX, seq=L_CTX,
                                  rb_t=0, rb_tok=0)
        s0 = lax.dynamic_index_in_dim(s0_lat, l, 0, keepdims=False)
        m_r, _ = _rwkv_branch(lidx, proj, proj_t, p, s0, m_r, nb=B_LAT, seq=L_LAT,
                              rb_t=N_CTX // L_LAT, rb_tok=N_CTX // 256)
        oc = _hyena(lidx, proj, p, *hy[L_CTX], None, nb=B_CTX, seq=L_CTX, rb=0)
        oc = _hyena(lidx, proj, p, *hy[L_LAT], oc, nb=B_LAT, seq=L_LAT, rb=N_CTX // L_LAT)
        merged = _merge(lidx, oa, oc, m_r, proj, p)
        x = _out_proj(lidx, merged, x, mod, w_out_b)
        x = _ffn(lidx, x, ln2, mod, p)
        k_new = proj[:N_CTX, W:2 * W]
        v_new = proj[:N_CTX, 2 * W:3 * W]
        return x, (k_new, v_new, s_ctx)

    x, (ks, vs, ss) = lax.scan(layer, x, jnp.arange(DEPTH, dtype=jnp.int32))
    y = _final_norm(x, final_g)
    y_prompt = y[:N_CTX].reshape(B_CTX, L_CTX, D)
    y_sample = y[N_CTX:].reshape(B_LAT, L_LAT, D)
    new_k = jnp.transpose(ks.reshape(DEPTH, B_CTX, L_CTX, H, DH), (1, 0, 2, 3, 4))
    new_v = jnp.transpose(vs.reshape(DEPTH, B_CTX, L_CTX, H, DH), (1, 0, 2, 3, 4))
    new_s = jnp.transpose(ss.reshape(DEPTH, DH, DH, B_CTX, 2, H), (3, 0, 4, 5, 2, 1))
    return (y_prompt, y_sample, new_k, new_v, new_s)
```

```python
import functools
import math

import numpy as np
import jax
import jax.numpy as jnp
from jax import lax
from jax.experimental import pallas as pl
from jax.experimental.pallas import tpu as pltpu

F32 = jnp.float32
BF16 = jnp.bfloat16

D = 2048
DEPTH = 4
B_CTX, L_CTX = 32, 256
B_LAT, L_LAT = 4, 2048
N_CTX = B_CTX * L_CTX
N_LAT = B_LAT * L_LAT
N_TOK = N_CTX + N_LAT
H = 16
DH = 64
W = H * DH
GRID_W = 64
ROWS = L_LAT // GRID_W
WIN_R, WIN_C = 8, 16
NL = WIN_R * GRID_W
LORA = 64
LORA_G = 128
N_LORA = 512
N_A = 6 * W + 3 * D
HY_OFF = 3 * W
GL_OFF = 6 * W
N_T = 3 * W + N_LORA
D_FF = 4 * D
N_MOD = 6
NORM_EPS = 1e-6
GN_EPS = 64e-5
NEG_INF = -1e30
POS_BANDS = 16
FH_PAD = 128
LANES = 128
GB = LANES // (2 * H)
TCH = 128
VMEM_LIMIT = 56 * 1024 * 1024


def _params(sem):
    return pltpu.CompilerParams(dimension_semantics=sem, vmem_limit_bytes=VMEM_LIMIT)


def _call(kernel, lidx, args, *, grid, in_specs, out_specs, out_shape, scratch=(), sem, name,
          aliases=None):
    gs = pltpu.PrefetchScalarGridSpec(num_scalar_prefetch=1, grid=grid, in_specs=in_specs,
                                      out_specs=out_specs, scratch_shapes=list(scratch))
    return pl.pallas_call(kernel, grid_spec=gs, out_shape=out_shape,
                          compiler_params=_params(sem), name=name,
                          input_output_aliases=aliases or {})(lidx, *args)


_ANY = pl.BlockSpec(memory_space=pl.ANY)


def _mod_row(i, tm):
    start = i * tm
    return jnp.where(start < N_CTX, 0, 1 + (start - N_CTX) // L_LAT)


def _modnorm(x, g, sc, sh):
    y = x * lax.rsqrt(jnp.mean(x * x, -1, keepdims=True) + NORM_EPS)
    return (y * g) * (1.0 + sc) + sh


def _softplus(x):
    return jnp.maximum(x, 0.0) + jnp.log1p(jnp.exp(-jnp.abs(x)))


def _dot(a, b):
    return jnp.dot(a, b, preferred_element_type=F32)


def _dot_nt(a, b):
    return lax.dot_general(a, b, (((1,), (1,)), ((), ())), preferred_element_type=F32)


def _dot_tn(a, b):
    return lax.dot_general(a, b, (((0,), (0,)), ((), ())), preferred_element_type=F32)


def _split(x):
    hi = x.astype(BF16)
    lo = (x - hi.astype(F32)).astype(BF16)
    return hi, lo


def _dot3(ah, al, bh, bl):
    return _dot(ah, bh) + (_dot(al, bh) + _dot(ah, bl))


def _mod_kernel(c_ref, w_ref, b_ref, o_ref):
    c = c_ref[...]
    s = c * jax.nn.sigmoid(c)
    o_ref[...] = _dot(s.astype(BF16), w_ref[...].astype(BF16)) + b_ref[...]


def _modulation(cvec8, w_mod, b_mod):
    tn = 1024
    return pl.pallas_call(
        _mod_kernel,
        grid=(DEPTH, N_MOD * D // tn),
        in_specs=[pl.BlockSpec((8, D), lambda l, j: (0, 0)),
                  pl.BlockSpec((None, D, tn), lambda l, j: (l, 0, j)),
                  pl.BlockSpec((None, 1, tn), lambda l, j: (l, 0, j))],
        out_specs=pl.BlockSpec((None, 8, tn), lambda l, j: (l, 0, j)),
        out_shape=jax.ShapeDtypeStruct((DEPTH, 8, N_MOD * D), F32),
        compiler_params=_params(("parallel", "parallel")), name="modulation",
    )(cvec8, w_mod, b_mod.reshape(DEPTH, 1, N_MOD * D))


def _in_proj_kernel(l_ref, x_ref, g_ref, sh_ref, sc_ref, w_ref, o_ref, h_ref, *, tm):
    @pl.when(pl.program_id(1) == 0)
    def _():
        row = _mod_row(pl.program_id(0), tm)
        h = _modnorm(x_ref[...], g_ref[...], sc_ref[pl.ds(row, 1), :], sh_ref[pl.ds(row, 1), :])
        h_ref[...] = h.astype(BF16)

    o_ref[...] = _dot(h_ref[...], w_ref[...])


def _in_proj(lidx, x, ln_g, mod, w):
    tm, tn = 1024, 512
    n_out = w.shape[2]
    return _call(
        functools.partial(_in_proj_kernel, tm=tm), lidx, (x, ln_g, mod, mod, w),
        grid=(N_TOK // tm, n_out // tn),
        in_specs=[pl.BlockSpec((tm, D), lambda i, j, l: (i, 0)),
                  pl.BlockSpec((None, 1, D), lambda i, j, l: (l[0], 0, 0)),
                  pl.BlockSpec((None, 8, D), lambda i, j, l: (l[0], 0, 0)),
                  pl.BlockSpec((None, 8, D), lambda i, j, l: (l[0], 0, 1)),
                  pl.BlockSpec((None, D, tn), lambda i, j, l: (l[0], 0, j))],
        out_specs=[pl.BlockSpec((tm, tn), lambda i, j, l: (i, j)),
                   pl.BlockSpec((tm, D), lambda i, j, l: (i, 0))],
        out_shape=[jax.ShapeDtypeStruct((N_TOK, n_out), F32), jax.ShapeDtypeStruct((N_TOK, D), BF16)],
        sem=("parallel", "arbitrary"), name="in_proj")


def _in_proj_t_kernel(l_ref, h_ref, w_ref, o_ref):
    o_ref[...] = _dot_nt(w_ref[...], h_ref[...])


def _in_proj_t(lidx, h, w):
    tm, tn = 1024, 896
    n_out = w.shape[1]
    return _call(
        _in_proj_t_kernel, lidx, (h, w),
        grid=(N_TOK // tm, n_out // tn),
        in_specs=[pl.BlockSpec((tm, D), lambda i, j, l: (i, 0)),
                  pl.BlockSpec((None, tn, D), lambda i, j, l: (l[0], j, 0))],
        out_specs=pl.BlockSpec((tn, tm), lambda i, j, l: (j, i)),
        out_shape=jax.ShapeDtypeStruct((n_out, N_TOK), F32),
        sem=("parallel", "parallel"), name="in_proj_t")


def _kv_out_kernel(l_ref, k_ref, v_ref, ka_ref, va_ref, ko_ref, vo_ref):
    ko_ref[...] = k_ref[...]
    vo_ref[...] = v_ref[...]


def _kv_out(lidx, proj, k_acc, v_acc):
    cur = pl.BlockSpec((L_CTX, W), lambda b, l: (b, 1))
    cur_v = pl.BlockSpec((L_CTX, W), lambda b, l: (b, 2))
    dst = pl.BlockSpec((None, None, L_CTX, W), lambda b, l: (b, l[0], 0, 0))
    shape = jax.ShapeDtypeStruct((B_CTX, DEPTH, L_CTX, W), F32)
    return _call(
        _kv_out_kernel, lidx, (proj, proj, k_acc, v_acc),
        grid=(B_CTX,), in_specs=[cur, cur_v, _ANY, _ANY],
        out_specs=[dst, dst], out_shape=[shape, shape],
        sem=("parallel",), name="kv_out", aliases={3: 0, 4: 1})


ATT_CH = 8


def _attn_ctx_kernel(l_ref, q_ref, k_ref, v_ref, o_ref, s_ref):
    scale = DH ** -0.5
    for hh in range(ATT_CH):
        sl = slice(hh * DH, (hh + 1) * DH)
        q = (q_ref[:, sl] * scale).astype(BF16)
        s_ref[hh] = _dot_nt(q, k_ref[:, sl].astype(BF16))
    outs = []
    for hh in range(ATT_CH):
        sl = slice(hh * DH, (hh + 1) * DH)
        s = s_ref[hh]
        p = jnp.exp(s - jnp.max(s, -1, keepdims=True))
        den = jnp.sum(p, -1, keepdims=True)
        outs.append(_dot(p.astype(BF16), v_ref[:, sl].astype(BF16)) / den)
    o_ref[...] = jnp.concatenate(outs, -1).astype(BF16)


def _attn_ctx(lidx, proj):
    wb = ATT_CH * DH
    nq = W // wb
    return _call(
        _attn_ctx_kernel, lidx, (proj, proj, proj),
        grid=(B_CTX, nq),
        in_specs=[pl.BlockSpec((L_CTX, wb), lambda b, p, l: (b, p)),
                  pl.BlockSpec((L_CTX, wb), lambda b, p, l: (b, nq + p)),
                  pl.BlockSpec((L_CTX, wb), lambda b, p, l: (b, 2 * nq + p))],
        out_specs=pl.BlockSpec((L_CTX, wb), lambda b, p, l: (b, p)),
        out_shape=jax.ShapeDtypeStruct((N_TOK, W), BF16),
        scratch=[pltpu.VMEM((ATT_CH, L_CTX, L_CTX), F32)],
        sem=("parallel", "parallel"), name="attn_ctx")


ATT_RB = 8


def _attn_lat_kernel(l_ref, q_ref, k_ref, v_ref, kc_ref, vc_ref, bias_ref, alias_ref, o_ref,
                     kb_ref, vb_ref, kcb_ref, vcb_ref, s_ref):
    scale = DH ** -0.5
    for hh in range(2):
        sl = slice(hh * DH, (hh + 1) * DH)
        kb_ref[hh] = k_ref[:, sl].astype(BF16)
        vb_ref[hh] = v_ref[:, sl].astype(BF16)
        kcb_ref[hh] = kc_ref[:, sl].astype(BF16)
        vcb_ref[hh] = vc_ref[:, sl].astype(BF16)

    def window(r):
        r0 = jnp.clip(r - WIN_R // 2, 0, ROWS - WIN_R)
        return r - r0, pl.multiple_of(r * GRID_W, GRID_W), pl.multiple_of(r0 * GRID_W, GRID_W)

    def row_block(rr, carry):
        for i in range(ATT_RB):
            d, qrow, krow = window(rr * ATT_RB + i)
            for hh in range(2):
                sl = slice(hh * DH, (hh + 1) * DH)
                q = (q_ref[pl.ds(qrow, GRID_W), sl] * scale).astype(BF16)
                s_ref[2 * i + hh, :, :NL] = _dot_nt(q, kb_ref[hh, pl.ds(krow, NL), :]) + bias_ref[hh, d]
                s_ref[2 * i + hh, :, NL:] = _dot_nt(q, kcb_ref[hh])
        for i in range(ATT_RB):
            d, qrow, krow = window(rr * ATT_RB + i)
            outs = []
            for hh in range(2):
                s = s_ref[2 * i + hh]
                p = jnp.exp(s - jnp.max(s, -1, keepdims=True))
                den = jnp.sum(p, -1, keepdims=True)
                pb = p.astype(BF16)
                o = _dot(pb[:, :NL], vb_ref[hh, pl.ds(krow, NL), :]) + _dot(pb[:, NL:], vcb_ref[hh])
                outs.append(o / den)
            o_ref[pl.ds(qrow, GRID_W), :] = jnp.concatenate(outs, -1).astype(BF16)
        return carry

    lax.fori_loop(0, ROWS // ATT_RB, row_block, 0)


def _attn_lat(lidx, proj, cache_k, cache_v, bias_tab, o_ctx):
    nq = W // LANES
    rb = N_CTX // L_LAT
    return _call(
        _attn_lat_kernel, lidx, (proj, proj, proj, cache_k, cache_v, bias_tab, o_ctx),
        grid=(B_LAT, nq),
        in_specs=[pl.BlockSpec((L_LAT, LANES), lambda b, p, l: (rb + b, p)),
                  pl.BlockSpec((L_LAT, LANES), lambda b, p, l: (rb + b, nq + p)),
                  pl.BlockSpec((L_LAT, LANES), lambda b, p, l: (rb + b, 2 * nq + p)),
                  pl.BlockSpec((None, None, L_CTX, LANES), lambda b, p, l: (b, l[0], 0, p)),
                  pl.BlockSpec((None, None, L_CTX, LANES), lambda b, p, l: (b, l[0], 0, p)),
                  pl.BlockSpec((None, 2, WIN_R, GRID_W, NL), lambda b, p, l: (l[0], p, 0, 0, 0)),
                  _ANY],
        out_specs=pl.BlockSpec((L_LAT, LANES), lambda b, p, l: (rb + b, p)),
        out_shape=jax.ShapeDtypeStruct((N_TOK, W), BF16),
        scratch=[pltpu.VMEM((2, L_LAT, DH), BF16), pltpu.VMEM((2, L_LAT, DH), BF16),
                 pltpu.VMEM((2, L_CTX, DH), BF16), pltpu.VMEM((2, L_CTX, DH), BF16),
                 pltpu.VMEM((2 * ATT_RB, GRID_W, NL + L_CTX), F32)],
        sem=("parallel", "parallel"), name="attn_lat", aliases={7: 0})


def _bias_table(rpb):
    cq = np.arange(GRID_W)
    c0 = np.clip(cq - WIN_C // 2, 0, GRID_W - WIN_C)
    ck = np.arange(GRID_W)
    col_ok = (ck[None, :] >= c0[:, None]) & (ck[None, :] < c0[:, None] + WIN_C)
    dc = np.clip(ck[None, :] - cq[:, None], -(WIN_C - 1), WIN_C - 1) + (WIN_C - 1)
    onehot = (dc.reshape(-1)[None, :] == np.arange(2 * WIN_C - 1)[:, None]).astype(np.float32)
    cols = jnp.einsum('lhrc,cx->lhrx', rpb, jnp.asarray(onehot), precision=lax.Precision.HIGHEST)
    cols = cols.reshape(DEPTH, H, 2 * WIN_R - 1, GRID_W, GRID_W)
    tab = jnp.stack([cols[:, :, WIN_R - 1 - d:2 * WIN_R - 1 - d] for d in range(WIN_R)], 2)
    tab = jnp.transpose(tab, (0, 1, 2, 4, 3, 5))
    tab = jnp.where(col_ok[None, None, None, :, None, :], tab, NEG_INF)
    return tab.reshape(DEPTH, H, WIN_R, GRID_W, NL).astype(F32)


def _conv3(x, w_ref, b_ref):
    n = x.shape[0]
    row = lax.broadcasted_iota(jnp.int32, x.shape, 0)
    prev = jnp.where(row == 0, 0.0, pltpu.roll(x, 1, 0))
    nxt = jnp.where(row == n - 1, 0.0, pltpu.roll(x, n - 1, 0))
    return prev * w_ref[0:1, :] + x * w_ref[1:2, :] + nxt * w_ref[2:3, :] + b_ref[...]


def _conv3_t(x, p):
    n = x.shape[1]
    lane = lax.broadcasted_iota(jnp.int32, x.shape, 1)
    prev = jnp.where(lane == 0, 0.0, pltpu.roll(x, 1, 1))
    nxt = jnp.where(lane == n - 1, 0.0, pltpu.roll(x, n - 1, 1))
    return prev * p[:, 0:1] + x * p[:, 1:2] + nxt * p[:, 2:3] + p[:, 3:4]


def _wkv_prep_kernel(l_ref, r_ref, k_ref, v_ref, lora_ref, cp_ref, rp_ref, w2_ref, a2_ref, g2_ref,
                     ro_ref, ko_ref, vo_ref, d0_ref, d1_ref, a0o_ref, a1o_ref, go_ref):
    ro_ref[...] = _conv3_t(r_ref[...], cp_ref[0])
    ko_ref[...] = _conv3_t(k_ref[...], cp_ref[1])
    vo_ref[...] = _conv3_t(v_ref[...], cp_ref[2])
    rp = rp_ref[...]
    for e, (d_ref, ao_ref) in enumerate(((d0_ref, a0o_ref), (d1_ref, a1o_ref))):
        lw = jnp.tanh(lora_ref[e * LORA:(e + 1) * LORA, :]).astype(BF16)
        la = lora_ref[(2 + e) * LORA:(3 + e) * LORA, :].astype(BF16)
        w_log = rp[:, e:e + 1] + _dot(w2_ref[e].astype(BF16), lw)
        d_ref[...] = jnp.exp(-jnp.exp(-_softplus(-w_log) - 0.5))
        ao_ref[...] = jax.nn.sigmoid(rp[:, 2 + e:3 + e] + _dot(a2_ref[e].astype(BF16), la))
    lg = jax.nn.sigmoid(lora_ref[4 * LORA:4 * LORA + LORA_G, :]).astype(BF16)
    go_ref[...] = _dot(g2_ref[...].astype(BF16), lg)


def _wkv_prep(lidx, proj_t, p, *, nb, seq, rb):
    ct = 512 if seq <= 256 else 128
    nc = W // ct

    def row(c):
        return pl.BlockSpec((ct, seq), lambda b, j, l: (c * nc + j, rb + b))

    out = pl.BlockSpec((None, ct, seq), lambda b, j, l: (b, j, 0))
    return _call(
        _wkv_prep_kernel, lidx,
        (proj_t, proj_t, proj_t, proj_t, p['wkv_conv'], p['wkv_rowp'], p['wkv_w2t'], p['wkv_a2t'],
         p['wkv_g2t']),
        grid=(nb, nc),
        in_specs=[row(0), row(1), row(2),
                  pl.BlockSpec((N_LORA, seq), lambda b, j, l: (3 * W // N_LORA, rb + b)),
                  pl.BlockSpec((None, 3, ct, 4), lambda b, j, l: (l[0], 0, j, 0)),
                  pl.BlockSpec((None, ct, 4), lambda b, j, l: (l[0], j, 0)),
                  pl.BlockSpec((None, 2, ct, LORA), lambda b, j, l: (l[0], 0, j, 0)),
                  pl.BlockSpec((None, 2, ct, LORA), lambda b, j, l: (l[0], 0, j, 0)),
                  pl.BlockSpec((None, ct, LORA_G), lambda b, j, l: (l[0], j, 0))],
        out_specs=[out] * 8,
        out_shape=[jax.ShapeDtypeStruct((nb, W, seq), F32)] * 8,
        sem=("parallel", "parallel"), name=f"wkv_prep_{seq}")


def _to_chains_kernel(l_ref, *refs, shared):
    o_ref = refs[-1]
    xs = [x for x in refs[:-1] for _ in range(2)] if shared else refs[:-1]
    for k0 in range(0, DH, 8):
        tiles = []
        for k in range(k0, k0 + 8):
            m = jnp.concatenate([x[k * H:(k + 1) * H, :] for x in xs], 0)
            tiles.append(m.T)
        o_ref[:, k0:k0 + 8, :] = pltpu.einshape("ktc->tkc", jnp.stack(tiles, 0))


def _to_chains(lidx, x0, x1, *, nb, seq):
    def src(b):
        return pl.BlockSpec((None, W, TCH), lambda g, t, l: (GB * g + b, 0, t))

    shared = x1 is None
    if shared:
        args = (x0,) * GB
        in_specs = [src(b) for b in range(GB)]
    else:
        args = tuple(x0 if i % 2 == 0 else x1 for i in range(2 * GB))
        in_specs = [src(i // 2) for i in range(2 * GB)]
    return _call(
        functools.partial(_to_chains_kernel, shared=shared), lidx, args,
        grid=(nb // GB, seq // TCH),
        in_specs=in_specs,
        out_specs=pl.BlockSpec((TCH, DH, LANES), lambda g, t, l: (t, 0, g)),
        out_shape=jax.ShapeDtypeStruct((seq, DH, nb * 2 * H), F32),
        sem=("parallel", "parallel"), name=f"to_chains_{seq}")


def _wkv_kernel(l_ref, rf_ref, kf_ref, vf_ref, wf_ref, af_ref, rb_ref, kb_ref, vb_ref, wb_ref, ab_ref,
                kkp_ref, kap_ref, s0_ref, yf_ref, yb_ref, s_ref, st_ref, *, tb):
    @pl.when(pl.program_id(1) == 0)
    def _():
        s_ref[...] = s0_ref[...]

    kkp = kkp_ref[...]
    kap = kap_ref[...]
    lane = lax.broadcasted_iota(jnp.int32, (DH, LANES), 1)
    bwd = (lane // H) % 2 == 1
    kc = 32
    zero = jnp.zeros((DH, LANES), F32)

    def step(t, carry):
        tr = tb - 1 - t
        pick = lambda f_ref, b_ref: jnp.where(bwd, b_ref[tr], f_ref[t])
        kt = pick(kf_ref, kb_ref)
        at = pick(af_ref, ab_ref)
        vt = pick(vf_ref, vb_ref)
        kk = kt * kkp
        kk = kk * lax.rsqrt(jnp.sum(kk * kk, 0, keepdims=True) + 1e-12)
        st_ref[0] = kk
        st_ref[1] = kk * at
        st_ref[2] = kt * (1.0 + (at - 1.0) * kap)
        st_ref[3] = pick(wf_ref, wb_ref)
        st_ref[4] = pick(rf_ref, rb_ref)

        def sa_body(c, acc):
            a0, a1 = acc
            for j in range(kc):
                k = c * kc + j
                term = s_ref[k] * st_ref[0, pl.ds(k, 1), :]
                if j % 2 == 0:
                    a0 = a0 + term
                else:
                    a1 = a1 + term
            return a0, a1

        a0, a1 = lax.fori_loop(0, DH // kc, sa_body, (zero, zero))
        sa = -(a0 + a1)

        def up_body(c, acc):
            y0, y1 = acc
            for j in range(kc):
                k = c * kc + j
                sk = ((s_ref[k] * st_ref[3, pl.ds(k, 1), :] + sa * st_ref[1, pl.ds(k, 1), :])
                      + vt * st_ref[2, pl.ds(k, 1), :])
                s_ref[k] = sk
                term = sk * st_ref[4, pl.ds(k, 1), :]
                if j % 2 == 0:
                    y0 = y0 + term
                else:
                    y1 = y1 + term
            return y0, y1

        y0, y1 = lax.fori_loop(0, DH // kc, up_body, (zero, zero))
        y = y0 + y1
        yf_ref[t] = y
        yb_ref[tr] = y
        return carry

    lax.fori_loop(0, tb, step, 0)


def _wkv(lidx, r, k, v, w, a, kkp, kap, s0, *, seq):
    tb = 32
    chains = r.shape[-1]
    nt = seq // tb
    f_spec = pl.BlockSpec((tb, DH, LANES), lambda g, t, l: (t, 0, g))
    b_spec = pl.BlockSpec((tb, DH, LANES), lambda g, t, l: (nt - 1 - t, 0, g))
    par_spec = pl.BlockSpec((None, DH, LANES), lambda g, t, l: (l[0], 0, g))
    st_spec = pl.BlockSpec((DH, DH, LANES), lambda g, t, l: (0, 0, g))
    y_shape = jax.ShapeDtypeStruct((seq, DH, chains), F32)
    return _call(
        functools.partial(_wkv_kernel, tb=tb), lidx, (r, k, v, w, a, r, k, v, w, a, kkp, kap, s0),
        grid=(chains // LANES, nt),
        in_specs=[f_spec] * 5 + [b_spec] * 5 + [par_spec, par_spec, st_spec],
        out_specs=[f_spec, b_spec, st_spec],
        out_shape=[y_shape, y_shape, jax.ShapeDtypeStruct((DH, DH, chains), F32)],
        scratch=[pltpu.VMEM((5, DH, LANES), F32)],
        sem=("parallel", "arbitrary"), name=f"wkv_{seq}")


def _from_chains_kernel(l_ref, yf_ref, yb_ref, o_ref):
    for v0 in range(0, DH, 8):
        f3 = pltpu.einshape("tvc->vtc", yf_ref[:, v0:v0 + 8, :])
        b3 = pltpu.einshape("tvc->vtc", yb_ref[:, v0:v0 + 8, :])
        for i in range(8):
            v = v0 + i
            tf = f3[i].T
            tb = b3[i].T
            for b in range(GB):
                lo = b * 2 * H
                o_ref[b, v * H:(v + 1) * H, :] = tf[lo:lo + H] + tb[lo + H:lo + 2 * H]


def _from_chains(lidx, yf, yb, *, nb, seq):
    y_spec = pl.BlockSpec((TCH, DH, LANES), lambda g, t, l: (t, 0, g))
    return _call(
        _from_chains_kernel, lidx, (yf, yb),
        grid=(nb // GB, seq // TCH),
        in_specs=[y_spec, y_spec],
        out_specs=pl.BlockSpec((GB, W, TCH), lambda g, t, l: (g, 0, t)),
        out_shape=jax.ShapeDtypeStruct((nb, W, seq), F32),
        sem=("parallel", "parallel"), name=f"from_chains_{seq}")


def _wkv_post_kernel(l_ref, y_ref, r_ref, k_ref, v_ref, g_ref, gate_ref, par_ref, w_ref, *rest):
    o_ref = rest[-1]
    t = y_ref.shape[-1]
    par = par_ref[...]
    y = y_ref[...].reshape(DH, H, t)
    mu = jnp.mean(y, 0, keepdims=True)
    yc = y - mu
    var = jnp.mean(yc * yc, 0, keepdims=True)
    yn = (yc * lax.rsqrt(var + GN_EPS)).reshape(W, t) * par[:, 1:2] + par[:, 2:3]
    rk = (r_ref[...] * k_ref[...] * par[:, 0:1]).reshape(DH, H, t)
    bonus = jnp.broadcast_to(jnp.sum(rk, 0, keepdims=True), (DH, H, t)).reshape(W, t) * v_ref[...]
    o = ((yn + bonus) * g_ref[...]).astype(BF16)
    o_ref[...] = jax.nn.sigmoid(gate_ref[...]) * _dot_tn(o, w_ref[...])


def _wkv_post(lidx, ysum, r, k, v, g, proj, p, prev, *, nb, seq, rb):
    tt = 256
    nt = seq // tt
    t_spec = pl.BlockSpec((None, W, tt), lambda b, t, l: (b, 0, t))
    tok = lambda b, t: rb + b * nt + t
    args = (ysum, r, k, v, g, proj, p['wkv_post'], p['w_pr'])
    in_specs = [t_spec] * 5 + [
        pl.BlockSpec((tt, D), lambda b, t, l: (tok(b, t), (GL_OFF + D) // D)),
        pl.BlockSpec((None, W, 4), lambda b, t, l: (l[0], 0, 0)),
        pl.BlockSpec((None, W, D), lambda b, t, l: (l[0], 0, 0))]
    aliases = None
    if prev is not None:
        args = args + (prev,)
        in_specs = in_specs + [_ANY]
        aliases = {len(args): 0}
    return _call(
        _wkv_post_kernel, lidx, args,
        grid=(nb, nt), in_specs=in_specs,
        out_specs=pl.BlockSpec((tt, D), lambda b, t, l: (tok(b, t), 0)),
        out_shape=jax.ShapeDtypeStruct((N_TOK, D), F32),
        sem=("parallel", "parallel"), name=f"wkv_post_{seq}", aliases=aliases)


def _head_param(p, reps):
    t = jnp.transpose(p.reshape(DEPTH, H, DH), (0, 2, 1))
    return jnp.tile(t, (1, 1, reps))


def _rwkv_branch(lidx, proj, proj_t, p, s0_chain, prev, *, nb, seq, rb_t, rb_tok):
    r, k, v, d0, d1, a0, a1, g = _wkv_prep(lidx, proj_t, p, nb=nb, seq=seq, rb=rb_t)
    tc = functools.partial(_to_chains, lidx, nb=nb, seq=seq)
    chains = nb * 2 * H
    yf, yb, s_fin = _wkv(lidx, tc(r, None), tc(k, None), tc(v, None), tc(d0, d1), tc(a0, a1),
                         p['kk_chain'][:, :, :chains], p['ka_chain'][:, :, :chains], s0_chain, seq=seq)
    ysum = _from_chains(lidx, yf, yb, nb=nb, seq=seq)
    m_r = _wkv_post(lidx, ysum, r, k, v, g, proj, p, prev, nb=nb, seq=seq, rb=rb_tok)
    return m_r, s_fin


def _pos_features(seq):
    t = np.linspace(0.0, 1.0, seq, dtype=np.float32)[:, None]
    w = 2.0 * np.pi * np.arange(seq, dtype=np.float32)[:, None] / seq
    f = np.linspace(1e-4, POS_BANDS - 1, POS_BANDS, dtype=np.float32)[None, :]
    z = np.concatenate([t, np.cos(f * w), -np.sin(f * w)], -1).astype(np.float32)
    zp = np.zeros((seq, FH_PAD), np.float32)
    zp[:, :z.shape[1]] = z
    dist = (np.abs(np.arange(seq) - seq // 2).astype(np.float32) / seq)[:, None]
    deltas = np.abs(np.linspace(math.log(1e-2) / 1.5, math.log(1e-2) / 0.3, W,
                                dtype=np.float32))[None, :]
    return zp, dist, deltas


def _filter_kernel(zp_ref, dist_ref, del_ref, f1_ref, b1_ref, f2_ref, b2_ref, fr_ref, f3_ref, o_ref):
    hi = lax.Precision.HIGHEST
    fr = fr_ref[...]
    t = jnp.sin(fr * (jnp.dot(zp_ref[...], f1_ref[...], precision=hi,
                              preferred_element_type=F32) + b1_ref[...]))
    t = jnp.sin(fr * (jnp.dot(t, f2_ref[...], precision=hi, preferred_element_type=F32) + b2_ref[...]))
    filt = jnp.dot(t, f3_ref[...], precision=hi, preferred_element_type=F32)
    filt = filt * jnp.exp(-dist_ref[...] * del_ref[...])
    o_ref[...] = filt / (jnp.sum(jnp.abs(filt), 0, keepdims=True) + 1e-6)


def _hyena_filter(p, seq):
    ct = 256
    zp, dist, deltas = _pos_features(seq)
    full = lambda shape: pl.BlockSpec((None,) + shape, lambda l, j: (l,) + (0,) * len(shape))
    return pl.pallas_call(
        _filter_kernel,
        grid=(DEPTH, W // ct),
        in_specs=[pl.BlockSpec((seq, FH_PAD), lambda l, j: (0, 0)),
                  pl.BlockSpec((seq, 1), lambda l, j: (0, 0)),
                  pl.BlockSpec((1, ct), lambda l, j: (0, j)),
                  full((FH_PAD, FH_PAD)), full((1, FH_PAD)), full((FH_PAD, FH_PAD)),
                  full((1, FH_PAD)), full((1, FH_PAD)),
                  pl.BlockSpec((None, FH_PAD, ct), lambda l, j: (l, 0, j))],
        out_specs=pl.BlockSpec((None, seq, ct), lambda l, j: (l, 0, j)),
        out_shape=jax.ShapeDtypeStruct((DEPTH, seq, W), F32),
        compiler_params=_params(("parallel", "parallel")), name=f"hyena_filter_{seq}",
    )(jnp.asarray(zp), jnp.asarray(dist), jnp.asarray(deltas),
      p['hy_f1p'], p['hy_fb1p'], p['hy_f2p'], p['hy_fb2p'], p['hy_freqp'], p['hy_f3p'])


def _freq_block(seq):
    return min(seq, 256)


def _dft_mats(seq):
    n = 2 * seq
    fq = _freq_block(seq)
    k = jnp.arange(seq, dtype=jnp.int32)
    t = jnp.arange(seq, dtype=jnp.int32)
    ph = ((2 * k[:, None] + 1) * t[None, :]) % (2 * n)
    ang = ph.astype(F32) * np.float32(np.pi / n)
    fwd = jnp.stack([jnp.cos(ang).reshape(seq // fq, fq, seq),
                     jnp.sin(ang).reshape(seq // fq, fq, seq)], 1).reshape(2 * seq, seq)
    m = t + seq // 2
    ph2 = ((2 * k[None, :] + 1) * m[:, None]) % (2 * n)
    ang2 = ph2.astype(F32) * np.float32(np.pi / n)
    inv = jnp.stack([jnp.cos(ang2).reshape(seq, seq // fq, fq),
                     jnp.sin(ang2).reshape(seq, seq // fq, fq)], 2).reshape(seq, 2 * seq)
    inv = inv * np.float32(2.0 / n)
    fwd_hi, fwd_lo = _split(fwd)
    return fwd_hi, fwd_lo, inv.astype(BF16)


def _spectrum_kernel(ah_ref, al_ref, b_ref, o_ref):
    bh, bl = _split(b_ref[...])
    o_ref[...] = _dot3(ah_ref[...], al_ref[...], bh, bl)


def _spectrum(fwd_hi, fwd_lo, filt, seq):
    tmm, tn = 256, 512
    return pl.pallas_call(
        _spectrum_kernel,
        grid=(DEPTH, 2 * seq // tmm, W // tn),
        in_specs=[pl.BlockSpec((tmm, seq), lambda l, i, j: (i, 0)),
                  pl.BlockSpec((tmm, seq), lambda l, i, j: (i, 0)),
                  pl.BlockSpec((None, seq, tn), lambda l, i, j: (l, 0, j))],
        out_specs=pl.BlockSpec((None, tmm, tn), lambda l, i, j: (l, i, j)),
        out_shape=jax.ShapeDtypeStruct((DEPTH, 2 * seq, W), F32),
        compiler_params=_params(("parallel", "parallel", "parallel")), name=f"hyena_spectrum_{seq}",
    )(fwd_hi, fwd_lo, filt)


def _hyena_kernel(l_ref, x0_ref, x1_ref, vv_ref, cw0_ref, cw1_ref, cw2_ref, cb0_ref, cb1_ref, cb2_ref,
                  fwd_ref, inv_ref, h_ref, d_ref, *rest):
    o_ref, z_ref, zb_ref, acc_ref = rest[-4:]
    fb = pl.program_id(2)
    fq = h_ref.shape[0] // 2

    @pl.when(fb == 0)
    def _():
        z = _conv3(vv_ref[...], cw2_ref, cb2_ref) * _conv3(x1_ref[...], cw1_ref, cb1_ref)
        z_ref[...] = z
        zb_ref[...] = z.astype(BF16)
        acc_ref[...] = jnp.zeros_like(acc_ref)

    zf = _dot(fwd_ref[...], zb_ref[...])
    zc, zs = zf[:fq], zf[fq:]
    hc, hs = h_ref[:fq, :], h_ref[fq:, :]
    pr = zc * hc - zs * hs
    pq = zc * hs + zs * hc
    acc_ref[...] += _dot(inv_ref[...], jnp.concatenate([pr, pq], 0).astype(BF16))

    @pl.when(fb == pl.num_programs(2) - 1)
    def _():
        y = acc_ref[...] + z_ref[...] * d_ref[...]
        o_ref[...] = (_conv3(x0_ref[...], cw0_ref, cb0_ref) * y).astype(BF16)


def _hyena(lidx, proj, p, mats, spec, prev, *, nb, seq, rb):
    ct = 512
    fq = _freq_block(seq)
    nc = W // ct
    off = HY_OFF // ct
    fwd_hi, _, inv_hi = mats

    def col(c):
        return pl.BlockSpec((seq, ct), lambda b, j, f, l: (rb + b, off + c * nc + j))

    def cw(c):
        return pl.BlockSpec((None, 3, ct), lambda b, j, f, l: (l[0], 0, c * nc + j))

    def cb(c):
        return pl.BlockSpec((None, 1, ct), lambda b, j, f, l: (l[0], 0, c * nc + j))

    args = (proj, proj, proj, p['hy_conv_w'], p['hy_conv_w'], p['hy_conv_w'],
            p['hy_conv_b'], p['hy_conv_b'], p['hy_conv_b'], fwd_hi, inv_hi, spec, p['hy_d'])
    in_specs = [col(0), col(1), col(2), cw(0), cw(1), cw(2), cb(0), cb(1), cb(2),
                pl.BlockSpec((2 * fq, seq), lambda b, j, f, l: (f, 0)),
                pl.BlockSpec((seq, 2 * fq), lambda b, j, f, l: (0, f)),
                pl.BlockSpec((None, 2 * fq, ct), lambda b, j, f, l: (l[0], f, j)),
                pl.BlockSpec((None, 1, ct), lambda b, j, f, l: (l[0], 0, j))]
    aliases = None
    if prev is not None:
        args = args + (prev,)
        in_specs = in_specs + [_ANY]
        aliases = {len(args): 0}
    return _call(
        _hyena_kernel, lidx, args,
        grid=(nb, nc, seq // fq), in_specs=in_specs,
        out_specs=pl.BlockSpec((seq, ct), lambda b, j, f, l: (rb + b, j)),
        out_shape=jax.ShapeDtypeStruct((N_TOK, W), BF16),
        scratch=[pltpu.VMEM((seq, ct), F32), pltpu.VMEM((seq, ct), BF16), pltpu.VMEM((seq, ct), F32)],
        sem=("parallel", "parallel", "arbitrary"), name=f"hyena_{seq}", aliases=aliases)


def _merge_kernel(l_ref, oa_ref, oc_ref, mr_ref, ga_ref, gc_ref, wa_ref, wc_ref, o_ref):
    m = (jax.nn.sigmoid(ga_ref[...]) * _dot(oa_ref[...], wa_ref[...]) + mr_ref[...]
         + jax.nn.sigmoid(gc_ref[...]) * _dot(oc_ref[...], wc_ref[...]))
    o_ref[...] = m.astype(BF16)


def _merge(lidx, o_a, o_c, m_r, proj, p):
    tm, tn = 1024, 512
    goff = GL_OFF // tn
    nd = D // tn
    row = lambda: pl.BlockSpec((tm, W), lambda i, j, l: (i, 0))
    gate = lambda c: pl.BlockSpec((tm, tn), lambda i, j, l: (i, goff + c * nd + j))
    wsp = lambda: pl.BlockSpec((None, W, tn), lambda i, j, l: (l[0], 0, j))
    return _call(
        _merge_kernel, lidx, (o_a, o_c, m_r, proj, proj, p['w_pa'], p['w_pc']),
        grid=(N_TOK // tm, nd),
        in_specs=[row(), row(), pl.BlockSpec((tm, tn), lambda i, j, l: (i, j)),
                  gate(0), gate(2), wsp(), wsp()],
        out_specs=pl.BlockSpec((tm, tn), lambda i, j, l: (i, j)),
        out_shape=jax.ShapeDtypeStruct((N_TOK, D), BF16),
        sem=("parallel", "parallel"), name="merge")


def _out_proj_kernel(l_ref, m_ref, w_ref, x_ref, ga_ref, g_ref, sh_ref, sc_ref, o_ref, h_ref, *, tm):
    row = _mod_row(pl.program_id(0), tm)
    xn = x_ref[...] + ga_ref[pl.ds(row, 1), :] * _dot(m_ref[...], w_ref[...])
    o_ref[...] = xn
    h_ref[...] = _modnorm(xn, g_ref[...], sc_ref[pl.ds(row, 1), :], sh_ref[pl.ds(row, 1), :]).astype(BF16)


def _out_proj(lidx, merged, x, mod, w_out, ln_g):
    tm = 512
    rows = pl.BlockSpec((tm, D), lambda i, l: (i, 0))
    chunk = lambda c: pl.BlockSpec((None, 8, D), lambda i, l: (l[0], 0, c))
    return _call(
        functools.partial(_out_proj_kernel, tm=tm), lidx, (merged, w_out, x, mod, ln_g, mod, mod),
        grid=(N_TOK // tm,),
        in_specs=[rows, pl.BlockSpec((None, D, D), lambda i, l: (l[0], 0, 0)), rows, chunk(2),
                  pl.BlockSpec((None, 1, D), lambda i, l: (l[0], 0, 0)), chunk(3), chunk(4)],
        out_specs=[rows, rows],
        out_shape=[jax.ShapeDtypeStruct((N_TOK, D), F32), jax.ShapeDtypeStruct((N_TOK, D), BF16)],
        sem=("parallel",), name="out_proj")


def _ffn_kernel(l_ref, x_ref, h_ref, ga_ref, w1_ref, b1_ref, w2_ref, b2_ref, o_ref, acc_ref, *, tm):
    j = pl.program_id(1)
    row = _mod_row(pl.program_id(0), tm)

    @pl.when(j == 0)
    def _():
        acc_ref[...] = jnp.zeros_like(acc_ref)

    a = _dot(h_ref[...], w1_ref[...]) + b1_ref[...]
    a = jnp.square(jnp.maximum(a, 0.0))
    acc_ref[...] += _dot(a.astype(BF16), w2_ref[...])

    @pl.when(j == pl.num_programs(1) - 1)
    def _():
        o_ref[...] = x_ref[...] + ga_ref[pl.ds(row, 1), :] * (acc_ref[...] + b2_ref[...])


def _ffn(lidx, x, h, mod, p):
    tm, tf = 512, 512
    return _call(
        functools.partial(_ffn_kernel, tm=tm), lidx,
        (x, h, mod, p['w_ff1'], p['b_ff1'], p['w_ff2'], p['b_ff2']),
        grid=(N_TOK // tm, D_FF // tf),
        in_specs=[pl.BlockSpec((tm, D), lambda i, j, l: (i, 0)),
                  pl.BlockSpec((tm, D), lambda i, j, l: (i, 0)),
                  pl.BlockSpec((None, 8, D), lambda i, j, l: (l[0], 0, 5)),
                  pl.BlockSpec((None, D, tf), lambda i, j, l: (l[0], 0, j)),
                  pl.BlockSpec((None, 1, tf), lambda i, j, l: (l[0], 0, j)),
                  pl.BlockSpec((None, tf, D), lambda i, j, l: (l[0], j, 0)),
                  pl.BlockSpec((None, 1, D), lambda i, j, l: (l[0], 0, 0))],
        out_specs=pl.BlockSpec((tm, D), lambda i, j, l: (i, 0)),
        out_shape=jax.ShapeDtypeStruct((N_TOK, D), F32),
        scratch=[pltpu.VMEM((tm, D), F32)],
        sem=("parallel", "arbitrary"), name="ffn")


def _final_norm_kernel(x_ref, g_ref, o_ref):
    x = x_ref[...]
    o_ref[...] = x * lax.rsqrt(jnp.mean(x * x, -1, keepdims=True) + NORM_EPS) * g_ref[...]


def _final_norm(x, g, row0, nrows):
    tm = 1024
    rb = row0 // tm
    return pl.pallas_call(
        _final_norm_kernel,
        grid=(nrows // tm,),
        in_specs=[pl.BlockSpec((tm, D), lambda i: (rb + i, 0)), pl.BlockSpec((1, D), lambda i: (0, 0))],
        out_specs=pl.BlockSpec((tm, D), lambda i: (i, 0)),
        out_shape=jax.ShapeDtypeStruct((nrows, D), F32),
        compiler_params=_params(("parallel",)), name="final_norm",
    )(x, g.reshape(1, D))


def kernel(x_prompt, x_sample, cache_k, cache_v, state_wkv, c, c_ctx, ln1_g, ln2_g, w_mod, b_mod, w_in, rpb, wkv_conv_w, wkv_conv_b, wkv_w0, wkv_w1, wkv_w2, wkv_a0, wkv_a1, wkv_a2, wkv_g1, wkv_g2, wkv_k_k, wkv_k_a, wkv_r_k, wkv_gn_g, wkv_gn_b, hy_conv_w, hy_conv_b, hy_f1, hy_fb1, hy_f2, hy_fb2, hy_freq, hy_f3, hy_d, w_pa, w_pr, w_pc, w_out, w_ff1, b_ff1, w_ff2, b_ff2, final_g):
    x = jnp.concatenate([x_prompt.reshape(N_CTX, D), x_sample.reshape(N_LAT, D)], 0)
    cvec = jnp.zeros((8, D), F32).at[0].set(c_ctx).at[1:1 + B_LAT].set(c)
    mod = _modulation(cvec, w_mod, b_mod)

    perm = (np.arange(H)[None, :] * DH + np.arange(DH)[:, None]).reshape(-1)
    pad_c = lambda a, n: jnp.pad(a, [(0, 0)] * (a.ndim - 1) + [(0, n - a.shape[-1])])
    pad_r = lambda a, n: jnp.pad(a, [(0, 0)] * (a.ndim - 2) + [(0, n - a.shape[-2]), (0, 0)])
    conv = jnp.concatenate([wkv_conv_w, wkv_conv_b[:, None, :]], 1)
    conv = jnp.transpose(conv.reshape(DEPTH, 4, 3, W)[..., perm], (0, 2, 3, 1))
    rowp = jnp.stack([wkv_w0[:, 0], wkv_w0[:, 1], wkv_a0[:, 0], wkv_a0[:, 1]], -1)[:, perm]
    post = jnp.stack([wkv_r_k.reshape(DEPTH, W), wkv_gn_g, wkv_gn_b, jnp.zeros((DEPTH, W), F32)],
                     -1)[:, perm]
    p = {
        'wkv_conv': conv, 'wkv_rowp': rowp, 'wkv_post': post,
        'wkv_w2t': jnp.swapaxes(wkv_w2[..., perm], -1, -2),
        'wkv_a2t': jnp.swapaxes(wkv_a2[..., perm], -1, -2),
        'wkv_g2t': jnp.swapaxes(wkv_g2[..., perm], -1, -2),
        'kk_chain': _head_param(wkv_k_k, 2 * B_CTX), 'ka_chain': _head_param(wkv_k_a, 2 * B_CTX),
        'hy_conv_w': hy_conv_w, 'hy_conv_b': hy_conv_b.reshape(DEPTH, 1, 3 * W),
        'hy_d': hy_d.reshape(DEPTH, 1, W),
        'hy_f1p': pad_c(pad_r(hy_f1, FH_PAD), FH_PAD), 'hy_fb1p': pad_c(hy_fb1, FH_PAD).reshape(DEPTH, 1, FH_PAD),
        'hy_f2p': pad_c(pad_r(hy_f2, FH_PAD), FH_PAD), 'hy_fb2p': pad_c(hy_fb2, FH_PAD).reshape(DEPTH, 1, FH_PAD),
        'hy_freqp': pad_c(hy_freq, FH_PAD).reshape(DEPTH, 1, FH_PAD), 'hy_f3p': pad_r(hy_f3, FH_PAD),
        'w_pa': w_pa.astype(BF16), 'w_pr': w_pr[:, perm].astype(BF16), 'w_pc': w_pc.astype(BF16),
        'w_ff1': w_ff1.astype(BF16), 'b_ff1': b_ff1.reshape(DEPTH, 1, D_FF),
        'w_ff2': w_ff2.astype(BF16), 'b_ff2': b_ff2.reshape(DEPTH, 1, D),
    }
    w_a = jnp.concatenate([w_in[..., :3 * W], w_in[..., 6 * W:]], -1).astype(BF16)
    rkv = w_in[..., 3 * W:6 * W].reshape(DEPTH, D, 3, W)[..., perm].reshape(DEPTH, D, 3 * W)
    w_t = jnp.concatenate(
        [rkv, wkv_w1[:, 0], wkv_w1[:, 1], wkv_a1[:, 0], wkv_a1[:, 1], wkv_g1,
         jnp.zeros((DEPTH, D, N_LORA - 4 * LORA - LORA_G), F32)], -1)
    w_t = jnp.swapaxes(w_t, 1, 2).astype(BF16)
    w_out_b = w_out.astype(BF16)
    ln1 = ln1_g.reshape(DEPTH, 1, D)
    ln2 = ln2_g.reshape(DEPTH, 1, D)
    bias_tab = _bias_table(rpb)
    ck = cache_k.reshape(B_LAT, DEPTH, L_CTX, W)
    cv = cache_v.reshape(B_LAT, DEPTH, L_CTX, W)
    s0_lat = jnp.transpose(state_wkv, (1, 5, 4, 0, 2, 3)).reshape(DEPTH, DH, DH, B_LAT * 2 * H)
    s0_ctx = jnp.zeros((DH, DH, B_CTX * 2 * H), F32)

    hy = {}
    for seq in (L_CTX, L_LAT):
        mats = _dft_mats(seq)
        filt = _hyena_filter(p, seq)
        hy[seq] = (mats, _spectrum(mats[0], mats[1], filt, seq))

    def layer(carry, l):
        x, k_acc, v_acc = carry
        lidx = jnp.reshape(l, (1,)).astype(jnp.int32)
        proj, h = _in_proj(lidx, x, ln1, mod, w_a)
        k_acc, v_acc = _kv_out(lidx, proj, k_acc, v_acc)
        proj_t = _in_proj_t(lidx, h, w_t)
        oa = _attn_lat(lidx, proj, ck, cv, bias_tab, _attn_ctx(lidx, proj))
        m_r, s_ctx = _rwkv_branch(lidx, proj, proj_t, p, s0_ctx, None, nb=B_CTX, seq=L_CTX,
                                  rb_t=0, rb_tok=0)
        s0 = lax.dynamic_index_in_dim(s0_lat, l, 0, keepdims=False)
        m_r, _ = _rwkv_branch(lidx, proj, proj_t, p, s0, m_r, nb=B_LAT, seq=L_LAT,
                              rb_t=N_CTX // L_LAT, rb_tok=N_CTX // 256)
        oc = _hyena(lidx, proj, p, *hy[L_CTX], None, nb=B_CTX, seq=L_CTX, rb=0)
        oc = _hyena(lidx, proj, p, *hy[L_LAT], oc, nb=B_LAT, seq=L_LAT, rb=N_CTX // L_LAT)
        merged = _merge(lidx, oa, oc, m_r, proj, p)
        x, h2 = _out_proj(lidx, merged, x, mod, w_out_b, ln2)
        x = _ffn(lidx, x, h2, mod, p)
        return (x, k_acc, v_acc), s_ctx

    kv0 = jnp.zeros((B_CTX, DEPTH, L_CTX, W), F32)
    (x, k_acc, v_acc), ss = lax.scan(layer, (x, kv0, kv0), jnp.arange(DEPTH, dtype=jnp.int32))
    y_prompt = _final_norm(x, final_g, 0, N_CTX).reshape(B_CTX, L_CTX, D)
    y_sample = _final_norm(x, final_g, N_CTX, N_LAT).reshape(B_LAT, L_LAT, D)
    new_k = k_acc.reshape(B_CTX, DEPTH, L_CTX, H, DH)
    new_v = v_acc.reshape(B_CTX, DEPTH, L_CTX, H, DH)
    new_s = jnp.transpose(ss.reshape(DEPTH, DH, DH, B_CTX, 2, H), (3, 0, 4, 5, 2, 1))
    return (y_prompt, y_sample, new_k, new_v, new_s)
```

```python
import functools
import math

import numpy as np
import jax
import jax.numpy as jnp
from jax import lax
from jax.experimental import pallas as pl
from jax.experimental.pallas import tpu as pltpu

F32 = jnp.float32
BF16 = jnp.bfloat16

D = 2048
DEPTH = 4
B_CTX, L_CTX = 32, 256
B_LAT, L_LAT = 4, 2048
N_CTX = B_CTX * L_CTX
N_LAT = B_LAT * L_LAT
N_TOK = N_CTX + N_LAT
H = 16
DH = 64
W = H * DH
GRID_W = 64
ROWS = L_LAT // GRID_W
WIN_R, WIN_C = 8, 16
NL = WIN_R * GRID_W
LORA = 64
LORA_G = 128
N_LORA = 512
N_A = 6 * W + 3 * D
HY_OFF = 3 * W
GL_OFF = 6 * W
N_T = 3 * W + N_LORA
D_FF = 4 * D
N_MOD = 6
NORM_EPS = 1e-6
GN_EPS = 64e-5
NEG_INF = -1e30
POS_BANDS = 16
FH_PAD = 128
LANES = 128
GB = LANES // (2 * H)
TCH = 128
VMEM_LIMIT = 56 * 1024 * 1024


def _params(sem):
    return pltpu.CompilerParams(dimension_semantics=sem, vmem_limit_bytes=VMEM_LIMIT)


def _call(kernel, lidx, args, *, grid, in_specs, out_specs, out_shape, scratch=(), sem, name,
          aliases=None):
    gs = pltpu.PrefetchScalarGridSpec(num_scalar_prefetch=1, grid=grid, in_specs=in_specs,
                                      out_specs=out_specs, scratch_shapes=list(scratch))
    return pl.pallas_call(kernel, grid_spec=gs, out_shape=out_shape,
                          compiler_params=_params(sem), name=name,
                          input_output_aliases=aliases or {})(lidx, *args)


_ANY = pl.BlockSpec(memory_space=pl.ANY)


def _mod_row(i, tm):
    start = i * tm
    return jnp.where(start < N_CTX, 0, 1 + (start - N_CTX) // L_LAT)


def _modnorm(x, g, sc, sh):
    y = x * lax.rsqrt(jnp.mean(x * x, -1, keepdims=True) + NORM_EPS)
    return (y * g) * (1.0 + sc) + sh


def _dot(a, b):
    return jnp.dot(a, b, preferred_element_type=F32)


def _dot_nt(a, b):
    return lax.dot_general(a, b, (((1,), (1,)), ((), ())), preferred_element_type=F32)


def _dot_tn(a, b):
    return lax.dot_general(a, b, (((0,), (0,)), ((), ())), preferred_element_type=F32)


def _mod_kernel(c_ref, w_ref, b_ref, o_ref):
    c = c_ref[...]
    s = c * jax.nn.sigmoid(c)
    o_ref[...] = _dot(s.astype(BF16), w_ref[...].astype(BF16)) + b_ref[...]


def _modulation(cvec8, w_mod, b_mod):
    tn = 1024
    return pl.pallas_call(
        _mod_kernel,
        grid=(DEPTH, N_MOD * D // tn),
        in_specs=[pl.BlockSpec((8, D), lambda l, j: (0, 0)),
                  pl.BlockSpec((None, D, tn), lambda l, j: (l, 0, j)),
                  pl.BlockSpec((None, 1, tn), lambda l, j: (l, 0, j))],
        out_specs=pl.BlockSpec((None, 8, tn), lambda l, j: (l, 0, j)),
        out_shape=jax.ShapeDtypeStruct((DEPTH, 8, N_MOD * D), F32),
        compiler_params=_params(("parallel", "parallel")), name="modulation",
    )(cvec8, w_mod, b_mod.reshape(DEPTH, 1, N_MOD * D))


def _in_proj_kernel(l_ref, x_ref, g_ref, sh_ref, sc_ref, w_ref, o_ref, h_ref, *, tm):
    @pl.when(pl.program_id(1) == 0)
    def _():
        row = _mod_row(pl.program_id(0), tm)
        h = _modnorm(x_ref[...], g_ref[...], sc_ref[pl.ds(row, 1), :], sh_ref[pl.ds(row, 1), :])
        h_ref[...] = h.astype(BF16)

    o_ref[...] = _dot(h_ref[...], w_ref[...])


def _in_proj(lidx, x, ln_g, mod, w):
    tm, tn = 1024, 512
    n_out = w.shape[2]
    return _call(
        functools.partial(_in_proj_kernel, tm=tm), lidx, (x, ln_g, mod, mod, w),
        grid=(N_TOK // tm, n_out // tn),
        in_specs=[pl.BlockSpec((tm, D), lambda i, j, l: (i, 0)),
                  pl.BlockSpec((None, 1, D), lambda i, j, l: (l[0], 0, 0)),
                  pl.BlockSpec((None, 8, D), lambda i, j, l: (l[0], 0, 0)),
                  pl.BlockSpec((None, 8, D), lambda i, j, l: (l[0], 0, 1)),
                  pl.BlockSpec((None, D, tn), lambda i, j, l: (l[0], 0, j))],
        out_specs=[pl.BlockSpec((tm, tn), lambda i, j, l: (i, j)),
                   pl.BlockSpec((tm, D), lambda i, j, l: (i, 0))],
        out_shape=[jax.ShapeDtypeStruct((N_TOK, n_out), F32), jax.ShapeDtypeStruct((N_TOK, D), BF16)],
        sem=("parallel", "arbitrary"), name="in_proj")


def _in_proj_t_kernel(l_ref, h_ref, w_ref, o_ref):
    o_ref[...] = _dot_nt(w_ref[...], h_ref[...])


def _in_proj_t(lidx, h, w):
    tm, tn = 1024, 896
    n_out = w.shape[1]
    return _call(
        _in_proj_t_kernel, lidx, (h, w),
        grid=(N_TOK // tm, n_out // tn),
        in_specs=[pl.BlockSpec((tm, D), lambda i, j, l: (i, 0)),
                  pl.BlockSpec((None, tn, D), lambda i, j, l: (l[0], j, 0))],
        out_specs=pl.BlockSpec((tn, tm), lambda i, j, l: (j, i)),
        out_shape=jax.ShapeDtypeStruct((n_out, N_TOK), F32),
        sem=("parallel", "parallel"), name="in_proj_t")


def _kv_out_kernel(l_ref, k_ref, v_ref, ka_ref, va_ref, ko_ref, vo_ref):
    ko_ref[...] = k_ref[...]
    vo_ref[...] = v_ref[...]


def _kv_out(lidx, proj, k_acc, v_acc):
    cur = pl.BlockSpec((L_CTX, W), lambda b, l: (b, 1))
    cur_v = pl.BlockSpec((L_CTX, W), lambda b, l: (b, 2))
    dst = pl.BlockSpec((None, None, L_CTX, W), lambda b, l: (b, l[0], 0, 0))
    shape = jax.ShapeDtypeStruct((B_CTX, DEPTH, L_CTX, W), F32)
    return _call(
        _kv_out_kernel, lidx, (proj, proj, k_acc, v_acc),
        grid=(B_CTX,), in_specs=[cur, cur_v, _ANY, _ANY],
        out_specs=[dst, dst], out_shape=[shape, shape],
        sem=("parallel",), name="kv_out", aliases={3: 0, 4: 1})


ATT_CH = 8


def _attn_ctx_kernel(l_ref, q_ref, k_ref, v_ref, o_ref, s_ref):
    scale = DH ** -0.5
    for hh in range(ATT_CH):
        sl = slice(hh * DH, (hh + 1) * DH)
        q = (q_ref[:, sl] * scale).astype(BF16)
        s_ref[hh] = _dot_nt(q, k_ref[:, sl].astype(BF16))
    outs = []
    for hh in range(ATT_CH):
        sl = slice(hh * DH, (hh + 1) * DH)
        s = s_ref[hh]
        p = jnp.exp(s - jnp.max(s, -1, keepdims=True))
        den = jnp.sum(p, -1, keepdims=True)
        outs.append(_dot(p.astype(BF16), v_ref[:, sl].astype(BF16)) / den)
    o_ref[...] = jnp.concatenate(outs, -1).astype(BF16)


def _attn_ctx(lidx, proj):
    wb = ATT_CH * DH
    nq = W // wb
    return _call(
        _attn_ctx_kernel, lidx, (proj, proj, proj),
        grid=(B_CTX, nq),
        in_specs=[pl.BlockSpec((L_CTX, wb), lambda b, p, l: (b, p)),
                  pl.BlockSpec((L_CTX, wb), lambda b, p, l: (b, nq + p)),
                  pl.BlockSpec((L_CTX, wb), lambda b, p, l: (b, 2 * nq + p))],
        out_specs=pl.BlockSpec((L_CTX, wb), lambda b, p, l: (b, p)),
        out_shape=jax.ShapeDtypeStruct((N_TOK, W), BF16),
        scratch=[pltpu.VMEM((ATT_CH, L_CTX, L_CTX), F32)],
        sem=("parallel", "parallel"), name="attn_ctx")


ATT_RB = 8


def _attn_lat_kernel(l_ref, q_ref, k_ref, v_ref, kc_ref, vc_ref, bias_ref, alias_ref, o_ref,
                     kb_ref, vb_ref, kcb_ref, vcb_ref, s_ref):
    scale = DH ** -0.5
    for hh in range(2):
        sl = slice(hh * DH, (hh + 1) * DH)
        kb_ref[hh] = k_ref[:, sl].astype(BF16)
        vb_ref[hh] = v_ref[:, sl].astype(BF16)
        kcb_ref[hh] = kc_ref[:, sl].astype(BF16)
        vcb_ref[hh] = vc_ref[:, sl].astype(BF16)

    def window(r):
        r0 = jnp.clip(r - WIN_R // 2, 0, ROWS - WIN_R)
        return r - r0, pl.multiple_of(r * GRID_W, GRID_W), pl.multiple_of(r0 * GRID_W, GRID_W)

    def row_block(rr, carry):
        for i in range(ATT_RB):
            d, qrow, krow = window(rr * ATT_RB + i)
            for hh in range(2):
                sl = slice(hh * DH, (hh + 1) * DH)
                q = (q_ref[pl.ds(qrow, GRID_W), sl] * scale).astype(BF16)
                s_ref[2 * i + hh, :, :NL] = _dot_nt(q, kb_ref[hh, pl.ds(krow, NL), :]) + bias_ref[hh, d]
                s_ref[2 * i + hh, :, NL:] = _dot_nt(q, kcb_ref[hh])
        for i in range(ATT_RB):
            d, qrow, krow = window(rr * ATT_RB + i)
            outs = []
            for hh in range(2):
                s = s_ref[2 * i + hh]
                p = jnp.exp(s - jnp.max(s, -1, keepdims=True))
                den = jnp.sum(p, -1, keepdims=True)
                pb = p.astype(BF16)
                o = _dot(pb[:, :NL], vb_ref[hh, pl.ds(krow, NL), :]) + _dot(pb[:, NL:], vcb_ref[hh])
                outs.append(o / den)
            o_ref[pl.ds(qrow, GRID_W), :] = jnp.concatenate(outs, -1).astype(BF16)
        return carry

    lax.fori_loop(0, ROWS // ATT_RB, row_block, 0)


def _attn_lat(lidx, proj, cache_k, cache_v, bias_tab, o_ctx):
    nq = W // LANES
    rb = N_CTX // L_LAT
    return _call(
        _attn_lat_kernel, lidx, (proj, proj, proj, cache_k, cache_v, bias_tab, o_ctx),
        grid=(B_LAT, nq),
        in_specs=[pl.BlockSpec((L_LAT, LANES), lambda b, p, l: (rb + b, p)),
                  pl.BlockSpec((L_LAT, LANES), lambda b, p, l: (rb + b, nq + p)),
                  pl.BlockSpec((L_LAT, LANES), lambda b, p, l: (rb + b, 2 * nq + p)),
                  pl.BlockSpec((None, None, L_CTX, LANES), lambda b, p, l: (b, l[0], 0, p)),
                  pl.BlockSpec((None, None, L_CTX, LANES), lambda b, p, l: (b, l[0], 0, p)),
                  pl.BlockSpec((None, 2, WIN_R, GRID_W, NL), lambda b, p, l: (l[0], p, 0, 0, 0)),
                  _ANY],
        out_specs=pl.BlockSpec((L_LAT, LANES), lambda b, p, l: (rb + b, p)),
        out_shape=jax.ShapeDtypeStruct((N_TOK, W), BF16),
        scratch=[pltpu.VMEM((2, L_LAT, DH), BF16), pltpu.VMEM((2, L_LAT, DH), BF16),
                 pltpu.VMEM((2, L_CTX, DH), BF16), pltpu.VMEM((2, L_CTX, DH), BF16),
                 pltpu.VMEM((2 * ATT_RB, GRID_W, NL + L_CTX), F32)],
        sem=("parallel", "parallel"), name="attn_lat", aliases={7: 0})


def _bias_table(rpb):
    cq = np.arange(GRID_W)
    c0 = np.clip(cq - WIN_C // 2, 0, GRID_W - WIN_C)
    ck = np.arange(GRID_W)
    col_ok = (ck[None, :] >= c0[:, None]) & (ck[None, :] < c0[:, None] + WIN_C)
    dc = np.clip(ck[None, :] - cq[:, None], -(WIN_C - 1), WIN_C - 1) + (WIN_C - 1)
    onehot = (dc.reshape(-1)[None, :] == np.arange(2 * WIN_C - 1)[:, None]).astype(np.float32)
    cols = jnp.einsum('lhrc,cx->lhrx', rpb, jnp.asarray(onehot), precision=lax.Precision.HIGHEST)
    cols = cols.reshape(DEPTH, H, 2 * WIN_R - 1, GRID_W, GRID_W)
    tab = jnp.stack([cols[:, :, WIN_R - 1 - d:2 * WIN_R - 1 - d] for d in range(WIN_R)], 2)
    tab = jnp.transpose(tab, (0, 1, 2, 4, 3, 5))
    tab = jnp.where(col_ok[None, None, None, :, None, :], tab, NEG_INF)
    return tab.reshape(DEPTH, H, WIN_R, GRID_W, NL).astype(F32)


def _conv3(x, w_ref, b_ref):
    n = x.shape[0]
    row = lax.broadcasted_iota(jnp.int32, x.shape, 0)
    prev = jnp.where(row == 0, 0.0, pltpu.roll(x, 1, 0))
    nxt = jnp.where(row == n - 1, 0.0, pltpu.roll(x, n - 1, 0))
    return prev * w_ref[0:1, :] + x * w_ref[1:2, :] + nxt * w_ref[2:3, :] + b_ref[...]


def _conv3_t(x, p):
    n = x.shape[1]
    lane = lax.broadcasted_iota(jnp.int32, x.shape, 1)
    prev = jnp.where(lane == 0, 0.0, pltpu.roll(x, 1, 1))
    nxt = jnp.where(lane == n - 1, 0.0, pltpu.roll(x, n - 1, 1))
    return prev * p[:, 0:1] + x * p[:, 1:2] + nxt * p[:, 2:3] + p[:, 3:4]


def _wkv_prep_kernel(l_ref, r_ref, k_ref, v_ref, lora_ref, cp_ref, rp_ref, w2_ref, a2_ref, g2_ref,
                     ro_ref, ko_ref, vo_ref, d0_ref, d1_ref, a0o_ref, a1o_ref, go_ref):
    ro_ref[...] = _conv3_t(r_ref[...], cp_ref[0])
    ko_ref[...] = _conv3_t(k_ref[...], cp_ref[1])
    vo_ref[...] = _conv3_t(v_ref[...], cp_ref[2])
    rp = rp_ref[...]
    for e, (d_ref, ao_ref) in enumerate(((d0_ref, a0o_ref), (d1_ref, a1o_ref))):
        lw = jnp.tanh(lora_ref[e * LORA:(e + 1) * LORA, :]).astype(BF16)
        la = lora_ref[(2 + e) * LORA:(3 + e) * LORA, :].astype(BF16)
        w_log = rp[:, e:e + 1] + _dot(w2_ref[e].astype(BF16), lw)
        d_ref[...] = jnp.exp(-math.exp(-0.5) * jax.nn.sigmoid(w_log))
        ao_ref[...] = jax.nn.sigmoid(rp[:, 2 + e:3 + e] + _dot(a2_ref[e].astype(BF16), la))
    lg = jax.nn.sigmoid(lora_ref[4 * LORA:4 * LORA + LORA_G, :]).astype(BF16)
    go_ref[...] = _dot(g2_ref[...].astype(BF16), lg)


def _wkv_prep(lidx, proj_t, p, *, nb, seq, rb):
    ct = 512 if seq <= 256 else 128
    nc = W // ct

    def row(c):
        return pl.BlockSpec((ct, seq), lambda b, j, l: (c * nc + j, rb + b))

    out = pl.BlockSpec((None, ct, seq), lambda b, j, l: (b, j, 0))
    return _call(
        _wkv_prep_kernel, lidx,
        (proj_t, proj_t, proj_t, proj_t, p['wkv_conv'], p['wkv_rowp'], p['wkv_w2t'], p['wkv_a2t'],
         p['wkv_g2t']),
        grid=(nb, nc),
        in_specs=[row(0), row(1), row(2),
                  pl.BlockSpec((N_LORA, seq), lambda b, j, l: (3 * W // N_LORA, rb + b)),
                  pl.BlockSpec((None, 3, ct, 4), lambda b, j, l: (l[0], 0, j, 0)),
                  pl.BlockSpec((None, ct, 4), lambda b, j, l: (l[0], j, 0)),
                  pl.BlockSpec((None, 2, ct, LORA), lambda b, j, l: (l[0], 0, j, 0)),
                  pl.BlockSpec((None, 2, ct, LORA), lambda b, j, l: (l[0], 0, j, 0)),
                  pl.BlockSpec((None, ct, LORA_G), lambda b, j, l: (l[0], j, 0))],
        out_specs=[out] * 8,
        out_shape=[jax.ShapeDtypeStruct((nb, W, seq), F32)] * 8,
        sem=("parallel", "parallel"), name=f"wkv_prep_{seq}")


def _to_chains_kernel(l_ref, *refs, shared):
    o_ref = refs[-1]
    xs = [x for x in refs[:-1] for _ in range(2)] if shared else refs[:-1]
    for k0 in range(0, DH, 8):
        tiles = []
        for k in range(k0, k0 + 8):
            m = jnp.concatenate([x[k * H:(k + 1) * H, :] for x in xs], 0)
            tiles.append(m.T)
        o_ref[:, k0:k0 + 8, :] = pltpu.einshape("ktc->tkc", jnp.stack(tiles, 0))


def _to_chains(lidx, x0, x1, *, nb, seq):
    def src(b):
        return pl.BlockSpec((None, W, TCH), lambda g, t, l: (GB * g + b, 0, t))

    shared = x1 is None
    if shared:
        args = (x0,) * GB
        in_specs = [src(b) for b in range(GB)]
    else:
        args = tuple(x0 if i % 2 == 0 else x1 for i in range(2 * GB))
        in_specs = [src(i // 2) for i in range(2 * GB)]
    return _call(
        functools.partial(_to_chains_kernel, shared=shared), lidx, args,
        grid=(nb // GB, seq // TCH),
        in_specs=in_specs,
        out_specs=pl.BlockSpec((TCH, DH, LANES), lambda g, t, l: (t, 0, g)),
        out_shape=jax.ShapeDtypeStruct((seq, DH, nb * 2 * H), F32),
        sem=("parallel", "parallel"), name=f"to_chains_{seq}")


def _wkv_kernel(l_ref, rf_ref, kf_ref, vf_ref, wf_ref, af_ref, rb_ref, kb_ref, vb_ref, wb_ref, ab_ref,
                kkp_ref, kap_ref, s0_ref, yf_ref, yb_ref, s_ref, st_ref, *, tb):
    @pl.when(pl.program_id(1) == 0)
    def _():
        s_ref[...] = s0_ref[...]

    kkp = kkp_ref[...]
    kap = kap_ref[...]
    lane = lax.broadcasted_iota(jnp.int32, (DH, LANES), 1)
    bwd = (lane // H) % 2 == 1
    kc = 32
    zero = jnp.zeros((DH, LANES), F32)

    def step(t, carry):
        tr = tb - 1 - t
        pick = lambda f_ref, b_ref: jnp.where(bwd, b_ref[tr], f_ref[t])
        kt = pick(kf_ref, kb_ref)
        at = pick(af_ref, ab_ref)
        vt = pick(vf_ref, vb_ref)
        kk = kt * kkp
        kk = kk * lax.rsqrt(jnp.sum(kk * kk, 0, keepdims=True) + 1e-12)
        st_ref[0] = kk
        st_ref[1] = kk * at
        st_ref[2] = kt * (1.0 + (at - 1.0) * kap)
        st_ref[3] = pick(wf_ref, wb_ref)
        st_ref[4] = pick(rf_ref, rb_ref)

        def sa_body(c, acc):
            a0, a1 = acc
            for j in range(kc):
                k = c * kc + j
                term = s_ref[k] * st_ref[0, pl.ds(k, 1), :]
                if j % 2 == 0:
                    a0 = a0 + term
                else:
                    a1 = a1 + term
            return a0, a1

        a0, a1 = lax.fori_loop(0, DH // kc, sa_body, (zero, zero))
        sa = -(a0 + a1)

        def up_body(c, acc):
            y0, y1 = acc
            for j in range(kc):
                k = c * kc + j
                sk = ((s_ref[k] * st_ref[3, pl.ds(k, 1), :] + sa * st_ref[1, pl.ds(k, 1), :])
                      + vt * st_ref[2, pl.ds(k, 1), :])
                s_ref[k] = sk
                term = sk * st_ref[4, pl.ds(k, 1), :]
                if j % 2 == 0:
                    y0 = y0 + term
                else:
                    y1 = y1 + term
            return y0, y1

        y0, y1 = lax.fori_loop(0, DH // kc, up_body, (zero, zero))
        y = y0 + y1
        yf_ref[t] = y
        yb_ref[tr] = y
        return carry

    lax.fori_loop(0, tb, step, 0)


def _wkv(lidx, r, k, v, w, a, kkp, kap, s0, *, seq):
    tb = 32
    chains = r.shape[-1]
    nt = seq // tb
    f_spec = pl.BlockSpec((tb, DH, LANES), lambda g, t, l: (t, 0, g))
    b_spec = pl.BlockSpec((tb, DH, LANES), lambda g, t, l: (nt - 1 - t, 0, g))
    par_spec = pl.BlockSpec((None, DH, LANES), lambda g, t, l: (l[0], 0, g))
    st_spec = pl.BlockSpec((DH, DH, LANES), lambda g, t, l: (0, 0, g))
    y_shape = jax.ShapeDtypeStruct((seq, DH, chains), F32)
    return _call(
        functools.partial(_wkv_kernel, tb=tb), lidx, (r, k, v, w, a, r, k, v, w, a, kkp, kap, s0),
        grid=(chains // LANES, nt),
        in_specs=[f_spec] * 5 + [b_spec] * 5 + [par_spec, par_spec, st_spec],
        out_specs=[f_spec, b_spec, st_spec],
        out_shape=[y_shape, y_shape, jax.ShapeDtypeStruct((DH, DH, chains), F32)],
        scratch=[pltpu.VMEM((5, DH, LANES), F32)],
        sem=("parallel", "arbitrary"), name=f"wkv_{seq}")


def _from_chains_kernel(l_ref, yf_ref, yb_ref, o_ref):
    for v0 in range(0, DH, 8):
        f3 = pltpu.einshape("tvc->vtc", yf_ref[:, v0:v0 + 8, :])
        b3 = pltpu.einshape("tvc->vtc", yb_ref[:, v0:v0 + 8, :])
        for i in range(8):
            v = v0 + i
            tf = f3[i].T
            tb = b3[i].T
            for b in range(GB):
                lo = b * 2 * H
                o_ref[b, v * H:(v + 1) * H, :] = tf[lo:lo + H] + tb[lo + H:lo + 2 * H]


def _from_chains(lidx, yf, yb, *, nb, seq):
    y_spec = pl.BlockSpec((TCH, DH, LANES), lambda g, t, l: (t, 0, g))
    return _call(
        _from_chains_kernel, lidx, (yf, yb),
        grid=(nb // GB, seq // TCH),
        in_specs=[y_spec, y_spec],
        out_specs=pl.BlockSpec((GB, W, TCH), lambda g, t, l: (g, 0, t)),
        out_shape=jax.ShapeDtypeStruct((nb, W, seq), F32),
        sem=("parallel", "parallel"), name=f"from_chains_{seq}")


def _wkv_post_kernel(l_ref, y_ref, r_ref, k_ref, v_ref, g_ref, gate_ref, par_ref, w_ref, *rest):
    o_ref = rest[-1]
    t = y_ref.shape[-1]
    par = par_ref[...]
    y = y_ref[...].reshape(DH, H, t)
    mu = jnp.mean(y, 0, keepdims=True)
    yc = y - mu
    var = jnp.mean(yc * yc, 0, keepdims=True)
    yn = (yc * lax.rsqrt(var + GN_EPS)).reshape(W, t) * par[:, 1:2] + par[:, 2:3]
    rk = (r_ref[...] * k_ref[...] * par[:, 0:1]).reshape(DH, H, t)
    bonus = jnp.broadcast_to(jnp.sum(rk, 0, keepdims=True), (DH, H, t)).reshape(W, t) * v_ref[...]
    o = ((yn + bonus) * g_ref[...]).astype(BF16)
    o_ref[...] = jax.nn.sigmoid(gate_ref[...]) * _dot_tn(o, w_ref[...])


def _wkv_post(lidx, ysum, r, k, v, g, proj, p, prev, *, nb, seq, rb):
    tt = 256
    nt = seq // tt
    t_spec = pl.BlockSpec((None, W, tt), lambda b, t, l: (b, 0, t))
    tok = lambda b, t: rb + b * nt + t
    args = (ysum, r, k, v, g, proj, p['wkv_post'], p['w_pr'])
    in_specs = [t_spec] * 5 + [
        pl.BlockSpec((tt, D), lambda b, t, l: (tok(b, t), (GL_OFF + D) // D)),
        pl.BlockSpec((None, W, 4), lambda b, t, l: (l[0], 0, 0)),
        pl.BlockSpec((None, W, D), lambda b, t, l: (l[0], 0, 0))]
    aliases = None
    if prev is not None:
        args = args + (prev,)
        in_specs = in_specs + [_ANY]
        aliases = {len(args): 0}
    return _call(
        _wkv_post_kernel, lidx, args,
        grid=(nb, nt), in_specs=in_specs,
        out_specs=pl.BlockSpec((tt, D), lambda b, t, l: (tok(b, t), 0)),
        out_shape=jax.ShapeDtypeStruct((N_TOK, D), F32),
        sem=("parallel", "parallel"), name=f"wkv_post_{seq}", aliases=aliases)


def _head_param(p, reps):
    t = jnp.transpose(p.reshape(DEPTH, H, DH), (0, 2, 1))
    return jnp.tile(t, (1, 1, reps))


def _rwkv_branch(lidx, proj, proj_t, p, s0_chain, prev, *, nb, seq, rb_t, rb_tok):
    r, k, v, d0, d1, a0, a1, g = _wkv_prep(lidx, proj_t, p, nb=nb, seq=seq, rb=rb_t)
    tc = functools.partial(_to_chains, lidx, nb=nb, seq=seq)
    chains = nb * 2 * H
    yf, yb, s_fin = _wkv(lidx, tc(r, None), tc(k, None), tc(v, None), tc(d0, d1), tc(a0, a1),
                         p['kk_chain'][:, :, :chains], p['ka_chain'][:, :, :chains], s0_chain, seq=seq)
    ysum = _from_chains(lidx, yf, yb, nb=nb, seq=seq)
    m_r = _wkv_post(lidx, ysum, r, k, v, g, proj, p, prev, nb=nb, seq=seq, rb=rb_tok)
    return m_r, s_fin


def _pos_features(seq):
    t = np.linspace(0.0, 1.0, seq, dtype=np.float32)[:, None]
    w = 2.0 * np.pi * np.arange(seq, dtype=np.float32)[:, None] / seq
    f = np.linspace(1e-4, POS_BANDS - 1, POS_BANDS, dtype=np.float32)[None, :]
    z = np.concatenate([t, np.cos(f * w), -np.sin(f * w)], -1).astype(np.float32)
    zp = np.zeros((seq, FH_PAD), np.float32)
    zp[:, :z.shape[1]] = z
    dist = (np.abs(np.arange(seq) - seq // 2).astype(np.float32) / seq)[:, None]
    deltas = np.abs(np.linspace(math.log(1e-2) / 1.5, math.log(1e-2) / 0.3, W,
                                dtype=np.float32))[None, :]
    return zp, dist, deltas


def _filter_kernel(zp_ref, dist_ref, del_ref, f1_ref, b1_ref, f2_ref, b2_ref, fr_ref, f3_ref, o_ref,
                   t_ref):
    hi = lax.Precision.HIGHEST

    @pl.when(pl.program_id(1) == 0)
    def _():
        fr = fr_ref[...]
        t = jnp.sin(fr * (jnp.dot(zp_ref[...], f1_ref[...], precision=hi,
                                  preferred_element_type=F32) + b1_ref[...]))
        t_ref[...] = jnp.sin(fr * (jnp.dot(t, f2_ref[...], precision=hi,
                                           preferred_element_type=F32) + b2_ref[...]))

    filt = jnp.dot(t_ref[...], f3_ref[...], precision=hi, preferred_element_type=F32)
    filt = filt * jnp.exp(-dist_ref[...] * del_ref[...])
    o_ref[...] = filt / (jnp.sum(jnp.abs(filt), 0, keepdims=True) + 1e-6)


def _hyena_filter(p, seq):
    ct = 256
    zp, dist, deltas = _pos_features(seq)
    full = lambda shape: pl.BlockSpec((None,) + shape, lambda l, j: (l,) + (0,) * len(shape))
    return pl.pallas_call(
        _filter_kernel,
        grid=(DEPTH, W // ct),
        in_specs=[pl.BlockSpec((seq, FH_PAD), lambda l, j: (0, 0)),
                  pl.BlockSpec((seq, 1), lambda l, j: (0, 0)),
                  pl.BlockSpec((1, ct), lambda l, j: (0, j)),
                  full((FH_PAD, FH_PAD)), full((1, FH_PAD)), full((FH_PAD, FH_PAD)),
                  full((1, FH_PAD)), full((1, FH_PAD)),
                  pl.BlockSpec((None, FH_PAD, ct), lambda l, j: (l, 0, j))],
        out_specs=pl.BlockSpec((None, seq, ct), lambda l, j: (l, 0, j)),
        out_shape=jax.ShapeDtypeStruct((DEPTH, seq, W), F32),
        scratch_shapes=[pltpu.VMEM((seq, FH_PAD), F32)],
        compiler_params=_params(("parallel", "arbitrary")), name=f"hyena_filter_{seq}",
    )(jnp.asarray(zp), jnp.asarray(dist), jnp.asarray(deltas),
      p['hy_f1p'], p['hy_fb1p'], p['hy_f2p'], p['hy_fb2p'], p['hy_freqp'], p['hy_f3p'])


def _freq_block(seq):
    return min(seq, 256)


def _dft_mats(seq):
    n = 2 * seq
    fq = _freq_block(seq)
    k = jnp.arange(seq, dtype=jnp.int32)
    t = jnp.arange(seq, dtype=jnp.int32)
    ph = ((2 * k[:, None] + 1) * t[None, :]) % (2 * n)
    ang = ph.astype(F32) * np.float32(np.pi / n)
    fwd = jnp.stack([jnp.cos(ang).reshape(seq // fq, fq, seq),
                     jnp.sin(ang).reshape(seq // fq, fq, seq)], 1).reshape(2 * seq, seq)
    m = t + seq // 2
    ph2 = ((2 * k[None, :] + 1) * m[:, None]) % (2 * n)
    ang2 = ph2.astype(F32) * np.float32(np.pi / n)
    inv = jnp.stack([jnp.cos(ang2).reshape(seq, seq // fq, fq),
                     jnp.sin(ang2).reshape(seq, seq // fq, fq)], 2).reshape(seq, 2 * seq)
    inv = inv * np.float32(2.0 / n)
    return fwd.astype(BF16), inv.astype(BF16)


def _spectrum_kernel(a_ref, b_ref, o_ref):
    o_ref[...] = _dot(a_ref[...], b_ref[...].astype(BF16))


def _spectrum(fwd, filt, seq):
    tmm, tn = 512, 512
    return pl.pallas_call(
        _spectrum_kernel,
        grid=(DEPTH, 2 * seq // tmm, W // tn),
        in_specs=[pl.BlockSpec((tmm, seq), lambda l, i, j: (i, 0)),
                  pl.BlockSpec((None, seq, tn), lambda l, i, j: (l, 0, j))],
        out_specs=pl.BlockSpec((None, tmm, tn), lambda l, i, j: (l, i, j)),
        out_shape=jax.ShapeDtypeStruct((DEPTH, 2 * seq, W), F32),
        compiler_params=_params(("parallel", "parallel", "parallel")), name=f"hyena_spectrum_{seq}",
    )(fwd, filt)


def _hyena_kernel(l_ref, x0_ref, x1_ref, vv_ref, cw0_ref, cw1_ref, cw2_ref, cb0_ref, cb1_ref, cb2_ref,
                  fwd_ref, inv_ref, h_ref, d_ref, *rest):
    o_ref, z_ref, zb_ref, acc_ref = rest[-4:]
    fb = pl.program_id(2)
    fq = h_ref.shape[0] // 2

    @pl.when(fb == 0)
    def _():
        z = _conv3(vv_ref[...], cw2_ref, cb2_ref) * _conv3(x1_ref[...], cw1_ref, cb1_ref)
        z_ref[...] = z
        zb_ref[...] = z.astype(BF16)
        acc_ref[...] = jnp.zeros_like(acc_ref)

    zf = _dot(fwd_ref[...], zb_ref[...])
    zc, zs = zf[:fq], zf[fq:]
    hc, hs = h_ref[:fq, :], h_ref[fq:, :]
    pr = zc * hc - zs * hs
    pq = zc * hs + zs * hc
    acc_ref[...] += _dot(inv_ref[...], jnp.concatenate([pr, pq], 0).astype(BF16))

    @pl.when(fb == pl.num_programs(2) - 1)
    def _():
        y = acc_ref[...] + z_ref[...] * d_ref[...]
        o_ref[...] = (_conv3(x0_ref[...], cw0_ref, cb0_ref) * y).astype(BF16)


def _hyena(lidx, proj, p, mats, spec, prev, *, nb, seq, rb):
    ct = 512
    fq = _freq_block(seq)
    nc = W // ct
    off = HY_OFF // ct
    fwd_hi, inv_hi = mats

    def col(c):
        return pl.BlockSpec((seq, ct), lambda b, j, f, l: (rb + b, off + c * nc + j))

    def cw(c):
        return pl.BlockSpec((None, 3, ct), lambda b, j, f, l: (l[0], 0, c * nc + j))

    def cb(c):
        return pl.BlockSpec((None, 1, ct), lambda b, j, f, l: (l[0], 0, c * nc + j))

    args = (proj, proj, proj, p['hy_conv_w'], p['hy_conv_w'], p['hy_conv_w'],
            p['hy_conv_b'], p['hy_conv_b'], p['hy_conv_b'], fwd_hi, inv_hi, spec, p['hy_d'])
    in_specs = [col(0), col(1), col(2), cw(0), cw(1), cw(2), cb(0), cb(1), cb(2),
                pl.BlockSpec((2 * fq, seq), lambda b, j, f, l: (f, 0)),
                pl.BlockSpec((seq, 2 * fq), lambda b, j, f, l: (0, f)),
                pl.BlockSpec((None, 2 * fq, ct), lambda b, j, f, l: (l[0], f, j)),
                pl.BlockSpec((None, 1, ct), lambda b, j, f, l: (l[0], 0, j))]
    aliases = None
    if prev is not None:
        args = args + (prev,)
        in_specs = in_specs + [_ANY]
        aliases = {len(args): 0}
    return _call(
        _hyena_kernel, lidx, args,
        grid=(nb, nc, seq // fq), in_specs=in_specs,
        out_specs=pl.BlockSpec((seq, ct), lambda b, j, f, l: (rb + b, j)),
        out_shape=jax.ShapeDtypeStruct((N_TOK, W), BF16),
        scratch=[pltpu.VMEM((seq, ct), F32), pltpu.VMEM((seq, ct), BF16), pltpu.VMEM((seq, ct), F32)],
        sem=("parallel", "parallel", "arbitrary"), name=f"hyena_{seq}", aliases=aliases)


def _merge_kernel(l_ref, oa_ref, oc_ref, mr_ref, ga_ref, gc_ref, wa_ref, wc_ref, o_ref):
    m = (jax.nn.sigmoid(ga_ref[...]) * _dot(oa_ref[...], wa_ref[...]) + mr_ref[...]
         + jax.nn.sigmoid(gc_ref[...]) * _dot(oc_ref[...], wc_ref[...]))
    o_ref[...] = m.astype(BF16)


def _merge(lidx, o_a, o_c, m_r, proj, p):
    tm, tn = 1024, 512
    goff = GL_OFF // tn
    nd = D // tn
    row = lambda: pl.BlockSpec((tm, W), lambda i, j, l: (i, 0))
    gate = lambda c: pl.BlockSpec((tm, tn), lambda i, j, l: (i, goff + c * nd + j))
    wsp = lambda: pl.BlockSpec((None, W, tn), lambda i, j, l: (l[0], 0, j))
    return _call(
        _merge_kernel, lidx, (o_a, o_c, m_r, proj, proj, p['w_pa'], p['w_pc']),
        grid=(N_TOK // tm, nd),
        in_specs=[row(), row(), pl.BlockSpec((tm, tn), lambda i, j, l: (i, j)),
                  gate(0), gate(2), wsp(), wsp()],
        out_specs=pl.BlockSpec((tm, tn), lambda i, j, l: (i, j)),
        out_shape=jax.ShapeDtypeStruct((N_TOK, D), BF16),
        sem=("parallel", "parallel"), name="merge")


def _out_proj_kernel(l_ref, m_ref, w_ref, x_ref, ga_ref, g_ref, sh_ref, sc_ref, o_ref, h_ref, *, tm):
    row = _mod_row(pl.program_id(0), tm)
    xn = x_ref[...] + ga_ref[pl.ds(row, 1), :] * _dot(m_ref[...], w_ref[...])
    o_ref[...] = xn
    h_ref[...] = _modnorm(xn, g_ref[...], sc_ref[pl.ds(row, 1), :], sh_ref[pl.ds(row, 1), :]).astype(BF16)


def _out_proj(lidx, merged, x, mod, w_out, ln_g):
    tm = 512
    rows = pl.BlockSpec((tm, D), lambda i, l: (i, 0))
    chunk = lambda c: pl.BlockSpec((None, 8, D), lambda i, l: (l[0], 0, c))
    return _call(
        functools.partial(_out_proj_kernel, tm=tm), lidx, (merged, w_out, x, mod, ln_g, mod, mod),
        grid=(N_TOK // tm,),
        in_specs=[rows, pl.BlockSpec((None, D, D), lambda i, l: (l[0], 0, 0)), rows, chunk(2),
                  pl.BlockSpec((None, 1, D), lambda i, l: (l[0], 0, 0)), chunk(3), chunk(4)],
        out_specs=[rows, rows],
        out_shape=[jax.ShapeDtypeStruct((N_TOK, D), F32), jax.ShapeDtypeStruct((N_TOK, D), BF16)],
        sem=("parallel",), name="out_proj")


def _ffn_kernel(l_ref, x_ref, h_ref, ga_ref, w1_ref, b1_ref, w2_ref, b2_ref, o_ref, acc_ref, *, tm):
    j = pl.program_id(1)
    row = _mod_row(pl.program_id(0), tm)

    @pl.when(j == 0)
    def _():
        acc_ref[...] = jnp.zeros_like(acc_ref)

    a = _dot(h_ref[...], w1_ref[...]) + b1_ref[...]
    a = jnp.square(jnp.maximum(a, 0.0))
    acc_ref[...] += _dot(a.astype(BF16), w2_ref[...])

    @pl.when(j == pl.num_programs(1) - 1)
    def _():
        o_ref[...] = x_ref[...] + ga_ref[pl.ds(row, 1), :] * (acc_ref[...] + b2_ref[...])


def _ffn(lidx, x, h, mod, p):
    tm, tf = 512, 512
    return _call(
        functools.partial(_ffn_kernel, tm=tm), lidx,
        (x, h, mod, p['w_ff1'], p['b_ff1'], p['w_ff2'], p['b_ff2']),
        grid=(N_TOK // tm, D_FF // tf),
        in_specs=[pl.BlockSpec((tm, D), lambda i, j, l: (i, 0)),
                  pl.BlockSpec((tm, D), lambda i, j, l: (i, 0)),
                  pl.BlockSpec((None, 8, D), lambda i, j, l: (l[0], 0, 5)),
                  pl.BlockSpec((None, D, tf), lambda i, j, l: (l[0], 0, j)),
                  pl.BlockSpec((None, 1, tf), lambda i, j, l: (l[0], 0, j)),
                  pl.BlockSpec((None, tf, D), lambda i, j, l: (l[0], j, 0)),
                  pl.BlockSpec((None, 1, D), lambda i, j, l: (l[0], 0, 0))],
        out_specs=pl.BlockSpec((tm, D), lambda i, j, l: (i, 0)),
        out_shape=jax.ShapeDtypeStruct((N_TOK, D), F32),
        scratch=[pltpu.VMEM((tm, D), F32)],
        sem=("parallel", "arbitrary"), name="ffn")


def _final_norm_kernel(x_ref, g_ref, o_ref):
    x = x_ref[...]
    o_ref[...] = x * lax.rsqrt(jnp.mean(x * x, -1, keepdims=True) + NORM_EPS) * g_ref[...]


def _final_norm(x, g, row0, nrows):
    tm = 1024
    rb = row0 // tm
    return pl.pallas_call(
        _final_norm_kernel,
        grid=(nrows // tm,),
        in_specs=[pl.BlockSpec((tm, D), lambda i: (rb + i, 0)), pl.BlockSpec((1, D), lambda i: (0, 0))],
        out_specs=pl.BlockSpec((tm, D), lambda i: (i, 0)),
        out_shape=jax.ShapeDtypeStruct((nrows, D), F32),
        compiler_params=_params(("parallel",)), name="final_norm",
    )(x, g.reshape(1, D))


def kernel(x_prompt, x_sample, cache_k, cache_v, state_wkv, c, c_ctx, ln1_g, ln2_g, w_mod, b_mod, w_in, rpb, wkv_conv_w, wkv_conv_b, wkv_w0, wkv_w1, wkv_w2, wkv_a0, wkv_a1, wkv_a2, wkv_g1, wkv_g2, wkv_k_k, wkv_k_a, wkv_r_k, wkv_gn_g, wkv_gn_b, hy_conv_w, hy_conv_b, hy_f1, hy_fb1, hy_f2, hy_fb2, hy_freq, hy_f3, hy_d, w_pa, w_pr, w_pc, w_out, w_ff1, b_ff1, w_ff2, b_ff2, final_g):
    x = jnp.concatenate([x_prompt.reshape(N_CTX, D), x_sample.reshape(N_LAT, D)], 0)
    cvec = jnp.zeros((8, D), F32).at[0].set(c_ctx).at[1:1 + B_LAT].set(c)
    mod = _modulation(cvec, w_mod, b_mod)

    perm = (np.arange(H)[None, :] * DH + np.arange(DH)[:, None]).reshape(-1)
    pad_c = lambda a, n: jnp.pad(a, [(0, 0)] * (a.ndim - 1) + [(0, n - a.shape[-1])])
    pad_r = lambda a, n: jnp.pad(a, [(0, 0)] * (a.ndim - 2) + [(0, n - a.shape[-2]), (0, 0)])
    conv = jnp.concatenate([wkv_conv_w, wkv_conv_b[:, None, :]], 1)
    conv = jnp.transpose(conv.reshape(DEPTH, 4, 3, W)[..., perm], (0, 2, 3, 1))
    rowp = jnp.stack([wkv_w0[:, 0], wkv_w0[:, 1], wkv_a0[:, 0], wkv_a0[:, 1]], -1)[:, perm]
    post = jnp.stack([wkv_r_k.reshape(DEPTH, W), wkv_gn_g, wkv_gn_b, jnp.zeros((DEPTH, W), F32)],
                     -1)[:, perm]
    p = {
        'wkv_conv': conv, 'wkv_rowp': rowp, 'wkv_post': post,
        'wkv_w2t': jnp.swapaxes(wkv_w2[..., perm], -1, -2),
        'wkv_a2t': jnp.swapaxes(wkv_a2[..., perm], -1, -2),
        'wkv_g2t': jnp.swapaxes(wkv_g2[..., perm], -1, -2),
        'kk_chain': _head_param(wkv_k_k, 2 * B_CTX), 'ka_chain': _head_param(wkv_k_a, 2 * B_CTX),
        'hy_conv_w': hy_conv_w, 'hy_conv_b': hy_conv_b.reshape(DEPTH, 1, 3 * W),
        'hy_d': hy_d.reshape(DEPTH, 1, W),
        'hy_f1p': pad_c(pad_r(hy_f1, FH_PAD), FH_PAD), 'hy_fb1p': pad_c(hy_fb1, FH_PAD).reshape(DEPTH, 1, FH_PAD),
        'hy_f2p': pad_c(pad_r(hy_f2, FH_PAD), FH_PAD), 'hy_fb2p': pad_c(hy_fb2, FH_PAD).reshape(DEPTH, 1, FH_PAD),
        'hy_freqp': pad_c(hy_freq, FH_PAD).reshape(DEPTH, 1, FH_PAD), 'hy_f3p': pad_r(hy_f3, FH_PAD),
        'w_pa': w_pa.astype(BF16), 'w_pr': w_pr[:, perm].astype(BF16), 'w_pc': w_pc.astype(BF16),
        'w_ff1': w_ff1.astype(BF16), 'b_ff1': b_ff1.reshape(DEPTH, 1, D_FF),
        'w_ff2': w_ff2.astype(BF16), 'b_ff2': b_ff2.reshape(DEPTH, 1, D),
    }
    w_a = jnp.concatenate([w_in[..., :3 * W], w_in[..., 6 * W:]], -1).astype(BF16)
    rkv = w_in[..., 3 * W:6 * W].reshape(DEPTH, D, 3, W)[..., perm].reshape(DEPTH, D, 3 * W)
    w_t = jnp.concatenate(
        [rkv, wkv_w1[:, 0], wkv_w1[:, 1], wkv_a1[:, 0], wkv_a1[:, 1], wkv_g1,
         jnp.zeros((DEPTH, D, N_LORA - 4 * LORA - LORA_G), F32)], -1)
    w_t = jnp.swapaxes(w_t, 1, 2).astype(BF16)
    w_out_b = w_out.astype(BF16)
    ln1 = ln1_g.reshape(DEPTH, 1, D)
    ln2 = ln2_g.reshape(DEPTH, 1, D)
    bias_tab = _bias_table(rpb)
    ck = cache_k.reshape(B_LAT, DEPTH, L_CTX, W)
    cv = cache_v.reshape(B_LAT, DEPTH, L_CTX, W)
    s0_lat = jnp.transpose(state_wkv, (1, 5, 4, 0, 2, 3)).reshape(DEPTH, DH, DH, B_LAT * 2 * H)
    s0_ctx = jnp.zeros((DH, DH, B_CTX * 2 * H), F32)

    hy = {}
    for seq in (L_CTX, L_LAT):
        mats = _dft_mats(seq)
        filt = _hyena_filter(p, seq)
        hy[seq] = (mats, _spectrum(mats[0], filt, seq))

    def layer(carry, l):
        x, k_acc, v_acc = carry
        lidx = jnp.reshape(l, (1,)).astype(jnp.int32)
        proj, h = _in_proj(lidx, x, ln1, mod, w_a)
        k_acc, v_acc = _kv_out(lidx, proj, k_acc, v_acc)
        proj_t = _in_proj_t(lidx, h, w_t)
        oa = _attn_lat(lidx, proj, ck, cv, bias_tab, _attn_ctx(lidx, proj))
        m_r, s_ctx = _rwkv_branch(lidx, proj, proj_t, p, s0_ctx, None, nb=B_CTX, seq=L_CTX,
                                  rb_t=0, rb_tok=0)
        s0 = lax.dynamic_index_in_dim(s0_lat, l, 0, keepdims=False)
        m_r, _ = _rwkv_branch(lidx, proj, proj_t, p, s0, m_r, nb=B_LAT, seq=L_LAT,
                              rb_t=N_CTX // L_LAT, rb_tok=N_CTX // 256)
        oc = _hyena(lidx, proj, p, *hy[L_CTX], None, nb=B_CTX, seq=L_CTX, rb=0)
        oc = _hyena(lidx, proj, p, *hy[L_LAT], oc, nb=B_LAT, seq=L_LAT, rb=N_CTX // L_LAT)
        merged = _merge(lidx, oa, oc, m_r, proj, p)
        x, h2 = _out_proj(lidx, merged, x, mod, w_out_b, ln2)
        x = _ffn(lidx, x, h2, mod, p)
        return (x, k_acc, v_acc), s_ctx

    kv0 = jnp.zeros((B_CTX, DEPTH, L_CTX, W), F32)
    (x, k_acc, v_acc), ss = lax.scan(layer, (x, kv0, kv0), jnp.arange(DEPTH, dtype=jnp.int32))
    y_prompt = _final_norm(x, final_g, 0, N_CTX).reshape(B_CTX, L_CTX, D)
    y_sample = _final_norm(x, final_g, N_CTX, N_LAT).reshape(B_LAT, L_LAT, D)
    new_k = k_acc.reshape(B_CTX, DEPTH, L_CTX, H, DH)
    new_v = v_acc.reshape(B_CTX, DEPTH, L_CTX, H, DH)
    new_s = jnp.transpose(ss.reshape(DEPTH, DH, DH, B_CTX, 2, H), (3, 0, 4, 5, 2, 1))
    return (y_prompt, y_sample, new_k, new_v, new_s)
```

```python
import functools
import math

import numpy as np
import jax
import jax.numpy as jnp
from jax import lax
from jax.experimental import pallas as pl
from jax.experimental.pallas import tpu as pltpu

F32 = jnp.float32
BF16 = jnp.bfloat16

D = 2048
DEPTH = 4
B_CTX, L_CTX = 32, 256
B_LAT, L_LAT = 4, 2048
N_CTX = B_CTX * L_CTX
N_LAT = B_LAT * L_LAT
N_TOK = N_CTX + N_LAT
H = 16
DH = 64
W = H * DH
GRID_W = 64
ROWS = L_LAT // GRID_W
WIN_R, WIN_C = 8, 16
NL = WIN_R * GRID_W
LORA = 64
LORA_G = 128
N_LORA = 512
N_A = 6 * W + 3 * D
HY_OFF = 3 * W
GL_OFF = 6 * W
N_T = 3 * W + N_LORA
D_FF = 4 * D
N_MOD = 6
NORM_EPS = 1e-6
GN_EPS = 64e-5
NEG_INF = -1e30
POS_BANDS = 16
FH_PAD = 128
LANES = 128
GB = LANES // (2 * H)
TCH = 128
WKV_RUN = 16
VMEM_LIMIT = 56 * 1024 * 1024


def _params(sem):
    return pltpu.CompilerParams(dimension_semantics=sem, vmem_limit_bytes=VMEM_LIMIT)


def _call(kernel, lidx, args, *, grid, in_specs, out_specs, out_shape, scratch=(), sem, name,
          aliases=None):
    gs = pltpu.PrefetchScalarGridSpec(num_scalar_prefetch=1, grid=grid, in_specs=in_specs,
                                      out_specs=out_specs, scratch_shapes=list(scratch))
    return pl.pallas_call(kernel, grid_spec=gs, out_shape=out_shape,
                          compiler_params=_params(sem), name=name,
                          input_output_aliases=aliases or {})(lidx, *args)


_ANY = pl.BlockSpec(memory_space=pl.ANY)


def _mod_row(i, tm):
    start = i * tm
    return jnp.where(start < N_CTX, 0, 1 + (start - N_CTX) // L_LAT)


def _modnorm(x, g, sc, sh):
    y = x * lax.rsqrt(jnp.mean(x * x, -1, keepdims=True) + NORM_EPS)
    return (y * g) * (1.0 + sc) + sh


def _dot(a, b):
    return jnp.dot(a, b, preferred_element_type=F32)


def _dot_nt(a, b):
    return lax.dot_general(a, b, (((1,), (1,)), ((), ())), preferred_element_type=F32)


def _dot_tn(a, b):
    return lax.dot_general(a, b, (((0,), (0,)), ((), ())), preferred_element_type=F32)


def _mod_kernel(c_ref, w_ref, b_ref, o_ref):
    c = c_ref[...]
    s = c * jax.nn.sigmoid(c)
    o_ref[...] = _dot(s.astype(BF16), w_ref[...].astype(BF16)) + b_ref[...]


def _modulation(cvec8, w_mod, b_mod):
    tn = 1024
    return pl.pallas_call(
        _mod_kernel,
        grid=(DEPTH, N_MOD * D // tn),
        in_specs=[pl.BlockSpec((8, D), lambda l, j: (0, 0)),
                  pl.BlockSpec((None, D, tn), lambda l, j: (l, 0, j)),
                  pl.BlockSpec((None, 1, tn), lambda l, j: (l, 0, j))],
        out_specs=pl.BlockSpec((None, 8, tn), lambda l, j: (l, 0, j)),
        out_shape=jax.ShapeDtypeStruct((DEPTH, 8, N_MOD * D), F32),
        compiler_params=_params(("parallel", "parallel")), name="modulation",
    )(cvec8, w_mod, b_mod.reshape(DEPTH, 1, N_MOD * D))


def _in_proj_kernel(l_ref, x_ref, g_ref, sh_ref, sc_ref, w_ref, o_ref, h_ref, *, tm):
    @pl.when(pl.program_id(1) == 0)
    def _():
        row = _mod_row(pl.program_id(0), tm)
        h = _modnorm(x_ref[...], g_ref[...], sc_ref[pl.ds(row, 1), :], sh_ref[pl.ds(row, 1), :])
        h_ref[...] = h.astype(BF16)

    o_ref[...] = _dot(h_ref[...], w_ref[...])


def _in_proj(lidx, x, ln_g, mod, w):
    tm, tn = 1024, 512
    n_out = w.shape[2]
    return _call(
        functools.partial(_in_proj_kernel, tm=tm), lidx, (x, ln_g, mod, mod, w),
        grid=(N_TOK // tm, n_out // tn),
        in_specs=[pl.BlockSpec((tm, D), lambda i, j, l: (i, 0)),
                  pl.BlockSpec((None, 1, D), lambda i, j, l: (l[0], 0, 0)),
                  pl.BlockSpec((None, 8, D), lambda i, j, l: (l[0], 0, 0)),
                  pl.BlockSpec((None, 8, D), lambda i, j, l: (l[0], 0, 1)),
                  pl.BlockSpec((None, D, tn), lambda i, j, l: (l[0], 0, j))],
        out_specs=[pl.BlockSpec((tm, tn), lambda i, j, l: (i, j)),
                   pl.BlockSpec((tm, D), lambda i, j, l: (i, 0))],
        out_shape=[jax.ShapeDtypeStruct((N_TOK, n_out), F32), jax.ShapeDtypeStruct((N_TOK, D), BF16)],
        sem=("parallel", "arbitrary"), name="in_proj")


def _in_proj_t_kernel(l_ref, h_ref, w_ref, o_ref):
    o_ref[...] = _dot_nt(w_ref[...], h_ref[...])


def _in_proj_t(lidx, h, w):
    tm, tn = 1024, 896
    n_out = w.shape[1]
    return _call(
        _in_proj_t_kernel, lidx, (h, w),
        grid=(N_TOK // tm, n_out // tn),
        in_specs=[pl.BlockSpec((tm, D), lambda i, j, l: (i, 0)),
                  pl.BlockSpec((None, tn, D), lambda i, j, l: (l[0], j, 0))],
        out_specs=pl.BlockSpec((tn, tm), lambda i, j, l: (j, i)),
        out_shape=jax.ShapeDtypeStruct((n_out, N_TOK), F32),
        sem=("parallel", "parallel"), name="in_proj_t")


def _kv_out_kernel(l_ref, k_ref, v_ref, ka_ref, va_ref, ko_ref, vo_ref):
    ko_ref[...] = k_ref[...]
    vo_ref[...] = v_ref[...]


def _kv_out(lidx, proj, k_acc, v_acc):
    cur = pl.BlockSpec((L_CTX, W), lambda b, l: (b, 1))
    cur_v = pl.BlockSpec((L_CTX, W), lambda b, l: (b, 2))
    dst = pl.BlockSpec((None, None, L_CTX, W), lambda b, l: (b, l[0], 0, 0))
    shape = jax.ShapeDtypeStruct((B_CTX, DEPTH, L_CTX, W), F32)
    return _call(
        _kv_out_kernel, lidx, (proj, proj, k_acc, v_acc),
        grid=(B_CTX,), in_specs=[cur, cur_v, _ANY, _ANY],
        out_specs=[dst, dst], out_shape=[shape, shape],
        sem=("parallel",), name="kv_out", aliases={3: 0, 4: 1})


ATT_CH = 8


def _attn_ctx_kernel(l_ref, q_ref, k_ref, v_ref, o_ref, s_ref):
    scale = DH ** -0.5
    for hh in range(ATT_CH):
        sl = slice(hh * DH, (hh + 1) * DH)
        q = (q_ref[:, sl] * scale).astype(BF16)
        s_ref[hh] = _dot_nt(q, k_ref[:, sl].astype(BF16))
    outs = []
    for hh in range(ATT_CH):
        sl = slice(hh * DH, (hh + 1) * DH)
        s = s_ref[hh]
        p = jnp.exp(s - jnp.max(s, -1, keepdims=True))
        den = jnp.sum(p, -1, keepdims=True)
        outs.append(_dot(p.astype(BF16), v_ref[:, sl].astype(BF16)) / den)
    o_ref[...] = jnp.concatenate(outs, -1).astype(BF16)


def _attn_ctx(lidx, proj):
    wb = ATT_CH * DH
    nq = W // wb
    return _call(
        _attn_ctx_kernel, lidx, (proj, proj, proj),
        grid=(B_CTX, nq),
        in_specs=[pl.BlockSpec((L_CTX, wb), lambda b, p, l: (b, p)),
                  pl.BlockSpec((L_CTX, wb), lambda b, p, l: (b, nq + p)),
                  pl.BlockSpec((L_CTX, wb), lambda b, p, l: (b, 2 * nq + p))],
        out_specs=pl.BlockSpec((L_CTX, wb), lambda b, p, l: (b, p)),
        out_shape=jax.ShapeDtypeStruct((N_TOK, W), BF16),
        scratch=[pltpu.VMEM((ATT_CH, L_CTX, L_CTX), F32)],
        sem=("parallel", "parallel"), name="attn_ctx")


ATT_RB = 8


def _attn_lat_kernel(l_ref, q_ref, k_ref, v_ref, kc_ref, vc_ref, bias_ref, alias_ref, o_ref,
                     kb_ref, vb_ref, kcb_ref, vcb_ref, s_ref):
    scale = DH ** -0.5
    for hh in range(2):
        sl = slice(hh * DH, (hh + 1) * DH)
        kb_ref[hh] = k_ref[:, sl].astype(BF16)
        vb_ref[hh] = v_ref[:, sl].astype(BF16)
        kcb_ref[hh] = kc_ref[:, sl].astype(BF16)
        vcb_ref[hh] = vc_ref[:, sl].astype(BF16)

    def window(r):
        r0 = jnp.clip(r - WIN_R // 2, 0, ROWS - WIN_R)
        return r - r0, pl.multiple_of(r * GRID_W, GRID_W), pl.multiple_of(r0 * GRID_W, GRID_W)

    def row_block(rr, carry):
        for i in range(ATT_RB):
            d, qrow, krow = window(rr * ATT_RB + i)
            for hh in range(2):
                sl = slice(hh * DH, (hh + 1) * DH)
                q = (q_ref[pl.ds(qrow, GRID_W), sl] * scale).astype(BF16)
                s_ref[2 * i + hh, :, :NL] = _dot_nt(q, kb_ref[hh, pl.ds(krow, NL), :]) + bias_ref[hh, d]
                s_ref[2 * i + hh, :, NL:] = _dot_nt(q, kcb_ref[hh])
        for i in range(ATT_RB):
            d, qrow, krow = window(rr * ATT_RB + i)
            outs = []
            for hh in range(2):
                s = s_ref[2 * i + hh]
                p = jnp.exp(s - jnp.max(s, -1, keepdims=True))
                den = jnp.sum(p, -1, keepdims=True)
                pb = p.astype(BF16)
                o = _dot(pb[:, :NL], vb_ref[hh, pl.ds(krow, NL), :]) + _dot(pb[:, NL:], vcb_ref[hh])
                outs.append(o / den)
            o_ref[pl.ds(qrow, GRID_W), :] = jnp.concatenate(outs, -1).astype(BF16)
        return carry

    lax.fori_loop(0, ROWS // ATT_RB, row_block, 0)


def _attn_lat(lidx, proj, cache_k, cache_v, bias_tab, o_ctx):
    nq = W // LANES
    rb = N_CTX // L_LAT
    return _call(
        _attn_lat_kernel, lidx, (proj, proj, proj, cache_k, cache_v, bias_tab, o_ctx),
        grid=(B_LAT, nq),
        in_specs=[pl.BlockSpec((L_LAT, LANES), lambda b, p, l: (rb + b, p)),
                  pl.BlockSpec((L_LAT, LANES), lambda b, p, l: (rb + b, nq + p)),
                  pl.BlockSpec((L_LAT, LANES), lambda b, p, l: (rb + b, 2 * nq + p)),
                  pl.BlockSpec((None, None, L_CTX, LANES), lambda b, p, l: (b, l[0], 0, p)),
                  pl.BlockSpec((None, None, L_CTX, LANES), lambda b, p, l: (b, l[0], 0, p)),
                  pl.BlockSpec((None, 2, WIN_R, GRID_W, NL), lambda b, p, l: (l[0], p, 0, 0, 0)),
                  _ANY],
        out_specs=pl.BlockSpec((L_LAT, LANES), lambda b, p, l: (rb + b, p)),
        out_shape=jax.ShapeDtypeStruct((N_TOK, W), BF16),
        scratch=[pltpu.VMEM((2, L_LAT, DH), BF16), pltpu.VMEM((2, L_LAT, DH), BF16),
                 pltpu.VMEM((2, L_CTX, DH), BF16), pltpu.VMEM((2, L_CTX, DH), BF16),
                 pltpu.VMEM((2 * ATT_RB, GRID_W, NL + L_CTX), F32)],
        sem=("parallel", "parallel"), name="attn_lat", aliases={7: 0})


def _bias_table(rpb):
    cq = np.arange(GRID_W)
    c0 = np.clip(cq - WIN_C // 2, 0, GRID_W - WIN_C)
    ck = np.arange(GRID_W)
    col_ok = (ck[None, :] >= c0[:, None]) & (ck[None, :] < c0[:, None] + WIN_C)
    dc = np.clip(ck[None, :] - cq[:, None], -(WIN_C - 1), WIN_C - 1) + (WIN_C - 1)
    onehot = (dc.reshape(-1)[None, :] == np.arange(2 * WIN_C - 1)[:, None]).astype(np.float32)
    cols = jnp.einsum('lhrc,cx->lhrx', rpb, jnp.asarray(onehot), precision=lax.Precision.HIGHEST)
    cols = cols.reshape(DEPTH, H, 2 * WIN_R - 1, GRID_W, GRID_W)
    tab = jnp.stack([cols[:, :, WIN_R - 1 - d:2 * WIN_R - 1 - d] for d in range(WIN_R)], 2)
    tab = jnp.transpose(tab, (0, 1, 2, 4, 3, 5))
    tab = jnp.where(col_ok[None, None, None, :, None, :], tab, NEG_INF)
    return tab.reshape(DEPTH, H, WIN_R, GRID_W, NL).astype(F32)


def _conv3(x, w_ref, b_ref):
    n = x.shape[0]
    row = lax.broadcasted_iota(jnp.int32, x.shape, 0)
    prev = jnp.where(row == 0, 0.0, pltpu.roll(x, 1, 0))
    nxt = jnp.where(row == n - 1, 0.0, pltpu.roll(x, n - 1, 0))
    return prev * w_ref[0:1, :] + x * w_ref[1:2, :] + nxt * w_ref[2:3, :] + b_ref[...]


def _conv3_t(x, p):
    n = x.shape[1]
    lane = lax.broadcasted_iota(jnp.int32, x.shape, 1)
    prev = jnp.where(lane == 0, 0.0, pltpu.roll(x, 1, 1))
    nxt = jnp.where(lane == n - 1, 0.0, pltpu.roll(x, n - 1, 1))
    return prev * p[:, 0:1] + x * p[:, 1:2] + nxt * p[:, 2:3] + p[:, 3:4]


def _wkv_prep_kernel(l_ref, r_ref, k_ref, v_ref, lora_ref, cp_ref, rp_ref, w2_ref, a2_ref, g2_ref,
                     ro_ref, ko_ref, vo_ref, d0_ref, d1_ref, a0o_ref, a1o_ref, go_ref):
    ro_ref[...] = _conv3_t(r_ref[...], cp_ref[0])
    ko_ref[...] = _conv3_t(k_ref[...], cp_ref[1])
    vo_ref[...] = _conv3_t(v_ref[...], cp_ref[2])
    rp = rp_ref[...]
    for e, (d_ref, ao_ref) in enumerate(((d0_ref, a0o_ref), (d1_ref, a1o_ref))):
        lw = jnp.tanh(lora_ref[e * LORA:(e + 1) * LORA, :]).astype(BF16)
        la = lora_ref[(2 + e) * LORA:(3 + e) * LORA, :].astype(BF16)
        w_log = rp[:, e:e + 1] + _dot(w2_ref[e].astype(BF16), lw)
        d_ref[...] = jnp.exp(-math.exp(-0.5) * jax.nn.sigmoid(w_log))
        ao_ref[...] = jax.nn.sigmoid(rp[:, 2 + e:3 + e] + _dot(a2_ref[e].astype(BF16), la))
    lg = jax.nn.sigmoid(lora_ref[4 * LORA:4 * LORA + LORA_G, :]).astype(BF16)
    go_ref[...] = _dot(g2_ref[...].astype(BF16), lg)


def _wkv_prep(lidx, proj_t, p, *, nb, seq, rb):
    ct = 512 if seq <= 256 else 128
    nc = W // ct

    def row(c):
        return pl.BlockSpec((ct, seq), lambda b, j, l: (c * nc + j, rb + b))

    out = pl.BlockSpec((None, ct, seq), lambda b, j, l: (b, j, 0))
    return _call(
        _wkv_prep_kernel, lidx,
        (proj_t, proj_t, proj_t, proj_t, p['wkv_conv'], p['wkv_rowp'], p['wkv_w2t'], p['wkv_a2t'],
         p['wkv_g2t']),
        grid=(nb, nc),
        in_specs=[row(0), row(1), row(2),
                  pl.BlockSpec((N_LORA, seq), lambda b, j, l: (3 * W // N_LORA, rb + b)),
                  pl.BlockSpec((None, 3, ct, 4), lambda b, j, l: (l[0], 0, j, 0)),
                  pl.BlockSpec((None, ct, 4), lambda b, j, l: (l[0], j, 0)),
                  pl.BlockSpec((None, 2, ct, LORA), lambda b, j, l: (l[0], 0, j, 0)),
                  pl.BlockSpec((None, 2, ct, LORA), lambda b, j, l: (l[0], 0, j, 0)),
                  pl.BlockSpec((None, ct, LORA_G), lambda b, j, l: (l[0], j, 0))],
        out_specs=[out] * 8,
        out_shape=[jax.ShapeDtypeStruct((nb, W, seq), F32)] * 8,
        sem=("parallel", "parallel"), name=f"wkv_prep_{seq}")


def _to_chains_kernel(l_ref, *refs, shared):
    o_ref = refs[-1]
    xs = [x for x in refs[:-1] for _ in range(2)] if shared else refs[:-1]
    for k0 in range(0, DH, 8):
        tiles = []
        for k in range(k0, k0 + 8):
            m = jnp.concatenate([x[k * H:(k + 1) * H, :] for x in xs], 0)
            tiles.append(m.T)
        o_ref[:, k0:k0 + 8, :] = pltpu.einshape("ktc->tkc", jnp.stack(tiles, 0))


def _to_chains(lidx, x0, x1, *, nb, seq):
    def src(b):
        return pl.BlockSpec((None, W, TCH), lambda g, t, l: (GB * g + b, 0, t))

    shared = x1 is None
    if shared:
        args = (x0,) * GB
        in_specs = [src(b) for b in range(GB)]
    else:
        args = tuple(x0 if i % 2 == 0 else x1 for i in range(2 * GB))
        in_specs = [src(i // 2) for i in range(2 * GB)]
    return _call(
        functools.partial(_to_chains_kernel, shared=shared), lidx, args,
        grid=(nb // GB, seq // TCH),
        in_specs=in_specs,
        out_specs=pl.BlockSpec((TCH, DH, LANES), lambda g, t, l: (t, 0, g)),
        out_shape=jax.ShapeDtypeStruct((seq, DH, nb * 2 * H), F32),
        sem=("parallel", "parallel"), name=f"to_chains_{seq}")


def _wkv_kernel(l_ref, rf_ref, kf_ref, vf_ref, wf_ref, af_ref, rb_ref, kb_ref, vb_ref, wb_ref, ab_ref,
                kkp_ref, kap_ref, s0_ref, yf_ref, yb_ref, s_ref, st_ref, p_ref, *, tb):
    @pl.when(pl.program_id(1) == 0)
    def _():
        s_ref[...] = s0_ref[...]

    kkp = kkp_ref[...]
    kap = kap_ref[...]
    lane = lax.broadcasted_iota(jnp.int32, (DH, LANES), 1)
    bwd = (lane // H) % 2 == 1
    kc = 32
    zero = jnp.zeros((DH, LANES), F32)

    def step(t, carry):
        tr = tb - 1 - t
        pick = lambda f_ref, b_ref: jnp.where(bwd, b_ref[tr], f_ref[t])
        kt = pick(kf_ref, kb_ref)
        at = pick(af_ref, ab_ref)
        vt = pick(vf_ref, vb_ref)
        kk = kt * kkp
        kk = kk * lax.rsqrt(jnp.sum(kk * kk, 0, keepdims=True) + 1e-12)
        p_prev = p_ref[...]
        p_new = p_prev * pick(wf_ref, wb_ref)
        p_inv = 1.0 / p_new
        p_ref[...] = p_new
        st_ref[0] = kk * p_prev
        st_ref[1] = (kk * at) * p_inv
        st_ref[2] = (kt * (1.0 + (at - 1.0) * kap)) * p_inv
        st_ref[3] = pick(rf_ref, rb_ref) * p_new

        def sa_body(c, acc):
            a0, a1 = acc
            for j in range(kc):
                k = c * kc + j
                term = s_ref[k] * st_ref[0, pl.ds(k, 1), :]
                if j % 2 == 0:
                    a0 = a0 + term
                else:
                    a1 = a1 + term
            return a0, a1

        a0, a1 = lax.fori_loop(0, DH // kc, sa_body, (zero, zero))
        sa = -(a0 + a1)

        def up_body(c, acc):
            y0, y1 = acc
            for j in range(kc):
                k = c * kc + j
                sk = s_ref[k] + (sa * st_ref[1, pl.ds(k, 1), :] + vt * st_ref[2, pl.ds(k, 1), :])
                s_ref[k] = sk
                term = sk * st_ref[3, pl.ds(k, 1), :]
                if j % 2 == 0:
                    y0 = y0 + term
                else:
                    y1 = y1 + term
            return y0, y1

        y0, y1 = lax.fori_loop(0, DH // kc, up_body, (zero, zero))
        y = y0 + y1
        yf_ref[t] = y
        yb_ref[tr] = y
        return carry

    def rescale(c, carry):
        for j in range(kc):
            k = c * kc + j
            s_ref[k] = s_ref[k] * p_ref[pl.ds(k, 1), :]
        return carry

    def run(i, carry):
        p_ref[...] = jnp.ones((DH, LANES), F32)
        lax.fori_loop(0, WKV_RUN, lambda j, c: step(i * WKV_RUN + j, c), 0)
        lax.fori_loop(0, DH // kc, rescale, 0)
        return carry

    lax.fori_loop(0, tb // WKV_RUN, run, 0)


def _wkv(lidx, r, k, v, w, a, kkp, kap, s0, *, seq):
    tb = 32
    chains = r.shape[-1]
    nt = seq // tb
    f_spec = pl.BlockSpec((tb, DH, LANES), lambda g, t, l: (t, 0, g))
    b_spec = pl.BlockSpec((tb, DH, LANES), lambda g, t, l: (nt - 1 - t, 0, g))
    par_spec = pl.BlockSpec((None, DH, LANES), lambda g, t, l: (l[0], 0, g))
    st_spec = pl.BlockSpec((DH, DH, LANES), lambda g, t, l: (0, 0, g))
    y_shape = jax.ShapeDtypeStruct((seq, DH, chains), F32)
    return _call(
        functools.partial(_wkv_kernel, tb=tb), lidx, (r, k, v, w, a, r, k, v, w, a, kkp, kap, s0),
        grid=(chains // LANES, nt),
        in_specs=[f_spec] * 5 + [b_spec] * 5 + [par_spec, par_spec, st_spec],
        out_specs=[f_spec, b_spec, st_spec],
        out_shape=[y_shape, y_shape, jax.ShapeDtypeStruct((DH, DH, chains), F32)],
        scratch=[pltpu.VMEM((4, DH, LANES), F32), pltpu.VMEM((DH, LANES), F32)],
        sem=("parallel", "arbitrary"), name=f"wkv_{seq}")


def _from_chains_kernel(l_ref, yf_ref, yb_ref, o_ref):
    for v0 in range(0, DH, 8):
        f3 = pltpu.einshape("tvc->vtc", yf_ref[:, v0:v0 + 8, :])
        b3 = pltpu.einshape("tvc->vtc", yb_ref[:, v0:v0 + 8, :])
        for i in range(8):
            v = v0 + i
            tf = f3[i].T
            tb = b3[i].T
            for b in range(GB):
                lo = b * 2 * H
                o_ref[b, v * H:(v + 1) * H, :] = tf[lo:lo + H] + tb[lo + H:lo + 2 * H]


def _from_chains(lidx, yf, yb, *, nb, seq):
    y_spec = pl.BlockSpec((TCH, DH, LANES), lambda g, t, l: (t, 0, g))
    return _call(
        _from_chains_kernel, lidx, (yf, yb),
        grid=(nb // GB, seq // TCH),
        in_specs=[y_spec, y_spec],
        out_specs=pl.BlockSpec((GB, W, TCH), lambda g, t, l: (g, 0, t)),
        out_shape=jax.ShapeDtypeStruct((nb, W, seq), F32),
        sem=("parallel", "parallel"), name=f"from_chains_{seq}")


def _wkv_post_kernel(l_ref, y_ref, r_ref, k_ref, v_ref, g_ref, gate_ref, par_ref, w_ref, *rest):
    o_ref = rest[-1]
    t = y_ref.shape[-1]
    par = par_ref[...]
    y = y_ref[...].reshape(DH, H, t)
    mu = jnp.mean(y, 0, keepdims=True)
    yc = y - mu
    var = jnp.mean(yc * yc, 0, keepdims=True)
    yn = (yc * lax.rsqrt(var + GN_EPS)).reshape(W, t) * par[:, 1:2] + par[:, 2:3]
    rk = (r_ref[...] * k_ref[...] * par[:, 0:1]).reshape(DH, H, t)
    bonus = jnp.broadcast_to(jnp.sum(rk, 0, keepdims=True), (DH, H, t)).reshape(W, t) * v_ref[...]
    o = ((yn + bonus) * g_ref[...]).astype(BF16)
    o_ref[...] = jax.nn.sigmoid(gate_ref[...]) * _dot_tn(o, w_ref[...])


def _wkv_post(lidx, ysum, r, k, v, g, proj, p, prev, *, nb, seq, rb):
    tt = 256
    nt = seq // tt
    t_spec = pl.BlockSpec((None, W, tt), lambda b, t, l: (b, 0, t))
    tok = lambda b, t: rb + b * nt + t
    args = (ysum, r, k, v, g, proj, p['wkv_post'], p['w_pr'])
    in_specs = [t_spec] * 5 + [
        pl.BlockSpec((tt, D), lambda b, t, l: (tok(b, t), (GL_OFF + D) // D)),
        pl.BlockSpec((None, W, 4), lambda b, t, l: (l[0], 0, 0)),
        pl.BlockSpec((None, W, D), lambda b, t, l: (l[0], 0, 0))]
    aliases = None
    if prev is not None:
        args = args + (prev,)
        in_specs = in_specs + [_ANY]
        aliases = {len(args): 0}
    return _call(
        _wkv_post_kernel, lidx, args,
        grid=(nb, nt), in_specs=in_specs,
        out_specs=pl.BlockSpec((tt, D), lambda b, t, l: (tok(b, t), 0)),
        out_shape=jax.ShapeDtypeStruct((N_TOK, D), F32),
        sem=("parallel", "parallel"), name=f"wkv_post_{seq}", aliases=aliases)


def _head_param(p, reps):
    t = jnp.transpose(p.reshape(DEPTH, H, DH), (0, 2, 1))
    return jnp.tile(t, (1, 1, reps))


def _rwkv_branch(lidx, proj, proj_t, p, s0_chain, prev, *, nb, seq, rb_t, rb_tok):
    r, k, v, d0, d1, a0, a1, g = _wkv_prep(lidx, proj_t, p, nb=nb, seq=seq, rb=rb_t)
    tc = functools.partial(_to_chains, lidx, nb=nb, seq=seq)
    chains = nb * 2 * H
    yf, yb, s_fin = _wkv(lidx, tc(r, None), tc(k, None), tc(v, None), tc(d0, d1), tc(a0, a1),
                         p['kk_chain'][:, :, :chains], p['ka_chain'][:, :, :chains], s0_chain, seq=seq)
    ysum = _from_chains(lidx, yf, yb, nb=nb, seq=seq)
    m_r = _wkv_post(lidx, ysum, r, k, v, g, proj, p, prev, nb=nb, seq=seq, rb=rb_tok)
    return m_r, s_fin


def _pos_features(seq):
    t = np.linspace(0.0, 1.0, seq, dtype=np.float32)[:, None]
    w = 2.0 * np.pi * np.arange(seq, dtype=np.float32)[:, None] / seq
    f = np.linspace(1e-4, POS_BANDS - 1, POS_BANDS, dtype=np.float32)[None, :]
    z = np.concatenate([t, np.cos(f * w), -np.sin(f * w)], -1).astype(np.float32)
    zp = np.zeros((seq, FH_PAD), np.float32)
    zp[:, :z.shape[1]] = z
    dist = (np.abs(np.arange(seq) - seq // 2).astype(np.float32) / seq)[:, None]
    deltas = np.abs(np.linspace(math.log(1e-2) / 1.5, math.log(1e-2) / 0.3, W,
                                dtype=np.float32))[None, :]
    return zp, dist, deltas


def _filter_kernel(zp_ref, dist_ref, del_ref, f1_ref, b1_ref, f2_ref, b2_ref, fr_ref, f3_ref, o_ref,
                   t_ref):
    hi = lax.Precision.HIGHEST

    @pl.when(pl.program_id(1) == 0)
    def _():
        fr = fr_ref[...]
        t = jnp.sin(fr * (jnp.dot(zp_ref[...], f1_ref[...], precision=hi,
                                  preferred_element_type=F32) + b1_ref[...]))
        t_ref[...] = jnp.sin(fr * (jnp.dot(t, f2_ref[...], precision=hi,
                                           preferred_element_type=F32) + b2_ref[...]))

    filt = jnp.dot(t_ref[...], f3_ref[...], precision=hi, preferred_element_type=F32)
    filt = filt * jnp.exp(-dist_ref[...] * del_ref[...])
    o_ref[...] = filt / (jnp.sum(jnp.abs(filt), 0, keepdims=True) + 1e-6)


def _hyena_filter(p, seq):
    ct = 256
    zp, dist, deltas = _pos_features(seq)
    full = lambda shape: pl.BlockSpec((None,) + shape, lambda l, j: (l,) + (0,) * len(shape))
    return pl.pallas_call(
        _filter_kernel,
        grid=(DEPTH, W // ct),
        in_specs=[pl.BlockSpec((seq, FH_PAD), lambda l, j: (0, 0)),
                  pl.BlockSpec((seq, 1), lambda l, j: (0, 0)),
                  pl.BlockSpec((1, ct), lambda l, j: (0, j)),
                  full((FH_PAD, FH_PAD)), full((1, FH_PAD)), full((FH_PAD, FH_PAD)),
                  full((1, FH_PAD)), full((1, FH_PAD)),
                  pl.BlockSpec((None, FH_PAD, ct), lambda l, j: (l, 0, j))],
        out_specs=pl.BlockSpec((None, seq, ct), lambda l, j: (l, 0, j)),
        out_shape=jax.ShapeDtypeStruct((DEPTH, seq, W), F32),
        scratch_shapes=[pltpu.VMEM((seq, FH_PAD), F32)],
        compiler_params=_params(("parallel", "arbitrary")), name=f"hyena_filter_{seq}",
    )(jnp.asarray(zp), jnp.asarray(dist), jnp.asarray(deltas),
      p['hy_f1p'], p['hy_fb1p'], p['hy_f2p'], p['hy_fb2p'], p['hy_freqp'], p['hy_f3p'])


def _freq_block(seq):
    return min(seq, 256)


def _dft_mats(seq):
    n = 2 * seq
    fq = _freq_block(seq)
    k = jnp.arange(seq, dtype=jnp.int32)
    t = jnp.arange(seq, dtype=jnp.int32)
    ph = ((2 * k[:, None] + 1) * t[None, :]) % (2 * n)
    ang = ph.astype(F32) * np.float32(np.pi / n)
    fwd = jnp.stack([jnp.cos(ang).reshape(seq // fq, fq, seq),
                     jnp.sin(ang).reshape(seq // fq, fq, seq)], 1).reshape(2 * seq, seq)
    m = t + seq // 2
    ph2 = ((2 * k[None, :] + 1) * m[:, None]) % (2 * n)
    ang2 = ph2.astype(F32) * np.float32(np.pi / n)
    inv = jnp.stack([jnp.cos(ang2).reshape(seq, seq // fq, fq),
                     jnp.sin(ang2).reshape(seq, seq // fq, fq)], 2).reshape(seq, 2 * seq)
    inv = inv * np.float32(2.0 / n)
    return fwd.astype(BF16), inv.astype(BF16)


def _spectrum_kernel(a_ref, b_ref, o_ref):
    o_ref[...] = _dot(a_ref[...], b_ref[...].astype(BF16))


def _spectrum(fwd, filt, seq):
    tmm, tn = 512, 512
    return pl.pallas_call(
        _spectrum_kernel,
        grid=(DEPTH, 2 * seq // tmm, W // tn),
        in_specs=[pl.BlockSpec((tmm, seq), lambda l, i, j: (i, 0)),
                  pl.BlockSpec((None, seq, tn), lambda l, i, j: (l, 0, j))],
        out_specs=pl.BlockSpec((None, tmm, tn), lambda l, i, j: (l, i, j)),
        out_shape=jax.ShapeDtypeStruct((DEPTH, 2 * seq, W), F32),
        compiler_params=_params(("parallel", "parallel", "parallel")), name=f"hyena_spectrum_{seq}",
    )(fwd, filt)


def _hyena_kernel(l_ref, x0_ref, x1_ref, vv_ref, cw0_ref, cw1_ref, cw2_ref, cb0_ref, cb1_ref, cb2_ref,
                  fwd_ref, inv_ref, h_ref, d_ref, *rest):
    o_ref, z_ref, zb_ref, acc_ref = rest[-4:]
    fb = pl.program_id(2)
    fq = h_ref.shape[0] // 2

    @pl.when(fb == 0)
    def _():
        z = _conv3(vv_ref[...], cw2_ref, cb2_ref) * _conv3(x1_ref[...], cw1_ref, cb1_ref)
        z_ref[...] = z
        zb_ref[...] = z.astype(BF16)
        acc_ref[...] = jnp.zeros_like(acc_ref)

    zf = _dot(fwd_ref[...], zb_ref[...])
    zc, zs = zf[:fq], zf[fq:]
    hc, hs = h_ref[:fq, :], h_ref[fq:, :]
    pr = zc * hc - zs * hs
    pq = zc * hs + zs * hc
    acc_ref[...] += _dot(inv_ref[...], jnp.concatenate([pr, pq], 0).astype(BF16))

    @pl.when(fb == pl.num_programs(2) - 1)
    def _():
        y = acc_ref[...] + z_ref[...] * d_ref[...]
        o_ref[...] = (_conv3(x0_ref[...], cw0_ref, cb0_ref) * y).astype(BF16)


def _hyena(lidx, proj, p, mats, spec, prev, *, nb, seq, rb):
    ct = 512
    fq = _freq_block(seq)
    nc = W // ct
    off = HY_OFF // ct
    fwd_hi, inv_hi = mats

    def col(c):
        return pl.BlockSpec((seq, ct), lambda b, j, f, l: (rb + b, off + c * nc + j))

    def cw(c):
        return pl.BlockSpec((None, 3, ct), lambda b, j, f, l: (l[0], 0, c * nc + j))

    def cb(c):
        return pl.BlockSpec((None, 1, ct), lambda b, j, f, l: (l[0], 0, c * nc + j))

    args = (proj, proj, proj, p['hy_conv_w'], p['hy_conv_w'], p['hy_conv_w'],
            p['hy_conv_b'], p['hy_conv_b'], p['hy_conv_b'], fwd_hi, inv_hi, spec, p['hy_d'])
    in_specs = [col(0), col(1), col(2), cw(0), cw(1), cw(2), cb(0), cb(1), cb(2),
                pl.BlockSpec((2 * fq, seq), lambda b, j, f, l: (f, 0)),
                pl.BlockSpec((seq, 2 * fq), lambda b, j, f, l: (0, f)),
                pl.BlockSpec((None, 2 * fq, ct), lambda b, j, f, l: (l[0], f, j)),
                pl.BlockSpec((None, 1, ct), lambda b, j, f, l: (l[0], 0, j))]
    aliases = None
    if prev is not None:
        args = args + (prev,)
        in_specs = in_specs + [_ANY]
        aliases = {len(args): 0}
    return _call(
        _hyena_kernel, lidx, args,
        grid=(nb, nc, seq // fq), in_specs=in_specs,
        out_specs=pl.BlockSpec((seq, ct), lambda b, j, f, l: (rb + b, j)),
        out_shape=jax.ShapeDtypeStruct((N_TOK, W), BF16),
        scratch=[pltpu.VMEM((seq, ct), F32), pltpu.VMEM((seq, ct), BF16), pltpu.VMEM((seq, ct), F32)],
        sem=("parallel", "parallel", "arbitrary"), name=f"hyena_{seq}", aliases=aliases)


def _merge_kernel(l_ref, oa_ref, oc_ref, mr_ref, ga_ref, gc_ref, wa_ref, wc_ref, o_ref):
    m = (jax.nn.sigmoid(ga_ref[...]) * _dot(oa_ref[...], wa_ref[...]) + mr_ref[...]
         + jax.nn.sigmoid(gc_ref[...]) * _dot(oc_ref[...], wc_ref[...]))
    o_ref[...] = m.astype(BF16)


def _merge(lidx, o_a, o_c, m_r, proj, p):
    tm, tn = 1024, 512
    goff = GL_OFF // tn
    nd = D // tn
    row = lambda: pl.BlockSpec((tm, W), lambda i, j, l: (i, 0))
    gate = lambda c: pl.BlockSpec((tm, tn), lambda i, j, l: (i, goff + c * nd + j))
    wsp = lambda: pl.BlockSpec((None, W, tn), lambda i, j, l: (l[0], 0, j))
    return _call(
        _merge_kernel, lidx, (o_a, o_c, m_r, proj, proj, p['w_pa'], p['w_pc']),
        grid=(N_TOK // tm, nd),
        in_specs=[row(), row(), pl.BlockSpec((tm, tn), lambda i, j, l: (i, j)),
                  gate(0), gate(2), wsp(), wsp()],
        out_specs=pl.BlockSpec((tm, tn), lambda i, j, l: (i, j)),
        out_shape=jax.ShapeDtypeStruct((N_TOK, D), BF16),
        sem=("parallel", "parallel"), name="merge")


def _out_proj_kernel(l_ref, m_ref, w_ref, x_ref, ga_ref, g_ref, sh_ref, sc_ref, o_ref, h_ref, *, tm):
    row = _mod_row(pl.program_id(0), tm)
    xn = x_ref[...] + ga_ref[pl.ds(row, 1), :] * _dot(m_ref[...], w_ref[...])
    o_ref[...] = xn
    h_ref[...] = _modnorm(xn, g_ref[...], sc_ref[pl.ds(row, 1), :], sh_ref[pl.ds(row, 1), :]).astype(BF16)


def _out_proj(lidx, merged, x, mod, w_out, ln_g):
    tm = 512
    rows = pl.BlockSpec((tm, D), lambda i, l: (i, 0))
    chunk = lambda c: pl.BlockSpec((None, 8, D), lambda i, l: (l[0], 0, c))
    return _call(
        functools.partial(_out_proj_kernel, tm=tm), lidx, (merged, w_out, x, mod, ln_g, mod, mod),
        grid=(N_TOK // tm,),
        in_specs=[rows, pl.BlockSpec((None, D, D), lambda i, l: (l[0], 0, 0)), rows, chunk(2),
                  pl.BlockSpec((None, 1, D), lambda i, l: (l[0], 0, 0)), chunk(3), chunk(4)],
        out_specs=[rows, rows],
        out_shape=[jax.ShapeDtypeStruct((N_TOK, D), F32), jax.ShapeDtypeStruct((N_TOK, D), BF16)],
        sem=("parallel",), name="out_proj")


def _ffn_kernel(l_ref, x_ref, h_ref, ga_ref, w1_ref, b1_ref, w2_ref, b2_ref, o_ref, acc_ref, *, tm):
    j = pl.program_id(1)
    row = _mod_row(pl.program_id(0), tm)

    @pl.when(j == 0)
    def _():
        acc_ref[...] = jnp.zeros_like(acc_ref)

    a = _dot(h_ref[...], w1_ref[...]) + b1_ref[...]
    a = jnp.square(jnp.maximum(a, 0.0))
    acc_ref[...] += _dot(a.astype(BF16), w2_ref[...])

    @pl.when(j == pl.num_programs(1) - 1)
    def _():
        o_ref[...] = x_ref[...] + ga_ref[pl.ds(row, 1), :] * (acc_ref[...] + b2_ref[...])


def _ffn(lidx, x, h, mod, p):
    tm, tf = 512, 512
    return _call(
        functools.partial(_ffn_kernel, tm=tm), lidx,
        (x, h, mod, p['w_ff1'], p['b_ff1'], p['w_ff2'], p['b_ff2']),
        grid=(N_TOK // tm, D_FF // tf),
        in_specs=[pl.BlockSpec((tm, D), lambda i, j, l: (i, 0)),
                  pl.BlockSpec((tm, D), lambda i, j, l: (i, 0)),
                  pl.BlockSpec((None, 8, D), lambda i, j, l: (l[0], 0, 5)),
                  pl.BlockSpec((None, D, tf), lambda i, j, l: (l[0], 0, j)),
                  pl.BlockSpec((None, 1, tf), lambda i, j, l: (l[0], 0, j)),
                  pl.BlockSpec((None, tf, D), lambda i, j, l: (l[0], j, 0)),
                  pl.BlockSpec((None, 1, D), lambda i, j, l: (l[0], 0, 0))],
        out_specs=pl.BlockSpec((tm, D), lambda i, j, l: (i, 0)),
        out_shape=jax.ShapeDtypeStruct((N_TOK, D), F32),
        scratch=[pltpu.VMEM((tm, D), F32)],
        sem=("parallel", "arbitrary"), name="ffn")


def _final_norm_kernel(x_ref, g_ref, o_ref):
    x = x_ref[...]
    o_ref[...] = x * lax.rsqrt(jnp.mean(x * x, -1, keepdims=True) + NORM_EPS) * g_ref[...]


def _final_norm(x, g, row0, nrows):
    tm = 1024
    rb = row0 // tm
    return pl.pallas_call(
        _final_norm_kernel,
        grid=(nrows // tm,),
        in_specs=[pl.BlockSpec((tm, D), lambda i: (rb + i, 0)), pl.BlockSpec((1, D), lambda i: (0, 0))],
        out_specs=pl.BlockSpec((tm, D), lambda i: (i, 0)),
        out_shape=jax.ShapeDtypeStruct((nrows, D), F32),
        compiler_params=_params(("parallel",)), name="final_norm",
    )(x, g.reshape(1, D))


def kernel(x_prompt, x_sample, cache_k, cache_v, state_wkv, c, c_ctx, ln1_g, ln2_g, w_mod, b_mod, w_in, rpb, wkv_conv_w, wkv_conv_b, wkv_w0, wkv_w1, wkv_w2, wkv_a0, wkv_a1, wkv_a2, wkv_g1, wkv_g2, wkv_k_k, wkv_k_a, wkv_r_k, wkv_gn_g, wkv_gn_b, hy_conv_w, hy_conv_b, hy_f1, hy_fb1, hy_f2, hy_fb2, hy_freq, hy_f3, hy_d, w_pa, w_pr, w_pc, w_out, w_ff1, b_ff1, w_ff2, b_ff2, final_g):
    x = jnp.concatenate([x_prompt.reshape(N_CTX, D), x_sample.reshape(N_LAT, D)], 0)
    cvec = jnp.zeros((8, D), F32).at[0].set(c_ctx).at[1:1 + B_LAT].set(c)
    mod = _modulation(cvec, w_mod, b_mod)

    perm = (np.arange(H)[None, :] * DH + np.arange(DH)[:, None]).reshape(-1)
    pad_c = lambda a, n: jnp.pad(a, [(0, 0)] * (a.ndim - 1) + [(0, n - a.shape[-1])])
    pad_r = lambda a, n: jnp.pad(a, [(0, 0)] * (a.ndim - 2) + [(0, n - a.shape[-2]), (0, 0)])
    conv = jnp.concatenate([wkv_conv_w, wkv_conv_b[:, None, :]], 1)
    conv = jnp.transpose(conv.reshape(DEPTH, 4, 3, W)[..., perm], (0, 2, 3, 1))
    rowp = jnp.stack([wkv_w0[:, 0], wkv_w0[:, 1], wkv_a0[:, 0], wkv_a0[:, 1]], -1)[:, perm]
    post = jnp.stack([wkv_r_k.reshape(DEPTH, W), wkv_gn_g, wkv_gn_b, jnp.zeros((DEPTH, W), F32)],
                     -1)[:, perm]
    p = {
        'wkv_conv': conv, 'wkv_rowp': rowp, 'wkv_post': post,
        'wkv_w2t': jnp.swapaxes(wkv_w2[..., perm], -1, -2),
        'wkv_a2t': jnp.swapaxes(wkv_a2[..., perm], -1, -2),
        'wkv_g2t': jnp.swapaxes(wkv_g2[..., perm], -1, -2),
        'kk_chain': _head_param(wkv_k_k, 2 * B_CTX), 'ka_chain': _head_param(wkv_k_a, 2 * B_CTX),
        'hy_conv_w': hy_conv_w, 'hy_conv_b': hy_conv_b.reshape(DEPTH, 1, 3 * W),
        'hy_d': hy_d.reshape(DEPTH, 1, W),
        'hy_f1p': pad_c(pad_r(hy_f1, FH_PAD), FH_PAD), 'hy_fb1p': pad_c(hy_fb1, FH_PAD).reshape(DEPTH, 1, FH_PAD),
        'hy_f2p': pad_c(pad_r(hy_f2, FH_PAD), FH_PAD), 'hy_fb2p': pad_c(hy_fb2, FH_PAD).reshape(DEPTH, 1, FH_PAD),
        'hy_freqp': pad_c(hy_freq, FH_PAD).reshape(DEPTH, 1, FH_PAD), 'hy_f3p': pad_r(hy_f3, FH_PAD),
        'w_pa': w_pa.astype(BF16), 'w_pr': w_pr[:, perm].astype(BF16), 'w_pc': w_pc.astype(BF16),
        'w_ff1': w_ff1.astype(BF16), 'b_ff1': b_ff1.reshape(DEPTH, 1, D_FF),
        'w_ff2': w_ff2.astype(BF16), 'b_ff2': b_ff2.reshape(DEPTH, 1, D),
    }
    w_a = jnp.concatenate([w_in[..., :3 * W], w_in[..., 6 * W:]], -1).astype(BF16)
    rkv = w_in[..., 3 * W:6 * W].reshape(DEPTH, D, 3, W)[..., perm].reshape(DEPTH, D, 3 * W)
    w_t = jnp.concatenate(
        [rkv, wkv_w1[:, 0], wkv_w1[:, 1], wkv_a1[:, 0], wkv_a1[:, 1], wkv_g1,
         jnp.zeros((DEPTH, D, N_LORA - 4 * LORA - LORA_G), F32)], -1)
    w_t = jnp.swapaxes(w_t, 1, 2).astype(BF16)
    w_out_b = w_out.astype(BF16)
    ln1 = ln1_g.reshape(DEPTH, 1, D)
    ln2 = ln2_g.reshape(DEPTH, 1, D)
    bias_tab = _bias_table(rpb)
    ck = cache_k.reshape(B_LAT, DEPTH, L_CTX, W)
    cv = cache_v.reshape(B_LAT, DEPTH, L_CTX, W)
    s0_lat = jnp.transpose(state_wkv, (1, 5, 4, 0, 2, 3)).reshape(DEPTH, DH, DH, B_LAT * 2 * H)
    s0_ctx = jnp.zeros((DH, DH, B_CTX * 2 * H), F32)

    hy = {}
    for seq in (L_CTX, L_LAT):
        mats = _dft_mats(seq)
        filt = _hyena_filter(p, seq)
        hy[seq] = (mats, _spectrum(mats[0], filt, seq))

    def layer(carry, l):
        x, k_acc, v_acc = carry
        lidx = jnp.reshape(l, (1,)).astype(jnp.int32)
        proj, h = _in_proj(lidx, x, ln1, mod, w_a)
        k_acc, v_acc = _kv_out(lidx, proj, k_acc, v_acc)
        proj_t = _in_proj_t(lidx, h, w_t)
        oa = _attn_lat(lidx, proj, ck, cv, bias_tab, _attn_ctx(lidx, proj))
        m_r, s_ctx = _rwkv_branch(lidx, proj, proj_t, p, s0_ctx, None, nb=B_CTX, seq=L_CTX,
                                  rb_t=0, rb_tok=0)
        s0 = lax.dynamic_index_in_dim(s0_lat, l, 0, keepdims=False)
        m_r, _ = _rwkv_branch(lidx, proj, proj_t, p, s0, m_r, nb=B_LAT, seq=L_LAT,
                              rb_t=N_CTX // L_LAT, rb_tok=N_CTX // 256)
        oc = _hyena(lidx, proj, p, *hy[L_CTX], None, nb=B_CTX, seq=L_CTX, rb=0)
        oc = _hyena(lidx, proj, p, *hy[L_LAT], oc, nb=B_LAT, seq=L_LAT, rb=N_CTX // L_LAT)
        merged = _merge(lidx, oa, oc, m_r, proj, p)
        x, h2 = _out_proj(lidx, merged, x, mod, w_out_b, ln2)
        x = _ffn(lidx, x, h2, mod, p)
        return (x, k_acc, v_acc), s_ctx

    kv0 = jnp.zeros((B_CTX, DEPTH, L_CTX, W), F32)
    (x, k_acc, v_acc), ss = lax.scan(layer, (x, kv0, kv0), jnp.arange(DEPTH, dtype=jnp.int32))
    y_prompt = _final_norm(x, final_g, 0, N_CTX).reshape(B_CTX, L_CTX, D)
    y_sample = _final_norm(x, final_g, N_CTX, N_LAT).reshape(B_LAT, L_LAT, D)
    new_k = k_acc.reshape(B_CTX, DEPTH, L_CTX, H, DH)
    new_v = v_acc.reshape(B_CTX, DEPTH, L_CTX, H, DH)
    new_s = jnp.transpose(ss.reshape(DEPTH, DH, DH, B_CTX, 2, H), (3, 0, 4, 5, 2, 1))
    return (y_prompt, y_sample, new_k, new_v, new_s)
```

```python
import functools
import math

import numpy as np
import jax
import jax.numpy as jnp
from jax import lax
from jax.experimental import pallas as pl
from jax.experimental.pallas import tpu as pltpu

F32 = jnp.float32
BF16 = jnp.bfloat16

D = 2048
DEPTH = 4
B_CTX, L_CTX = 32, 256
B_LAT, L_LAT = 4, 2048
N_CTX = B_CTX * L_CTX
N_LAT = B_LAT * L_LAT
N_TOK = N_CTX + N_LAT
H = 16
DH = 64
W = H * DH
GRID_W = 64
ROWS = L_LAT // GRID_W
WIN_R, WIN_C = 8, 16
NL = WIN_R * GRID_W
LORA = 64
LORA_G = 128
N_LORA = 512
N_A = 6 * W + 3 * D
HY_OFF = 3 * W
GL_OFF = 6 * W
N_T = 3 * W + N_LORA
D_FF = 4 * D
N_MOD = 6
NORM_EPS = 1e-6
GN_EPS = 64e-5
NEG_INF = -1e30
POS_BANDS = 16
FH_PAD = 128
LANES = 128
GB = LANES // (2 * H)
TCH = 128
WKV_RUN = 16
WKV_STAGE = 4
VMEM_LIMIT = 56 * 1024 * 1024


def _params(sem):
    return pltpu.CompilerParams(dimension_semantics=sem, vmem_limit_bytes=VMEM_LIMIT)


def _call(kernel, lidx, args, *, grid, in_specs, out_specs, out_shape, scratch=(), sem, name,
          aliases=None):
    gs = pltpu.PrefetchScalarGridSpec(num_scalar_prefetch=1, grid=grid, in_specs=in_specs,
                                      out_specs=out_specs, scratch_shapes=list(scratch))
    return pl.pallas_call(kernel, grid_spec=gs, out_shape=out_shape,
                          compiler_params=_params(sem), name=name,
                          input_output_aliases=aliases or {})(lidx, *args)


_ANY = pl.BlockSpec(memory_space=pl.ANY)


def _mod_row(i, tm):
    start = i * tm
    return jnp.where(start < N_CTX, 0, 1 + (start - N_CTX) // L_LAT)


def _modnorm(x, g, sc, sh):
    y = x * lax.rsqrt(jnp.mean(x * x, -1, keepdims=True) + NORM_EPS)
    return (y * g) * (1.0 + sc) + sh


def _dot(a, b):
    return jnp.dot(a, b, preferred_element_type=F32)


def _dot_nt(a, b):
    return lax.dot_general(a, b, (((1,), (1,)), ((), ())), preferred_element_type=F32)


def _dot_tn(a, b):
    return lax.dot_general(a, b, (((0,), (0,)), ((), ())), preferred_element_type=F32)


def _mod_kernel(c_ref, w_ref, b_ref, o_ref):
    c = c_ref[...]
    s = c * jax.nn.sigmoid(c)
    o_ref[...] = _dot(s.astype(BF16), w_ref[...].astype(BF16)) + b_ref[...]


def _modulation(cvec8, w_mod, b_mod):
    tn = 1024
    return pl.pallas_call(
        _mod_kernel,
        grid=(DEPTH, N_MOD * D // tn),
        in_specs=[pl.BlockSpec((8, D), lambda l, j: (0, 0)),
                  pl.BlockSpec((None, D, tn), lambda l, j: (l, 0, j)),
                  pl.BlockSpec((None, 1, tn), lambda l, j: (l, 0, j))],
        out_specs=pl.BlockSpec((None, 8, tn), lambda l, j: (l, 0, j)),
        out_shape=jax.ShapeDtypeStruct((DEPTH, 8, N_MOD * D), F32),
        compiler_params=_params(("parallel", "parallel")), name="modulation",
    )(cvec8, w_mod, b_mod.reshape(DEPTH, 1, N_MOD * D))


def _in_proj_kernel(l_ref, x_ref, g_ref, sh_ref, sc_ref, w_ref, o_ref, h_ref, *, tm):
    @pl.when(pl.program_id(1) == 0)
    def _():
        row = _mod_row(pl.program_id(0), tm)
        h = _modnorm(x_ref[...], g_ref[...], sc_ref[pl.ds(row, 1), :], sh_ref[pl.ds(row, 1), :])
        h_ref[...] = h.astype(BF16)

    o_ref[...] = _dot(h_ref[...], w_ref[...])


def _in_proj(lidx, x, ln_g, mod, w):
    tm, tn = 1024, 512
    n_out = w.shape[2]
    return _call(
        functools.partial(_in_proj_kernel, tm=tm), lidx, (x, ln_g, mod, mod, w),
        grid=(N_TOK // tm, n_out // tn),
        in_specs=[pl.BlockSpec((tm, D), lambda i, j, l: (i, 0)),
                  pl.BlockSpec((None, 1, D), lambda i, j, l: (l[0], 0, 0)),
                  pl.BlockSpec((None, 8, D), lambda i, j, l: (l[0], 0, 0)),
                  pl.BlockSpec((None, 8, D), lambda i, j, l: (l[0], 0, 1)),
                  pl.BlockSpec((None, D, tn), lambda i, j, l: (l[0], 0, j))],
        out_specs=[pl.BlockSpec((tm, tn), lambda i, j, l: (i, j)),
                   pl.BlockSpec((tm, D), lambda i, j, l: (i, 0))],
        out_shape=[jax.ShapeDtypeStruct((N_TOK, n_out), F32), jax.ShapeDtypeStruct((N_TOK, D), BF16)],
        sem=("parallel", "arbitrary"), name="in_proj")


def _in_proj_t_kernel(l_ref, h_ref, w_ref, o_ref):
    o_ref[...] = _dot_nt(w_ref[...], h_ref[...])


def _in_proj_t(lidx, h, w):
    tm, tn = 1024, 896
    n_out = w.shape[1]
    return _call(
        _in_proj_t_kernel, lidx, (h, w),
        grid=(N_TOK // tm, n_out // tn),
        in_specs=[pl.BlockSpec((tm, D), lambda i, j, l: (i, 0)),
                  pl.BlockSpec((None, tn, D), lambda i, j, l: (l[0], j, 0))],
        out_specs=pl.BlockSpec((tn, tm), lambda i, j, l: (j, i)),
        out_shape=jax.ShapeDtypeStruct((n_out, N_TOK), F32),
        sem=("parallel", "parallel"), name="in_proj_t")


def _kv_out_kernel(l_ref, k_ref, v_ref, ka_ref, va_ref, ko_ref, vo_ref):
    ko_ref[...] = k_ref[...]
    vo_ref[...] = v_ref[...]


def _kv_out(lidx, proj, k_acc, v_acc):
    cur = pl.BlockSpec((L_CTX, W), lambda b, l: (b, 1))
    cur_v = pl.BlockSpec((L_CTX, W), lambda b, l: (b, 2))
    dst = pl.BlockSpec((None, None, L_CTX, W), lambda b, l: (b, l[0], 0, 0))
    shape = jax.ShapeDtypeStruct((B_CTX, DEPTH, L_CTX, W), F32)
    return _call(
        _kv_out_kernel, lidx, (proj, proj, k_acc, v_acc),
        grid=(B_CTX,), in_specs=[cur, cur_v, _ANY, _ANY],
        out_specs=[dst, dst], out_shape=[shape, shape],
        sem=("parallel",), name="kv_out", aliases={3: 0, 4: 1})


ATT_CH = 8


def _attn_ctx_kernel(l_ref, q_ref, k_ref, v_ref, o_ref, s_ref):
    scale = DH ** -0.5
    for hh in range(ATT_CH):
        sl = slice(hh * DH, (hh + 1) * DH)
        q = (q_ref[:, sl] * scale).astype(BF16)
        s_ref[hh] = _dot_nt(q, k_ref[:, sl].astype(BF16))
    outs = []
    for hh in range(ATT_CH):
        sl = slice(hh * DH, (hh + 1) * DH)
        s = s_ref[hh]
        p = jnp.exp(s - jnp.max(s, -1, keepdims=True))
        den = jnp.sum(p, -1, keepdims=True)
        outs.append(_dot(p.astype(BF16), v_ref[:, sl].astype(BF16)) / den)
    o_ref[...] = jnp.concatenate(outs, -1).astype(BF16)


def _attn_ctx(lidx, proj):
    wb = ATT_CH * DH
    nq = W // wb
    return _call(
        _attn_ctx_kernel, lidx, (proj, proj, proj),
        grid=(B_CTX, nq),
        in_specs=[pl.BlockSpec((L_CTX, wb), lambda b, p, l: (b, p)),
                  pl.BlockSpec((L_CTX, wb), lambda b, p, l: (b, nq + p)),
                  pl.BlockSpec((L_CTX, wb), lambda b, p, l: (b, 2 * nq + p))],
        out_specs=pl.BlockSpec((L_CTX, wb), lambda b, p, l: (b, p)),
        out_shape=jax.ShapeDtypeStruct((N_TOK, W), BF16),
        scratch=[pltpu.VMEM((ATT_CH, L_CTX, L_CTX), F32)],
        sem=("parallel", "parallel"), name="attn_ctx")


ATT_RB = 8


def _attn_lat_kernel(l_ref, q_ref, k_ref, v_ref, kc_ref, vc_ref, bias_ref, alias_ref, o_ref,
                     kb_ref, vb_ref, kcb_ref, vcb_ref, s_ref):
    scale = DH ** -0.5
    for hh in range(2):
        sl = slice(hh * DH, (hh + 1) * DH)
        kb_ref[hh] = k_ref[:, sl].astype(BF16)
        vb_ref[hh] = v_ref[:, sl].astype(BF16)
        kcb_ref[hh] = kc_ref[:, sl].astype(BF16)
        vcb_ref[hh] = vc_ref[:, sl].astype(BF16)

    def window(r):
        r0 = jnp.clip(r - WIN_R // 2, 0, ROWS - WIN_R)
        return r - r0, pl.multiple_of(r * GRID_W, GRID_W), pl.multiple_of(r0 * GRID_W, GRID_W)

    def row_block(rr, carry):
        for i in range(ATT_RB):
            d, qrow, krow = window(rr * ATT_RB + i)
            for hh in range(2):
                sl = slice(hh * DH, (hh + 1) * DH)
                q = (q_ref[pl.ds(qrow, GRID_W), sl] * scale).astype(BF16)
                s_ref[2 * i + hh, :, :NL] = _dot_nt(q, kb_ref[hh, pl.ds(krow, NL), :]) + bias_ref[hh, d]
                s_ref[2 * i + hh, :, NL:] = _dot_nt(q, kcb_ref[hh])
        for i in range(ATT_RB):
            d, qrow, krow = window(rr * ATT_RB + i)
            outs = []
            for hh in range(2):
                s = s_ref[2 * i + hh]
                p = jnp.exp(s - jnp.max(s, -1, keepdims=True))
                den = jnp.sum(p, -1, keepdims=True)
                pb = p.astype(BF16)
                o = _dot(pb[:, :NL], vb_ref[hh, pl.ds(krow, NL), :]) + _dot(pb[:, NL:], vcb_ref[hh])
                outs.append(o / den)
            o_ref[pl.ds(qrow, GRID_W), :] = jnp.concatenate(outs, -1).astype(BF16)
        return carry

    lax.fori_loop(0, ROWS // ATT_RB, row_block, 0)


def _attn_lat(lidx, proj, cache_k, cache_v, bias_tab, o_ctx):
    nq = W // LANES
    rb = N_CTX // L_LAT
    return _call(
        _attn_lat_kernel, lidx, (proj, proj, proj, cache_k, cache_v, bias_tab, o_ctx),
        grid=(B_LAT, nq),
        in_specs=[pl.BlockSpec((L_LAT, LANES), lambda b, p, l: (rb + b, p)),
                  pl.BlockSpec((L_LAT, LANES), lambda b, p, l: (rb + b, nq + p)),
                  pl.BlockSpec((L_LAT, LANES), lambda b, p, l: (rb + b, 2 * nq + p)),
                  pl.BlockSpec((None, None, L_CTX, LANES), lambda b, p, l: (b, l[0], 0, p)),
                  pl.BlockSpec((None, None, L_CTX, LANES), lambda b, p, l: (b, l[0], 0, p)),
                  pl.BlockSpec((None, 2, WIN_R, GRID_W, NL), lambda b, p, l: (l[0], p, 0, 0, 0)),
                  _ANY],
        out_specs=pl.BlockSpec((L_LAT, LANES), lambda b, p, l: (rb + b, p)),
        out_shape=jax.ShapeDtypeStruct((N_TOK, W), BF16),
        scratch=[pltpu.VMEM((2, L_LAT, DH), BF16), pltpu.VMEM((2, L_LAT, DH), BF16),
                 pltpu.VMEM((2, L_CTX, DH), BF16), pltpu.VMEM((2, L_CTX, DH), BF16),
                 pltpu.VMEM((2 * ATT_RB, GRID_W, NL + L_CTX), F32)],
        sem=("parallel", "parallel"), name="attn_lat", aliases={7: 0})


def _bias_table(rpb):
    cq = np.arange(GRID_W)
    c0 = np.clip(cq - WIN_C // 2, 0, GRID_W - WIN_C)
    ck = np.arange(GRID_W)
    col_ok = (ck[None, :] >= c0[:, None]) & (ck[None, :] < c0[:, None] + WIN_C)
    dc = np.clip(ck[None, :] - cq[:, None], -(WIN_C - 1), WIN_C - 1) + (WIN_C - 1)
    onehot = (dc.reshape(-1)[None, :] == np.arange(2 * WIN_C - 1)[:, None]).astype(np.float32)
    cols = jnp.einsum('lhrc,cx->lhrx', rpb, jnp.asarray(onehot), precision=lax.Precision.HIGHEST)
    cols = cols.reshape(DEPTH, H, 2 * WIN_R - 1, GRID_W, GRID_W)
    tab = jnp.stack([cols[:, :, WIN_R - 1 - d:2 * WIN_R - 1 - d] for d in range(WIN_R)], 2)
    tab = jnp.transpose(tab, (0, 1, 2, 4, 3, 5))
    tab = jnp.where(col_ok[None, None, None, :, None, :], tab, NEG_INF)
    return tab.reshape(DEPTH, H, WIN_R, GRID_W, NL).astype(F32)


def _conv3(x, w_ref, b_ref):
    n = x.shape[0]
    row = lax.broadcasted_iota(jnp.int32, x.shape, 0)
    prev = jnp.where(row == 0, 0.0, pltpu.roll(x, 1, 0))
    nxt = jnp.where(row == n - 1, 0.0, pltpu.roll(x, n - 1, 0))
    return prev * w_ref[0:1, :] + x * w_ref[1:2, :] + nxt * w_ref[2:3, :] + b_ref[...]


def _conv3_t(x, p):
    n = x.shape[1]
    lane = lax.broadcasted_iota(jnp.int32, x.shape, 1)
    prev = jnp.where(lane == 0, 0.0, pltpu.roll(x, 1, 1))
    nxt = jnp.where(lane == n - 1, 0.0, pltpu.roll(x, n - 1, 1))
    return prev * p[:, 0:1] + x * p[:, 1:2] + nxt * p[:, 2:3] + p[:, 3:4]


def _wkv_prep_kernel(l_ref, r_ref, k_ref, v_ref, lora_ref, cp_ref, rp_ref, w2_ref, a2_ref, g2_ref,
                     ro_ref, ko_ref, vo_ref, d0_ref, d1_ref, a0o_ref, a1o_ref, go_ref):
    ro_ref[...] = _conv3_t(r_ref[...], cp_ref[0])
    ko_ref[...] = _conv3_t(k_ref[...], cp_ref[1])
    vo_ref[...] = _conv3_t(v_ref[...], cp_ref[2])
    rp = rp_ref[...]
    for e, (d_ref, ao_ref) in enumerate(((d0_ref, a0o_ref), (d1_ref, a1o_ref))):
        lw = jnp.tanh(lora_ref[e * LORA:(e + 1) * LORA, :]).astype(BF16)
        la = lora_ref[(2 + e) * LORA:(3 + e) * LORA, :].astype(BF16)
        w_log = rp[:, e:e + 1] + _dot(w2_ref[e].astype(BF16), lw)
        d_ref[...] = jnp.exp(-math.exp(-0.5) * jax.nn.sigmoid(w_log))
        ao_ref[...] = jax.nn.sigmoid(rp[:, 2 + e:3 + e] + _dot(a2_ref[e].astype(BF16), la))
    lg = jax.nn.sigmoid(lora_ref[4 * LORA:4 * LORA + LORA_G, :]).astype(BF16)
    go_ref[...] = _dot(g2_ref[...].astype(BF16), lg)


def _wkv_prep(lidx, proj_t, p, *, nb, seq, rb):
    ct = 512 if seq <= 256 else 128
    nc = W // ct

    def row(c):
        return pl.BlockSpec((ct, seq), lambda b, j, l: (c * nc + j, rb + b))

    out = pl.BlockSpec((None, ct, seq), lambda b, j, l: (b, j, 0))
    return _call(
        _wkv_prep_kernel, lidx,
        (proj_t, proj_t, proj_t, proj_t, p['wkv_conv'], p['wkv_rowp'], p['wkv_w2t'], p['wkv_a2t'],
         p['wkv_g2t']),
        grid=(nb, nc),
        in_specs=[row(0), row(1), row(2),
                  pl.BlockSpec((N_LORA, seq), lambda b, j, l: (3 * W // N_LORA, rb + b)),
                  pl.BlockSpec((None, 3, ct, 4), lambda b, j, l: (l[0], 0, j, 0)),
                  pl.BlockSpec((None, ct, 4), lambda b, j, l: (l[0], j, 0)),
                  pl.BlockSpec((None, 2, ct, LORA), lambda b, j, l: (l[0], 0, j, 0)),
                  pl.BlockSpec((None, 2, ct, LORA), lambda b, j, l: (l[0], 0, j, 0)),
                  pl.BlockSpec((None, ct, LORA_G), lambda b, j, l: (l[0], j, 0))],
        out_specs=[out] * 8,
        out_shape=[jax.ShapeDtypeStruct((nb, W, seq), F32)] * 8,
        sem=("parallel", "parallel"), name=f"wkv_prep_{seq}")


def _to_chains_kernel(l_ref, *refs, shared):
    o_ref = refs[-1]
    xs = [x for x in refs[:-1] for _ in range(2)] if shared else refs[:-1]
    for k0 in range(0, DH, 8):
        tiles = []
        for k in range(k0, k0 + 8):
            m = jnp.concatenate([x[k * H:(k + 1) * H, :] for x in xs], 0)
            tiles.append(m.T)
        o_ref[:, k0:k0 + 8, :] = pltpu.einshape("ktc->tkc", jnp.stack(tiles, 0))


def _to_chains(lidx, x0, x1, *, nb, seq):
    def src(b):
        return pl.BlockSpec((None, W, TCH), lambda g, t, l: (GB * g + b, 0, t))

    shared = x1 is None
    if shared:
        args = (x0,) * GB
        in_specs = [src(b) for b in range(GB)]
    else:
        args = tuple(x0 if i % 2 == 0 else x1 for i in range(2 * GB))
        in_specs = [src(i // 2) for i in range(2 * GB)]
    return _call(
        functools.partial(_to_chains_kernel, shared=shared), lidx, args,
        grid=(nb // GB, seq // TCH),
        in_specs=in_specs,
        out_specs=pl.BlockSpec((TCH, DH, LANES), lambda g, t, l: (t, 0, g)),
        out_shape=jax.ShapeDtypeStruct((seq, DH, nb * 2 * H), F32),
        sem=("parallel", "parallel"), name=f"to_chains_{seq}")


def _wkv_kernel(l_ref, rf_ref, kf_ref, vf_ref, wf_ref, af_ref, rb_ref, kb_ref, vb_ref, wb_ref, ab_ref,
                kkp_ref, kap_ref, s0_ref, yf_ref, yb_ref, s_ref, st_ref, p_ref, *, tb):
    @pl.when(pl.program_id(1) == 0)
    def _():
        s_ref[...] = s0_ref[...]

    kkp = kkp_ref[...]
    kap = kap_ref[...]
    lane = lax.broadcasted_iota(jnp.int32, (DH, LANES), 1)
    bwd = (lane // H) % 2 == 1
    kc = 32
    zero = jnp.zeros((DH, LANES), F32)

    def stage(t, u, p_prev):
        tr = tb - 1 - t
        pick = lambda f_ref, b_ref: jnp.where(bwd, b_ref[tr], f_ref[t])
        kt = pick(kf_ref, kb_ref)
        at = pick(af_ref, ab_ref)
        kk = kt * kkp
        kk = kk * lax.rsqrt(jnp.sum(kk * kk, 0, keepdims=True) + 1e-12)
        p_new = p_prev * pick(wf_ref, wb_ref)
        p_inv = 1.0 / p_new
        st_ref[u, 0] = kk * p_prev
        st_ref[u, 1] = (kk * at) * p_inv
        st_ref[u, 2] = (kt * (1.0 + (at - 1.0) * kap)) * p_inv
        st_ref[u, 3] = pick(rf_ref, rb_ref) * p_new
        st_ref[u, 4] = pick(vf_ref, vb_ref)
        return p_new

    def step(t, u):
        def sa_body(c, acc):
            a0, a1 = acc
            for j in range(kc):
                k = c * kc + j
                term = s_ref[k] * st_ref[u, 0, pl.ds(k, 1), :]
                if j % 2 == 0:
                    a0 = a0 + term
                else:
                    a1 = a1 + term
            return a0, a1

        a0, a1 = lax.fori_loop(0, DH // kc, sa_body, (zero, zero))
        sa = -(a0 + a1)
        vt = st_ref[u, 4]

        def up_body(c, acc):
            y0, y1 = acc
            for j in range(kc):
                k = c * kc + j
                sk = s_ref[k] + (sa * st_ref[u, 1, pl.ds(k, 1), :] + vt * st_ref[u, 2, pl.ds(k, 1), :])
                s_ref[k] = sk
                term = sk * st_ref[u, 3, pl.ds(k, 1), :]
                if j % 2 == 0:
                    y0 = y0 + term
                else:
                    y1 = y1 + term
            return y0, y1

        y0, y1 = lax.fori_loop(0, DH // kc, up_body, (zero, zero))
        y = y0 + y1
        yf_ref[t] = y
        yb_ref[tb - 1 - t] = y

    def group(t0):
        p = p_ref[...]
        for u in range(WKV_STAGE):
            p = stage(t0 + u, u, p)
        p_ref[...] = p
        for u in range(WKV_STAGE):
            step(t0 + u, u)

    def rescale(c, carry):
        for j in range(kc):
            k = c * kc + j
            s_ref[k] = s_ref[k] * p_ref[pl.ds(k, 1), :]
        return carry

    def run(i, carry):
        p_ref[...] = jnp.ones((DH, LANES), F32)

        def groups(q, c):
            group(i * WKV_RUN + q * WKV_STAGE)
            return c

        lax.fori_loop(0, WKV_RUN // WKV_STAGE, groups, 0)
        lax.fori_loop(0, DH // kc, rescale, 0)
        return carry

    lax.fori_loop(0, tb // WKV_RUN, run, 0)


def _wkv(lidx, r, k, v, w, a, kkp, kap, s0, *, seq):
    tb = 32
    chains = r.shape[-1]
    nt = seq // tb
    f_spec = pl.BlockSpec((tb, DH, LANES), lambda g, t, l: (t, 0, g))
    b_spec = pl.BlockSpec((tb, DH, LANES), lambda g, t, l: (nt - 1 - t, 0, g))
    par_spec = pl.BlockSpec((None, DH, LANES), lambda g, t, l: (l[0], 0, g))
    st_spec = pl.BlockSpec((DH, DH, LANES), lambda g, t, l: (0, 0, g))
    y_shape = jax.ShapeDtypeStruct((seq, DH, chains), F32)
    return _call(
        functools.partial(_wkv_kernel, tb=tb), lidx, (r, k, v, w, a, r, k, v, w, a, kkp, kap, s0),
        grid=(chains // LANES, nt),
        in_specs=[f_spec] * 5 + [b_spec] * 5 + [par_spec, par_spec, st_spec],
        out_specs=[f_spec, b_spec, st_spec],
        out_shape=[y_shape, y_shape, jax.ShapeDtypeStruct((DH, DH, chains), F32)],
        scratch=[pltpu.VMEM((WKV_STAGE, 5, DH, LANES), F32), pltpu.VMEM((DH, LANES), F32)],
        sem=("parallel", "arbitrary"), name=f"wkv_{seq}")


def _from_chains_kernel(l_ref, yf_ref, yb_ref, o_ref):
    for v0 in range(0, DH, 8):
        f3 = pltpu.einshape("tvc->vtc", yf_ref[:, v0:v0 + 8, :])
        b3 = pltpu.einshape("tvc->vtc", yb_ref[:, v0:v0 + 8, :])
        for i in range(8):
            v = v0 + i
            tf = f3[i].T
            tb = b3[i].T
            for b in range(GB):
                lo = b * 2 * H
                o_ref[b, v * H:(v + 1) * H, :] = tf[lo:lo + H] + tb[lo + H:lo + 2 * H]


def _from_chains(lidx, yf, yb, *, nb, seq):
    y_spec = pl.BlockSpec((TCH, DH, LANES), lambda g, t, l: (t, 0, g))
    return _call(
        _from_chains_kernel, lidx, (yf, yb),
        grid=(nb // GB, seq // TCH),
        in_specs=[y_spec, y_spec],
        out_specs=pl.BlockSpec((GB, W, TCH), lambda g, t, l: (g, 0, t)),
        out_shape=jax.ShapeDtypeStruct((nb, W, seq), F32),
        sem=("parallel", "parallel"), name=f"from_chains_{seq}")


def _wkv_post_kernel(l_ref, y_ref, r_ref, k_ref, v_ref, g_ref, gate_ref, par_ref, w_ref, *rest):
    o_ref = rest[-1]
    t = y_ref.shape[-1]
    par = par_ref[...]
    y = y_ref[...].reshape(DH, H, t)
    mu = jnp.mean(y, 0, keepdims=True)
    yc = y - mu
    var = jnp.mean(yc * yc, 0, keepdims=True)
    yn = (yc * lax.rsqrt(var + GN_EPS)).reshape(W, t) * par[:, 1:2] + par[:, 2:3]
    rk = (r_ref[...] * k_ref[...] * par[:, 0:1]).reshape(DH, H, t)
    bonus = jnp.broadcast_to(jnp.sum(rk, 0, keepdims=True), (DH, H, t)).reshape(W, t) * v_ref[...]
    o = ((yn + bonus) * g_ref[...]).astype(BF16)
    o_ref[...] = jax.nn.sigmoid(gate_ref[...]) * _dot_tn(o, w_ref[...])


def _wkv_post(lidx, ysum, r, k, v, g, proj, p, prev, *, nb, seq, rb):
    tt = 256
    nt = seq // tt
    t_spec = pl.BlockSpec((None, W, tt), lambda b, t, l: (b, 0, t))
    tok = lambda b, t: rb + b * nt + t
    args = (ysum, r, k, v, g, proj, p['wkv_post'], p['w_pr'])
    in_specs = [t_spec] * 5 + [
        pl.BlockSpec((tt, D), lambda b, t, l: (tok(b, t), (GL_OFF + D) // D)),
        pl.BlockSpec((None, W, 4), lambda b, t, l: (l[0], 0, 0)),
        pl.BlockSpec((None, W, D), lambda b, t, l: (l[0], 0, 0))]
    aliases = None
    if prev is not None:
        args = args + (prev,)
        in_specs = in_specs + [_ANY]
        aliases = {len(args): 0}
    return _call(
        _wkv_post_kernel, lidx, args,
        grid=(nb, nt), in_specs=in_specs,
        out_specs=pl.BlockSpec((tt, D), lambda b, t, l: (tok(b, t), 0)),
        out_shape=jax.ShapeDtypeStruct((N_TOK, D), F32),
        sem=("parallel", "parallel"), name=f"wkv_post_{seq}", aliases=aliases)


def _head_param(p, reps):
    t = jnp.transpose(p.reshape(DEPTH, H, DH), (0, 2, 1))
    return jnp.tile(t, (1, 1, reps))


def _rwkv_branch(lidx, proj, proj_t, p, s0_chain, prev, *, nb, seq, rb_t, rb_tok):
    r, k, v, d0, d1, a0, a1, g = _wkv_prep(lidx, proj_t, p, nb=nb, seq=seq, rb=rb_t)
    tc = functools.partial(_to_chains, lidx, nb=nb, seq=seq)
    chains = nb * 2 * H
    yf, yb, s_fin = _wkv(lidx, tc(r, None), tc(k, None), tc(v, None), tc(d0, d1), tc(a0, a1),
                         p['kk_chain'][:, :, :chains], p['ka_chain'][:, :, :chains], s0_chain, seq=seq)
    ysum = _from_chains(lidx, yf, yb, nb=nb, seq=seq)
    m_r = _wkv_post(lidx, ysum, r, k, v, g, proj, p, prev, nb=nb, seq=seq, rb=rb_tok)
    return m_r, s_fin


def _pos_features(seq):
    t = np.linspace(0.0, 1.0, seq, dtype=np.float32)[:, None]
    w = 2.0 * np.pi * np.arange(seq, dtype=np.float32)[:, None] / seq
    f = np.linspace(1e-4, POS_BANDS - 1, POS_BANDS, dtype=np.float32)[None, :]
    z = np.concatenate([t, np.cos(f * w), -np.sin(f * w)], -1).astype(np.float32)
    zp = np.zeros((seq, FH_PAD), np.float32)
    zp[:, :z.shape[1]] = z
    dist = (np.abs(np.arange(seq) - seq // 2).astype(np.float32) / seq)[:, None]
    deltas = np.abs(np.linspace(math.log(1e-2) / 1.5, math.log(1e-2) / 0.3, W,
                                dtype=np.float32))[None, :]
    return zp, dist, deltas


def _filter_kernel(zp_ref, dist_ref, del_ref, f1_ref, b1_ref, f2_ref, b2_ref, fr_ref, f3_ref, o_ref,
                   t_ref):
    hi = lax.Precision.HIGHEST

    @pl.when(pl.program_id(1) == 0)
    def _():
        fr = fr_ref[...]
        t = jnp.sin(fr * (jnp.dot(zp_ref[...], f1_ref[...], precision=hi,
                                  preferred_element_type=F32) + b1_ref[...]))
        t_ref[...] = jnp.sin(fr * (jnp.dot(t, f2_ref[...], precision=hi,
                                           preferred_element_type=F32) + b2_ref[...]))

    filt = jnp.dot(t_ref[...], f3_ref[...], precision=hi, preferred_element_type=F32)
    filt = filt * jnp.exp(-dist_ref[...] * del_ref[...])
    o_ref[...] = filt / (jnp.sum(jnp.abs(filt), 0, keepdims=True) + 1e-6)


def _hyena_filter(p, seq):
    ct = 256
    zp, dist, deltas = _pos_features(seq)
    full = lambda shape: pl.BlockSpec((None,) + shape, lambda l, j: (l,) + (0,) * len(shape))
    return pl.pallas_call(
        _filter_kernel,
        grid=(DEPTH, W // ct),
        in_specs=[pl.BlockSpec((seq, FH_PAD), lambda l, j: (0, 0)),
                  pl.BlockSpec((seq, 1), lambda l, j: (0, 0)),
                  pl.BlockSpec((1, ct), lambda l, j: (0, j)),
                  full((FH_PAD, FH_PAD)), full((1, FH_PAD)), full((FH_PAD, FH_PAD)),
                  full((1, FH_PAD)), full((1, FH_PAD)),
                  pl.BlockSpec((None, FH_PAD, ct), lambda l, j: (l, 0, j))],
        out_specs=pl.BlockSpec((None, seq, ct), lambda l, j: (l, 0, j)),
        out_shape=jax.ShapeDtypeStruct((DEPTH, seq, W), F32),
        scratch_shapes=[pltpu.VMEM((seq, FH_PAD), F32)],
        compiler_params=_params(("parallel", "arbitrary")), name=f"hyena_filter_{seq}",
    )(jnp.asarray(zp), jnp.asarray(dist), jnp.asarray(deltas),
      p['hy_f1p'], p['hy_fb1p'], p['hy_f2p'], p['hy_fb2p'], p['hy_freqp'], p['hy_f3p'])


def _freq_block(seq):
    return min(seq, 256)


def _dft_mats(seq):
    n = 2 * seq
    fq = _freq_block(seq)
    k = jnp.arange(seq, dtype=jnp.int32)
    t = jnp.arange(seq, dtype=jnp.int32)
    ph = ((2 * k[:, None] + 1) * t[None, :]) % (2 * n)
    ang = ph.astype(F32) * np.float32(np.pi / n)
    fwd = jnp.stack([jnp.cos(ang).reshape(seq // fq, fq, seq),
                     jnp.sin(ang).reshape(seq // fq, fq, seq)], 1).reshape(2 * seq, seq)
    m = t + seq // 2
    ph2 = ((2 * k[None, :] + 1) * m[:, None]) % (2 * n)
    ang2 = ph2.astype(F32) * np.float32(np.pi / n)
    inv = jnp.stack([jnp.cos(ang2).reshape(seq, seq // fq, fq),
                     jnp.sin(ang2).reshape(seq, seq // fq, fq)], 2).reshape(seq, 2 * seq)
    inv = inv * np.float32(2.0 / n)
    return fwd.astype(BF16), inv.astype(BF16)


def _spectrum_kernel(a_ref, b_ref, o_ref):
    o_ref[...] = _dot(a_ref[...], b_ref[...].astype(BF16))


def _spectrum(fwd, filt, seq):
    tmm, tn = 512, 512
    return pl.pallas_call(
        _spectrum_kernel,
        grid=(DEPTH, 2 * seq // tmm, W // tn),
        in_specs=[pl.BlockSpec((tmm, seq), lambda l, i, j: (i, 0)),
                  pl.BlockSpec((None, seq, tn), lambda l, i, j: (l, 0, j))],
        out_specs=pl.BlockSpec((None, tmm, tn), lambda l, i, j: (l, i, j)),
        out_shape=jax.ShapeDtypeStruct((DEPTH, 2 * seq, W), F32),
        compiler_params=_params(("parallel", "parallel", "parallel")), name=f"hyena_spectrum_{seq}",
    )(fwd, filt)


def _hyena_kernel(l_ref, x0_ref, x1_ref, vv_ref, cw0_ref, cw1_ref, cw2_ref, cb0_ref, cb1_ref, cb2_ref,
                  fwd_ref, inv_ref, h_ref, d_ref, *rest):
    o_ref, z_ref, zb_ref, acc_ref = rest[-4:]
    fb = pl.program_id(2)
    fq = h_ref.shape[0] // 2

    @pl.when(fb == 0)
    def _():
        z = _conv3(vv_ref[...], cw2_ref, cb2_ref) * _conv3(x1_ref[...], cw1_ref, cb1_ref)
        z_ref[...] = z
        zb_ref[...] = z.astype(BF16)
        acc_ref[...] = jnp.zeros_like(acc_ref)

    zf = _dot(fwd_ref[...], zb_ref[...])
    zc, zs = zf[:fq], zf[fq:]
    hc, hs = h_ref[:fq, :], h_ref[fq:, :]
    pr = zc * hc - zs * hs
    pq = zc * hs + zs * hc
    acc_ref[...] += _dot(inv_ref[...], jnp.concatenate([pr, pq], 0).astype(BF16))

    @pl.when(fb == pl.num_programs(2) - 1)
    def _():
        y = acc_ref[...] + z_ref[...] * d_ref[...]
        o_ref[...] = (_conv3(x0_ref[...], cw0_ref, cb0_ref) * y).astype(BF16)


def _hyena(lidx, proj, p, mats, spec, prev, *, nb, seq, rb):
    ct = 512
    fq = _freq_block(seq)
    nc = W // ct
    off = HY_OFF // ct
    fwd_hi, inv_hi = mats

    def col(c):
        return pl.BlockSpec((seq, ct), lambda b, j, f, l: (rb + b, off + c * nc + j))

    def cw(c):
        return pl.BlockSpec((None, 3, ct), lambda b, j, f, l: (l[0], 0, c * nc + j))

    def cb(c):
        return pl.BlockSpec((None, 1, ct), lambda b, j, f, l: (l[0], 0, c * nc + j))

    args = (proj, proj, proj, p['hy_conv_w'], p['hy_conv_w'], p['hy_conv_w'],
            p['hy_conv_b'], p['hy_conv_b'], p['hy_conv_b'], fwd_hi, inv_hi, spec, p['hy_d'])
    in_specs = [col(0), col(1), col(2), cw(0), cw(1), cw(2), cb(0), cb(1), cb(2),
                pl.BlockSpec((2 * fq, seq), lambda b, j, f, l: (f, 0)),
                pl.BlockSpec((seq, 2 * fq), lambda b, j, f, l: (0, f)),
                pl.BlockSpec((None, 2 * fq, ct), lambda b, j, f, l: (l[0], f, j)),
                pl.BlockSpec((None, 1, ct), lambda b, j, f, l: (l[0], 0, j))]
    aliases = None
    if prev is not None:
        args = args + (prev,)
        in_specs = in_specs + [_ANY]
        aliases = {len(args): 0}
    return _call(
        _hyena_kernel, lidx, args,
        grid=(nb, nc, seq // fq), in_specs=in_specs,
        out_specs=pl.BlockSpec((seq, ct), lambda b, j, f, l: (rb + b, j)),
        out_shape=jax.ShapeDtypeStruct((N_TOK, W), BF16),
        scratch=[pltpu.VMEM((seq, ct), F32), pltpu.VMEM((seq, ct), BF16), pltpu.VMEM((seq, ct), F32)],
        sem=("parallel", "parallel", "arbitrary"), name=f"hyena_{seq}", aliases=aliases)


def _merge_kernel(l_ref, oa_ref, oc_ref, mr_ref, ga_ref, gc_ref, wa_ref, wc_ref, o_ref):
    m = (jax.nn.sigmoid(ga_ref[...]) * _dot(oa_ref[...], wa_ref[...]) + mr_ref[...]
         + jax.nn.sigmoid(gc_ref[...]) * _dot(oc_ref[...], wc_ref[...]))
    o_ref[...] = m.astype(BF16)


def _merge(lidx, o_a, o_c, m_r, proj, p):
    tm, tn = 1024, 512
    goff = GL_OFF // tn
    nd = D // tn
    row = lambda: pl.BlockSpec((tm, W), lambda i, j, l: (i, 0))
    gate = lambda c: pl.BlockSpec((tm, tn), lambda i, j, l: (i, goff + c * nd + j))
    wsp = lambda: pl.BlockSpec((None, W, tn), lambda i, j, l: (l[0], 0, j))
    return _call(
        _merge_kernel, lidx, (o_a, o_c, m_r, proj, proj, p['w_pa'], p['w_pc']),
        grid=(N_TOK // tm, nd),
        in_specs=[row(), row(), pl.BlockSpec((tm, tn), lambda i, j, l: (i, j)),
                  gate(0), gate(2), wsp(), wsp()],
        out_specs=pl.BlockSpec((tm, tn), lambda i, j, l: (i, j)),
        out_shape=jax.ShapeDtypeStruct((N_TOK, D), BF16),
        sem=("parallel", "parallel"), name="merge")


def _out_proj_kernel(l_ref, m_ref, w_ref, x_ref, ga_ref, g_ref, sh_ref, sc_ref, o_ref, h_ref, *, tm):
    row = _mod_row(pl.program_id(0), tm)
    xn = x_ref[...] + ga_ref[pl.ds(row, 1), :] * _dot(m_ref[...], w_ref[...])
    o_ref[...] = xn
    h_ref[...] = _modnorm(xn, g_ref[...], sc_ref[pl.ds(row, 1), :], sh_ref[pl.ds(row, 1), :]).astype(BF16)


def _out_proj(lidx, merged, x, mod, w_out, ln_g):
    tm = 512
    rows = pl.BlockSpec((tm, D), lambda i, l: (i, 0))
    chunk = lambda c: pl.BlockSpec((None, 8, D), lambda i, l: (l[0], 0, c))
    return _call(
        functools.partial(_out_proj_kernel, tm=tm), lidx, (merged, w_out, x, mod, ln_g, mod, mod),
        grid=(N_TOK // tm,),
        in_specs=[rows, pl.BlockSpec((None, D, D), lambda i, l: (l[0], 0, 0)), rows, chunk(2),
                  pl.BlockSpec((None, 1, D), lambda i, l: (l[0], 0, 0)), chunk(3), chunk(4)],
        out_specs=[rows, rows],
        out_shape=[jax.ShapeDtypeStruct((N_TOK, D), F32), jax.ShapeDtypeStruct((N_TOK, D), BF16)],
        sem=("parallel",), name="out_proj")


def _ffn_kernel(l_ref, x_ref, h_ref, ga_ref, w1_ref, b1_ref, w2_ref, b2_ref, o_ref, acc_ref, *, tm):
    j = pl.program_id(1)
    row = _mod_row(pl.program_id(0), tm)

    @pl.when(j == 0)
    def _():
        acc_ref[...] = jnp.zeros_like(acc_ref)

    a = _dot(h_ref[...], w1_ref[...]) + b1_ref[...]
    a = jnp.square(jnp.maximum(a, 0.0))
    acc_ref[...] += _dot(a.astype(BF16), w2_ref[...])

    @pl.when(j == pl.num_programs(1) - 1)
    def _():
        o_ref[...] = x_ref[...] + ga_ref[pl.ds(row, 1), :] * (acc_ref[...] + b2_ref[...])


def _ffn(lidx, x, h, mod, p):
    tm, tf = 512, 512
    return _call(
        functools.partial(_ffn_kernel, tm=tm), lidx,
        (x, h, mod, p['w_ff1'], p['b_ff1'], p['w_ff2'], p['b_ff2']),
        grid=(N_TOK // tm, D_FF // tf),
        in_specs=[pl.BlockSpec((tm, D), lambda i, j, l: (i, 0)),
                  pl.BlockSpec((tm, D), lambda i, j, l: (i, 0)),
                  pl.BlockSpec((None, 8, D), lambda i, j, l: (l[0], 0, 5)),
                  pl.BlockSpec((None, D, tf), lambda i, j, l: (l[0], 0, j)),
                  pl.BlockSpec((None, 1, tf), lambda i, j, l: (l[0], 0, j)),
                  pl.BlockSpec((None, tf, D), lambda i, j, l: (l[0], j, 0)),
                  pl.BlockSpec((None, 1, D), lambda i, j, l: (l[0], 0, 0))],
        out_specs=pl.BlockSpec((tm, D), lambda i, j, l: (i, 0)),
        out_shape=jax.ShapeDtypeStruct((N_TOK, D), F32),
        scratch=[pltpu.VMEM((tm, D), F32)],
        sem=("parallel", "arbitrary"), name="ffn")


def _final_norm_kernel(x_ref, g_ref, o_ref):
    x = x_ref[...]
    o_ref[...] = x * lax.rsqrt(jnp.mean(x * x, -1, keepdims=True) + NORM_EPS) * g_ref[...]


def _final_norm(x, g, row0, nrows):
    tm = 1024
    rb = row0 // tm
    return pl.pallas_call(
        _final_norm_kernel,
        grid=(nrows // tm,),
        in_specs=[pl.BlockSpec((tm, D), lambda i: (rb + i, 0)), pl.BlockSpec((1, D), lambda i: (0, 0))],
        out_specs=pl.BlockSpec((tm, D), lambda i: (i, 0)),
        out_shape=jax.ShapeDtypeStruct((nrows, D), F32),
        compiler_params=_params(("parallel",)), name="final_norm",
    )(x, g.reshape(1, D))


def kernel(x_prompt, x_sample, cache_k, cache_v, state_wkv, c, c_ctx, ln1_g, ln2_g, w_mod, b_mod, w_in, rpb, wkv_conv_w, wkv_conv_b, wkv_w0, wkv_w1, wkv_w2, wkv_a0, wkv_a1, wkv_a2, wkv_g1, wkv_g2, wkv_k_k, wkv_k_a, wkv_r_k, wkv_gn_g, wkv_gn_b, hy_conv_w, hy_conv_b, hy_f1, hy_fb1, hy_f2, hy_fb2, hy_freq, hy_f3, hy_d, w_pa, w_pr, w_pc, w_out, w_ff1, b_ff1, w_ff2, b_ff2, final_g):
    x = jnp.concatenate([x_prompt.reshape(N_CTX, D), x_sample.reshape(N_LAT, D)], 0)
    cvec = jnp.zeros((8, D), F32).at[0].set(c_ctx).at[1:1 + B_LAT].set(c)
    mod = _modulation(cvec, w_mod, b_mod)

    perm = (np.arange(H)[None, :] * DH + np.arange(DH)[:, None]).reshape(-1)
    pad_c = lambda a, n: jnp.pad(a, [(0, 0)] * (a.ndim - 1) + [(0, n - a.shape[-1])])
    pad_r = lambda a, n: jnp.pad(a, [(0, 0)] * (a.ndim - 2) + [(0, n - a.shape[-2]), (0, 0)])
    conv = jnp.concatenate([wkv_conv_w, wkv_conv_b[:, None, :]], 1)
    conv = jnp.transpose(conv.reshape(DEPTH, 4, 3, W)[..., perm], (0, 2, 3, 1))
    rowp = jnp.stack([wkv_w0[:, 0], wkv_w0[:, 1], wkv_a0[:, 0], wkv_a0[:, 1]], -1)[:, perm]
    post = jnp.stack([wkv_r_k.reshape(DEPTH, W), wkv_gn_g, wkv_gn_b, jnp.zeros((DEPTH, W), F32)],
                     -1)[:, perm]
    p = {
        'wkv_conv': conv, 'wkv_rowp': rowp, 'wkv_post': post,
        'wkv_w2t': jnp.swapaxes(wkv_w2[..., perm], -1, -2),
        'wkv_a2t': jnp.swapaxes(wkv_a2[..., perm], -1, -2),
        'wkv_g2t': jnp.swapaxes(wkv_g2[..., perm], -1, -2),
        'kk_chain': _head_param(wkv_k_k, 2 * B_CTX), 'ka_chain': _head_param(wkv_k_a, 2 * B_CTX),
        'hy_conv_w': hy_conv_w, 'hy_conv_b': hy_conv_b.reshape(DEPTH, 1, 3 * W),
        'hy_d': hy_d.reshape(DEPTH, 1, W),
        'hy_f1p': pad_c(pad_r(hy_f1, FH_PAD), FH_PAD), 'hy_fb1p': pad_c(hy_fb1, FH_PAD).reshape(DEPTH, 1, FH_PAD),
        'hy_f2p': pad_c(pad_r(hy_f2, FH_PAD), FH_PAD), 'hy_fb2p': pad_c(hy_fb2, FH_PAD).reshape(DEPTH, 1, FH_PAD),
        'hy_freqp': pad_c(hy_freq, FH_PAD).reshape(DEPTH, 1, FH_PAD), 'hy_f3p': pad_r(hy_f3, FH_PAD),
        'w_pa': w_pa.astype(BF16), 'w_pr': w_pr[:, perm].astype(BF16), 'w_pc': w_pc.astype(BF16),
        'w_ff1': w_ff1.astype(BF16), 'b_ff1': b_ff1.reshape(DEPTH, 1, D_FF),
        'w_ff2': w_ff2.astype(BF16), 'b_ff2': b_ff2.reshape(DEPTH, 1, D),
    }
    w_a = jnp.concatenate([w_in[..., :3 * W], w_in[..., 6 * W:]], -1).astype(BF16)
    rkv = w_in[..., 3 * W:6 * W].reshape(DEPTH, D, 3, W)[..., perm].reshape(DEPTH, D, 3 * W)
    w_t = jnp.concatenate(
        [rkv, wkv_w1[:, 0], wkv_w1[:, 1], wkv_a1[:, 0], wkv_a1[:, 1], wkv_g1,
         jnp.zeros((DEPTH, D, N_LORA - 4 * LORA - LORA_G), F32)], -1)
    w_t = jnp.swapaxes(w_t, 1, 2).astype(BF16)
    w_out_b = w_out.astype(BF16)
    ln1 = ln1_g.reshape(DEPTH, 1, D)
    ln2 = ln2_g.reshape(DEPTH, 1, D)
    bias_tab = _bias_table(rpb)
    ck = cache_k.reshape(B_LAT, DEPTH, L_CTX, W)
    cv = cache_v.reshape(B_LAT, DEPTH, L_CTX, W)
    s0_lat = jnp.transpose(state_wkv, (1, 5, 4, 0, 2, 3)).reshape(DEPTH, DH, DH, B_LAT * 2 * H)
    s0_ctx = jnp.zeros((DH, DH, B_CTX * 2 * H), F32)

    hy = {}
    for seq in (L_CTX, L_LAT):
        mats = _dft_mats(seq)
        filt = _hyena_filter(p, seq)
        hy[seq] = (mats, _spectrum(mats[0], filt, seq))

    def layer(carry, l):
        x, k_acc, v_acc = carry
        lidx = jnp.reshape(l, (1,)).astype(jnp.int32)
        proj, h = _in_proj(lidx, x, ln1, mod, w_a)
        k_acc, v_acc = _kv_out(lidx, proj, k_acc, v_acc)
        proj_t = _in_proj_t(lidx, h, w_t)
        oa = _attn_lat(lidx, proj, ck, cv, bias_tab, _attn_ctx(lidx, proj))
        m_r, s_ctx = _rwkv_branch(lidx, proj, proj_t, p, s0_ctx, None, nb=B_CTX, seq=L_CTX,
                                  rb_t=0, rb_tok=0)
        s0 = lax.dynamic_index_in_dim(s0_lat, l, 0, keepdims=False)
        m_r, _ = _rwkv_branch(lidx, proj, proj_t, p, s0, m_r, nb=B_LAT, seq=L_LAT,
                              rb_t=N_CTX // L_LAT, rb_tok=N_CTX // 256)
        oc = _hyena(lidx, proj, p, *hy[L_CTX], None, nb=B_CTX, seq=L_CTX, rb=0)
        oc = _hyena(lidx, proj, p, *hy[L_LAT], oc, nb=B_LAT, seq=L_LAT, rb=N_CTX // L_LAT)
        merged = _merge(lidx, oa, oc, m_r, proj, p)
        x, h2 = _out_proj(lidx, merged, x, mod, w_out_b, ln2)
        x = _ffn(lidx, x, h2, mod, p)
        return (x, k_acc, v_acc), s_ctx

    kv0 = jnp.zeros((B_CTX, DEPTH, L_CTX, W), F32)
    (x, k_acc, v_acc), ss = lax.scan(layer, (x, kv0, kv0), jnp.arange(DEPTH, dtype=jnp.int32))
    y_prompt = _final_norm(x, final_g, 0, N_CTX).reshape(B_CTX, L_CTX, D)
    y_sample = _final_norm(x, final_g, N_CTX, N_LAT).reshape(B_LAT, L_LAT, D)
    new_k = k_acc.reshape(B_CTX, DEPTH, L_CTX, H, DH)
    new_v = v_acc.reshape(B_CTX, DEPTH, L_CTX, H, DH)
    new_s = jnp.transpose(ss.reshape(DEPTH, DH, DH, B_CTX, 2, H), (3, 0, 4, 5, 2, 1))
    return (y_prompt, y_sample, new_k, new_v, new_s)
```

```python
import functools
import math

import numpy as np
import jax
import jax.numpy as jnp
from jax import lax
from jax.experimental import pallas as pl
from jax.experimental.pallas import tpu as pltpu

F32 = jnp.float32
BF16 = jnp.bfloat16

D = 2048
DEPTH = 4
B_CTX, L_CTX = 32, 256
B_LAT, L_LAT = 4, 2048
N_CTX = B_CTX * L_CTX
N_LAT = B_LAT * L_LAT
N_TOK = N_CTX + N_LAT
H = 16
DH = 64
W = H * DH
GRID_W = 64
ROWS = L_LAT // GRID_W
WIN_R, WIN_C = 8, 16
NL = WIN_R * GRID_W
LORA = 64
LORA_G = 128
N_LORA = 512
N_A = 6 * W + 3 * D
HY_OFF = 3 * W
GL_OFF = 6 * W
N_T = 3 * W + N_LORA
D_FF = 4 * D
N_MOD = 6
NORM_EPS = 1e-6
GN_EPS = 64e-5
NEG_INF = -1e30
POS_BANDS = 16
FH_PAD = 128
LANES = 128
GB = LANES // (2 * H)
TCH = 128
WKV_RUN = 16
WKV_STAGE = 4
VMEM_LIMIT = 56 * 1024 * 1024


def _params(sem):
    return pltpu.CompilerParams(dimension_semantics=sem, vmem_limit_bytes=VMEM_LIMIT)


def _call(kernel, lidx, args, *, grid, in_specs, out_specs, out_shape, scratch=(), sem, name,
          aliases=None):
    gs = pltpu.PrefetchScalarGridSpec(num_scalar_prefetch=1, grid=grid, in_specs=in_specs,
                                      out_specs=out_specs, scratch_shapes=list(scratch))
    return pl.pallas_call(kernel, grid_spec=gs, out_shape=out_shape,
                          compiler_params=_params(sem), name=name,
                          input_output_aliases=aliases or {})(lidx, *args)


_ANY = pl.BlockSpec(memory_space=pl.ANY)


def _mod_row(i, tm):
    start = i * tm
    return jnp.where(start < N_CTX, 0, 1 + (start - N_CTX) // L_LAT)


def _modnorm(x, g, sc, sh):
    y = x * lax.rsqrt(jnp.mean(x * x, -1, keepdims=True) + NORM_EPS)
    return (y * g) * (1.0 + sc) + sh


def _dot(a, b):
    return jnp.dot(a, b, preferred_element_type=F32)


def _dot_nt(a, b):
    return lax.dot_general(a, b, (((1,), (1,)), ((), ())), preferred_element_type=F32)


def _dot_tn(a, b):
    return lax.dot_general(a, b, (((0,), (0,)), ((), ())), preferred_element_type=F32)


def _mod_kernel(c_ref, w_ref, b_ref, o_ref):
    c = c_ref[...]
    s = c * jax.nn.sigmoid(c)
    o_ref[...] = _dot(s.astype(BF16), w_ref[...].astype(BF16)) + b_ref[...]


def _modulation(cvec8, w_mod, b_mod):
    tn = 1024
    return pl.pallas_call(
        _mod_kernel,
        grid=(DEPTH, N_MOD * D // tn),
        in_specs=[pl.BlockSpec((8, D), lambda l, j: (0, 0)),
                  pl.BlockSpec((None, D, tn), lambda l, j: (l, 0, j)),
                  pl.BlockSpec((None, 1, tn), lambda l, j: (l, 0, j))],
        out_specs=pl.BlockSpec((None, 8, tn), lambda l, j: (l, 0, j)),
        out_shape=jax.ShapeDtypeStruct((DEPTH, 8, N_MOD * D), F32),
        compiler_params=_params(("parallel", "parallel")), name="modulation",
    )(cvec8, w_mod, b_mod.reshape(DEPTH, 1, N_MOD * D))


def _in_proj_kernel(l_ref, x_ref, g_ref, sh_ref, sc_ref, w_ref, o_ref, h_ref, *, tm):
    @pl.when(pl.program_id(1) == 0)
    def _():
        row = _mod_row(pl.program_id(0), tm)
        h = _modnorm(x_ref[...], g_ref[...], sc_ref[pl.ds(row, 1), :], sh_ref[pl.ds(row, 1), :])
        h_ref[...] = h.astype(BF16)

    o_ref[...] = _dot(h_ref[...], w_ref[...])


def _in_proj(lidx, x, ln_g, mod, w):
    tm, tn = 1024, 1024
    n_out = w.shape[2]
    return _call(
        functools.partial(_in_proj_kernel, tm=tm), lidx, (x, ln_g, mod, mod, w),
        grid=(N_TOK // tm, n_out // tn),
        in_specs=[pl.BlockSpec((tm, D), lambda i, j, l: (i, 0)),
                  pl.BlockSpec((None, 1, D), lambda i, j, l: (l[0], 0, 0)),
                  pl.BlockSpec((None, 8, D), lambda i, j, l: (l[0], 0, 0)),
                  pl.BlockSpec((None, 8, D), lambda i, j, l: (l[0], 0, 1)),
                  pl.BlockSpec((None, D, tn), lambda i, j, l: (l[0], 0, j))],
        out_specs=[pl.BlockSpec((tm, tn), lambda i, j, l: (i, j)),
                   pl.BlockSpec((tm, D), lambda i, j, l: (i, 0))],
        out_shape=[jax.ShapeDtypeStruct((N_TOK, n_out), F32), jax.ShapeDtypeStruct((N_TOK, D), BF16)],
        sem=("parallel", "arbitrary"), name="in_proj")


def _in_proj_t_kernel(l_ref, h_ref, w_ref, o_ref):
    o_ref[...] = _dot_nt(w_ref[...], h_ref[...])


def _in_proj_t(lidx, h, w):
    tm, tn = 1024, 896
    n_out = w.shape[1]
    return _call(
        _in_proj_t_kernel, lidx, (h, w),
        grid=(N_TOK // tm, n_out // tn),
        in_specs=[pl.BlockSpec((tm, D), lambda i, j, l: (i, 0)),
                  pl.BlockSpec((None, tn, D), lambda i, j, l: (l[0], j, 0))],
        out_specs=pl.BlockSpec((tn, tm), lambda i, j, l: (j, i)),
        out_shape=jax.ShapeDtypeStruct((n_out, N_TOK), F32),
        sem=("parallel", "parallel"), name="in_proj_t")


def _kv_out_kernel(l_ref, k_ref, v_ref, ka_ref, va_ref, ko_ref, vo_ref):
    ko_ref[...] = k_ref[...]
    vo_ref[...] = v_ref[...]


def _kv_out(lidx, proj, k_acc, v_acc):
    cur = pl.BlockSpec((L_CTX, W), lambda b, l: (b, 1))
    cur_v = pl.BlockSpec((L_CTX, W), lambda b, l: (b, 2))
    dst = pl.BlockSpec((None, None, L_CTX, W), lambda b, l: (b, l[0], 0, 0))
    shape = jax.ShapeDtypeStruct((B_CTX, DEPTH, L_CTX, W), F32)
    return _call(
        _kv_out_kernel, lidx, (proj, proj, k_acc, v_acc),
        grid=(B_CTX,), in_specs=[cur, cur_v, _ANY, _ANY],
        out_specs=[dst, dst], out_shape=[shape, shape],
        sem=("parallel",), name="kv_out", aliases={3: 0, 4: 1})


ATT_CH = 8


def _attn_ctx_kernel(l_ref, q_ref, k_ref, v_ref, o_ref, s_ref):
    scale = DH ** -0.5
    for hh in range(ATT_CH):
        sl = slice(hh * DH, (hh + 1) * DH)
        q = (q_ref[:, sl] * scale).astype(BF16)
        s_ref[hh] = _dot_nt(q, k_ref[:, sl].astype(BF16))
    outs = []
    for hh in range(ATT_CH):
        sl = slice(hh * DH, (hh + 1) * DH)
        s = s_ref[hh]
        p = jnp.exp(s - jnp.max(s, -1, keepdims=True))
        den = jnp.sum(p, -1, keepdims=True)
        outs.append(_dot(p.astype(BF16), v_ref[:, sl].astype(BF16)) / den)
    o_ref[...] = jnp.concatenate(outs, -1).astype(BF16)


def _attn_ctx(lidx, proj):
    wb = ATT_CH * DH
    nq = W // wb
    return _call(
        _attn_ctx_kernel, lidx, (proj, proj, proj),
        grid=(B_CTX, nq),
        in_specs=[pl.BlockSpec((L_CTX, wb), lambda b, p, l: (b, p)),
                  pl.BlockSpec((L_CTX, wb), lambda b, p, l: (b, nq + p)),
                  pl.BlockSpec((L_CTX, wb), lambda b, p, l: (b, 2 * nq + p))],
        out_specs=pl.BlockSpec((L_CTX, wb), lambda b, p, l: (b, p)),
        out_shape=jax.ShapeDtypeStruct((N_TOK, W), BF16),
        scratch=[pltpu.VMEM((ATT_CH, L_CTX, L_CTX), F32)],
        sem=("parallel", "parallel"), name="attn_ctx")


ATT_RB = 8


def _attn_lat_kernel(l_ref, q_ref, k_ref, v_ref, kc_ref, vc_ref, bias_ref, alias_ref, o_ref,
                     kb_ref, vb_ref, kcb_ref, vcb_ref, s_ref):
    scale = DH ** -0.5
    for hh in range(2):
        sl = slice(hh * DH, (hh + 1) * DH)
        kb_ref[hh] = k_ref[:, sl].astype(BF16)
        vb_ref[hh] = v_ref[:, sl].astype(BF16)
        kcb_ref[hh] = kc_ref[:, sl].astype(BF16)
        vcb_ref[hh] = vc_ref[:, sl].astype(BF16)

    def window(r):
        r0 = jnp.clip(r - WIN_R // 2, 0, ROWS - WIN_R)
        return r - r0, pl.multiple_of(r * GRID_W, GRID_W), pl.multiple_of(r0 * GRID_W, GRID_W)

    def row_block(rr, carry):
        for i in range(ATT_RB):
            d, qrow, krow = window(rr * ATT_RB + i)
            for hh in range(2):
                sl = slice(hh * DH, (hh + 1) * DH)
                q = (q_ref[pl.ds(qrow, GRID_W), sl] * scale).astype(BF16)
                s_ref[2 * i + hh, :, :NL] = _dot_nt(q, kb_ref[hh, pl.ds(krow, NL), :]) + bias_ref[hh, d]
                s_ref[2 * i + hh, :, NL:] = _dot_nt(q, kcb_ref[hh])
        for i in range(ATT_RB):
            d, qrow, krow = window(rr * ATT_RB + i)
            outs = []
            for hh in range(2):
                s = s_ref[2 * i + hh]
                p = jnp.exp(s - jnp.max(s, -1, keepdims=True))
                den = jnp.sum(p, -1, keepdims=True)
                pb = p.astype(BF16)
                o = _dot(pb[:, :NL], vb_ref[hh, pl.ds(krow, NL), :]) + _dot(pb[:, NL:], vcb_ref[hh])
                outs.append(o / den)
            o_ref[pl.ds(qrow, GRID_W), :] = jnp.concatenate(outs, -1).astype(BF16)
        return carry

    lax.fori_loop(0, ROWS // ATT_RB, row_block, 0)


def _attn_lat(lidx, proj, cache_k, cache_v, bias_tab, o_ctx):
    nq = W // LANES
    rb = N_CTX // L_LAT
    return _call(
        _attn_lat_kernel, lidx, (proj, proj, proj, cache_k, cache_v, bias_tab, o_ctx),
        grid=(B_LAT, nq),
        in_specs=[pl.BlockSpec((L_LAT, LANES), lambda b, p, l: (rb + b, p)),
                  pl.BlockSpec((L_LAT, LANES), lambda b, p, l: (rb + b, nq + p)),
                  pl.BlockSpec((L_LAT, LANES), lambda b, p, l: (rb + b, 2 * nq + p)),
                  pl.BlockSpec((None, None, L_CTX, LANES), lambda b, p, l: (b, l[0], 0, p)),
                  pl.BlockSpec((None, None, L_CTX, LANES), lambda b, p, l: (b, l[0], 0, p)),
                  pl.BlockSpec((None, 2, WIN_R, GRID_W, NL), lambda b, p, l: (l[0], p, 0, 0, 0)),
                  _ANY],
        out_specs=pl.BlockSpec((L_LAT, LANES), lambda b, p, l: (rb + b, p)),
        out_shape=jax.ShapeDtypeStruct((N_TOK, W), BF16),
        scratch=[pltpu.VMEM((2, L_LAT, DH), BF16), pltpu.VMEM((2, L_LAT, DH), BF16),
                 pltpu.VMEM((2, L_CTX, DH), BF16), pltpu.VMEM((2, L_CTX, DH), BF16),
                 pltpu.VMEM((2 * ATT_RB, GRID_W, NL + L_CTX), F32)],
        sem=("parallel", "parallel"), name="attn_lat", aliases={7: 0})


def _bias_table(rpb):
    cq = np.arange(GRID_W)
    c0 = np.clip(cq - WIN_C // 2, 0, GRID_W - WIN_C)
    ck = np.arange(GRID_W)
    col_ok = (ck[None, :] >= c0[:, None]) & (ck[None, :] < c0[:, None] + WIN_C)
    dc = np.clip(ck[None, :] - cq[:, None], -(WIN_C - 1), WIN_C - 1) + (WIN_C - 1)
    onehot = (dc.reshape(-1)[None, :] == np.arange(2 * WIN_C - 1)[:, None]).astype(np.float32)
    cols = jnp.einsum('lhrc,cx->lhrx', rpb, jnp.asarray(onehot), precision=lax.Precision.HIGHEST)
    cols = cols.reshape(DEPTH, H, 2 * WIN_R - 1, GRID_W, GRID_W)
    tab = jnp.stack([cols[:, :, WIN_R - 1 - d:2 * WIN_R - 1 - d] for d in range(WIN_R)], 2)
    tab = jnp.transpose(tab, (0, 1, 2, 4, 3, 5))
    tab = jnp.where(col_ok[None, None, None, :, None, :], tab, NEG_INF)
    return tab.reshape(DEPTH, H, WIN_R, GRID_W, NL).astype(F32)


def _conv3(x, w_ref, b_ref):
    n = x.shape[0]
    row = lax.broadcasted_iota(jnp.int32, x.shape, 0)
    prev = jnp.where(row == 0, 0.0, pltpu.roll(x, 1, 0))
    nxt = jnp.where(row == n - 1, 0.0, pltpu.roll(x, n - 1, 0))
    return prev * w_ref[0:1, :] + x * w_ref[1:2, :] + nxt * w_ref[2:3, :] + b_ref[...]


def _conv3_t(x, p):
    n = x.shape[1]
    lane = lax.broadcasted_iota(jnp.int32, x.shape, 1)
    prev = jnp.where(lane == 0, 0.0, pltpu.roll(x, 1, 1))
    nxt = jnp.where(lane == n - 1, 0.0, pltpu.roll(x, n - 1, 1))
    return prev * p[:, 0:1] + x * p[:, 1:2] + nxt * p[:, 2:3] + p[:, 3:4]


def _wkv_prep_kernel(l_ref, r_ref, k_ref, v_ref, lora_ref, cp_ref, rp_ref, w2_ref, a2_ref, g2_ref,
                     ro_ref, ko_ref, vo_ref, d0_ref, d1_ref, a0o_ref, a1o_ref, go_ref):
    ro_ref[...] = _conv3_t(r_ref[...], cp_ref[0])
    ko_ref[...] = _conv3_t(k_ref[...], cp_ref[1])
    vo_ref[...] = _conv3_t(v_ref[...], cp_ref[2])
    rp = rp_ref[...]
    for e, (d_ref, ao_ref) in enumerate(((d0_ref, a0o_ref), (d1_ref, a1o_ref))):
        lw = jnp.tanh(lora_ref[e * LORA:(e + 1) * LORA, :]).astype(BF16)
        la = lora_ref[(2 + e) * LORA:(3 + e) * LORA, :].astype(BF16)
        w_log = rp[:, e:e + 1] + _dot(w2_ref[e].astype(BF16), lw)
        d_ref[...] = jnp.exp(-math.exp(-0.5) * jax.nn.sigmoid(w_log))
        ao_ref[...] = jax.nn.sigmoid(rp[:, 2 + e:3 + e] + _dot(a2_ref[e].astype(BF16), la))
    lg = jax.nn.sigmoid(lora_ref[4 * LORA:4 * LORA + LORA_G, :]).astype(BF16)
    go_ref[...] = _dot(g2_ref[...].astype(BF16), lg)


def _wkv_prep(lidx, proj_t, p, *, nb, seq, rb):
    ct = 512 if seq <= 256 else 128
    nc = W // ct

    def row(c):
        return pl.BlockSpec((ct, seq), lambda b, j, l: (c * nc + j, rb + b))

    out = pl.BlockSpec((None, ct, seq), lambda b, j, l: (b, j, 0))
    return _call(
        _wkv_prep_kernel, lidx,
        (proj_t, proj_t, proj_t, proj_t, p['wkv_conv'], p['wkv_rowp'], p['wkv_w2t'], p['wkv_a2t'],
         p['wkv_g2t']),
        grid=(nb, nc),
        in_specs=[row(0), row(1), row(2),
                  pl.BlockSpec((N_LORA, seq), lambda b, j, l: (3 * W // N_LORA, rb + b)),
                  pl.BlockSpec((None, 3, ct, 4), lambda b, j, l: (l[0], 0, j, 0)),
                  pl.BlockSpec((None, ct, 4), lambda b, j, l: (l[0], j, 0)),
                  pl.BlockSpec((None, 2, ct, LORA), lambda b, j, l: (l[0], 0, j, 0)),
                  pl.BlockSpec((None, 2, ct, LORA), lambda b, j, l: (l[0], 0, j, 0)),
                  pl.BlockSpec((None, ct, LORA_G), lambda b, j, l: (l[0], j, 0))],
        out_specs=[out] * 8,
        out_shape=[jax.ShapeDtypeStruct((nb, W, seq), F32)] * 8,
        sem=("parallel", "parallel"), name=f"wkv_prep_{seq}")


def _to_chains_kernel(l_ref, *refs, shared):
    o_ref = refs[-1]
    xs = [x for x in refs[:-1] for _ in range(2)] if shared else refs[:-1]
    for k0 in range(0, DH, 8):
        tiles = []
        for k in range(k0, k0 + 8):
            m = jnp.concatenate([x[k * H:(k + 1) * H, :] for x in xs], 0)
            tiles.append(m.T)
        o_ref[:, k0:k0 + 8, :] = pltpu.einshape("ktc->tkc", jnp.stack(tiles, 0))


def _to_chains(lidx, x0, x1, *, nb, seq):
    def src(b):
        return pl.BlockSpec((None, W, TCH), lambda g, t, l: (GB * g + b, 0, t))

    shared = x1 is None
    if shared:
        args = (x0,) * GB
        in_specs = [src(b) for b in range(GB)]
    else:
        args = tuple(x0 if i % 2 == 0 else x1 for i in range(2 * GB))
        in_specs = [src(i // 2) for i in range(2 * GB)]
    return _call(
        functools.partial(_to_chains_kernel, shared=shared), lidx, args,
        grid=(nb // GB, seq // TCH),
        in_specs=in_specs,
        out_specs=pl.BlockSpec((TCH, DH, LANES), lambda g, t, l: (t, 0, g)),
        out_shape=jax.ShapeDtypeStruct((seq, DH, nb * 2 * H), F32),
        sem=("parallel", "parallel"), name=f"to_chains_{seq}")


def _wkv_kernel(l_ref, rf_ref, kf_ref, vf_ref, wf_ref, af_ref, rb_ref, kb_ref, vb_ref, wb_ref, ab_ref,
                kkp_ref, kap_ref, s0_ref, yf_ref, yb_ref, s_ref, st_ref, p_ref, *, tb):
    @pl.when(pl.program_id(1) == 0)
    def _():
        s_ref[...] = s0_ref[...]

    kkp = kkp_ref[...]
    kap = kap_ref[...]
    lane = lax.broadcasted_iota(jnp.int32, (DH, LANES), 1)
    bwd = (lane // H) % 2 == 1
    kc = 32
    zero = jnp.zeros((DH, LANES), F32)

    def stage(t, u, p_prev):
        tr = tb - 1 - t
        pick = lambda f_ref, b_ref: jnp.where(bwd, b_ref[tr], f_ref[t])
        kt = pick(kf_ref, kb_ref)
        at = pick(af_ref, ab_ref)
        kk = kt * kkp
        kk = kk * lax.rsqrt(jnp.sum(kk * kk, 0, keepdims=True) + 1e-12)
        p_new = p_prev * pick(wf_ref, wb_ref)
        p_inv = 1.0 / p_new
        st_ref[u, 0] = kk * p_prev
        st_ref[u, 1] = (kk * at) * p_inv
        st_ref[u, 2] = (kt * (1.0 + (at - 1.0) * kap)) * p_inv
        st_ref[u, 3] = pick(rf_ref, rb_ref) * p_new
        st_ref[u, 4] = pick(vf_ref, vb_ref)
        return p_new

    def step(t, u):
        def sa_body(c, acc):
            a0, a1 = acc
            for j in range(kc):
                k = c * kc + j
                term = s_ref[k] * st_ref[u, 0, pl.ds(k, 1), :]
                if j % 2 == 0:
                    a0 = a0 + term
                else:
                    a1 = a1 + term
            return a0, a1

        a0, a1 = lax.fori_loop(0, DH // kc, sa_body, (zero, zero))
        sa = -(a0 + a1)
        vt = st_ref[u, 4]

        def up_body(c, acc):
            y0, y1 = acc
            for j in range(kc):
                k = c * kc + j
                sk = s_ref[k] + (sa * st_ref[u, 1, pl.ds(k, 1), :] + vt * st_ref[u, 2, pl.ds(k, 1), :])
                s_ref[k] = sk
                term = sk * st_ref[u, 3, pl.ds(k, 1), :]
                if j % 2 == 0:
                    y0 = y0 + term
                else:
                    y1 = y1 + term
            return y0, y1

        y0, y1 = lax.fori_loop(0, DH // kc, up_body, (zero, zero))
        y = y0 + y1
        yf_ref[t] = y
        yb_ref[tb - 1 - t] = y

    def group(t0):
        p = p_ref[...]
        for u in range(WKV_STAGE):
            p = stage(t0 + u, u, p)
        p_ref[...] = p
        for u in range(WKV_STAGE):
            step(t0 + u, u)

    def rescale(c, carry):
        for j in range(kc):
            k = c * kc + j
            s_ref[k] = s_ref[k] * p_ref[pl.ds(k, 1), :]
        return carry

    def run(i, carry):
        p_ref[...] = jnp.ones((DH, LANES), F32)

        def groups(q, c):
            group(i * WKV_RUN + q * WKV_STAGE)
            return c

        lax.fori_loop(0, WKV_RUN // WKV_STAGE, groups, 0)
        lax.fori_loop(0, DH // kc, rescale, 0)
        return carry

    lax.fori_loop(0, tb // WKV_RUN, run, 0)


def _wkv(lidx, r, k, v, w, a, kkp, kap, s0, *, seq):
    tb = 32
    chains = r.shape[-1]
    nt = seq // tb
    f_spec = pl.BlockSpec((tb, DH, LANES), lambda g, t, l: (t, 0, g))
    b_spec = pl.BlockSpec((tb, DH, LANES), lambda g, t, l: (nt - 1 - t, 0, g))
    par_spec = pl.BlockSpec((None, DH, LANES), lambda g, t, l: (l[0], 0, g))
    st_spec = pl.BlockSpec((DH, DH, LANES), lambda g, t, l: (0, 0, g))
    y_shape = jax.ShapeDtypeStruct((seq, DH, chains), F32)
    return _call(
        functools.partial(_wkv_kernel, tb=tb), lidx, (r, k, v, w, a, r, k, v, w, a, kkp, kap, s0),
        grid=(chains // LANES, nt),
        in_specs=[f_spec] * 5 + [b_spec] * 5 + [par_spec, par_spec, st_spec],
        out_specs=[f_spec, b_spec, st_spec],
        out_shape=[y_shape, y_shape, jax.ShapeDtypeStruct((DH, DH, chains), F32)],
        scratch=[pltpu.VMEM((WKV_STAGE, 5, DH, LANES), F32), pltpu.VMEM((DH, LANES), F32)],
        sem=("parallel", "arbitrary"), name=f"wkv_{seq}")


def _from_chains_kernel(l_ref, yf_ref, yb_ref, o_ref):
    for v0 in range(0, DH, 8):
        f3 = pltpu.einshape("tvc->vtc", yf_ref[:, v0:v0 + 8, :])
        b3 = pltpu.einshape("tvc->vtc", yb_ref[:, v0:v0 + 8, :])
        for i in range(8):
            v = v0 + i
            tf = f3[i].T
            tb = b3[i].T
            for b in range(GB):
                lo = b * 2 * H
                o_ref[b, v * H:(v + 1) * H, :] = tf[lo:lo + H] + tb[lo + H:lo + 2 * H]


def _from_chains(lidx, yf, yb, *, nb, seq):
    y_spec = pl.BlockSpec((TCH, DH, LANES), lambda g, t, l: (t, 0, g))
    return _call(
        _from_chains_kernel, lidx, (yf, yb),
        grid=(nb // GB, seq // TCH),
        in_specs=[y_spec, y_spec],
        out_specs=pl.BlockSpec((GB, W, TCH), lambda g, t, l: (g, 0, t)),
        out_shape=jax.ShapeDtypeStruct((nb, W, seq), F32),
        sem=("parallel", "parallel"), name=f"from_chains_{seq}")


def _wkv_post_kernel(l_ref, y_ref, r_ref, k_ref, v_ref, g_ref, gate_ref, par_ref, w_ref, *rest):
    o_ref = rest[-1]
    t = y_ref.shape[-1]
    par = par_ref[...]
    y = y_ref[...].reshape(DH, H, t)
    mu = jnp.mean(y, 0, keepdims=True)
    yc = y - mu
    var = jnp.mean(yc * yc, 0, keepdims=True)
    yn = (yc * lax.rsqrt(var + GN_EPS)).reshape(W, t) * par[:, 1:2] + par[:, 2:3]
    rk = (r_ref[...] * k_ref[...] * par[:, 0:1]).reshape(DH, H, t)
    bonus = jnp.broadcast_to(jnp.sum(rk, 0, keepdims=True), (DH, H, t)).reshape(W, t) * v_ref[...]
    o = ((yn + bonus) * g_ref[...]).astype(BF16)
    o_ref[...] = jax.nn.sigmoid(gate_ref[...]) * _dot_tn(o, w_ref[...])


def _wkv_post(lidx, ysum, r, k, v, g, proj, p, prev, *, nb, seq, rb):
    tt = 256
    nt = seq // tt
    t_spec = pl.BlockSpec((None, W, tt), lambda b, t, l: (b, 0, t))
    tok = lambda b, t: rb + b * nt + t
    args = (ysum, r, k, v, g, proj, p['wkv_post'], p['w_pr'])
    in_specs = [t_spec] * 5 + [
        pl.BlockSpec((tt, D), lambda b, t, l: (tok(b, t), (GL_OFF + D) // D)),
        pl.BlockSpec((None, W, 4), lambda b, t, l: (l[0], 0, 0)),
        pl.BlockSpec((None, W, D), lambda b, t, l: (l[0], 0, 0))]
    aliases = None
    if prev is not None:
        args = args + (prev,)
        in_specs = in_specs + [_ANY]
        aliases = {len(args): 0}
    return _call(
        _wkv_post_kernel, lidx, args,
        grid=(nb, nt), in_specs=in_specs,
        out_specs=pl.BlockSpec((tt, D), lambda b, t, l: (tok(b, t), 0)),
        out_shape=jax.ShapeDtypeStruct((N_TOK, D), F32),
        sem=("parallel", "parallel"), name=f"wkv_post_{seq}", aliases=aliases)


def _head_param(p, reps):
    t = jnp.transpose(p.reshape(DEPTH, H, DH), (0, 2, 1))
    return jnp.tile(t, (1, 1, reps))


def _rwkv_branch(lidx, proj, proj_t, p, s0_chain, prev, *, nb, seq, rb_t, rb_tok):
    r, k, v, d0, d1, a0, a1, g = _wkv_prep(lidx, proj_t, p, nb=nb, seq=seq, rb=rb_t)
    tc = functools.partial(_to_chains, lidx, nb=nb, seq=seq)
    chains = nb * 2 * H
    yf, yb, s_fin = _wkv(lidx, tc(r, None), tc(k, None), tc(v, None), tc(d0, d1), tc(a0, a1),
                         p['kk_chain'][:, :, :chains], p['ka_chain'][:, :, :chains], s0_chain, seq=seq)
    ysum = _from_chains(lidx, yf, yb, nb=nb, seq=seq)
    m_r = _wkv_post(lidx, ysum, r, k, v, g, proj, p, prev, nb=nb, seq=seq, rb=rb_tok)
    return m_r, s_fin


def _pos_features(seq):
    t = np.linspace(0.0, 1.0, seq, dtype=np.float32)[:, None]
    w = 2.0 * np.pi * np.arange(seq, dtype=np.float32)[:, None] / seq
    f = np.linspace(1e-4, POS_BANDS - 1, POS_BANDS, dtype=np.float32)[None, :]
    z = np.concatenate([t, np.cos(f * w), -np.sin(f * w)], -1).astype(np.float32)
    zp = np.zeros((seq, FH_PAD), np.float32)
    zp[:, :z.shape[1]] = z
    dist = (np.abs(np.arange(seq) - seq // 2).astype(np.float32) / seq)[:, None]
    deltas = np.abs(np.linspace(math.log(1e-2) / 1.5, math.log(1e-2) / 0.3, W,
                                dtype=np.float32))[None, :]
    return zp, dist, deltas


def _filter_kernel(zp_ref, dist_ref, del_ref, f1_ref, b1_ref, f2_ref, b2_ref, fr_ref, f3_ref, o_ref,
                   t_ref):
    hi = lax.Precision.HIGHEST

    @pl.when(pl.program_id(1) == 0)
    def _():
        fr = fr_ref[...]
        t = jnp.sin(fr * (jnp.dot(zp_ref[...], f1_ref[...], precision=hi,
                                  preferred_element_type=F32) + b1_ref[...]))
        t_ref[...] = jnp.sin(fr * (jnp.dot(t, f2_ref[...], precision=hi,
                                           preferred_element_type=F32) + b2_ref[...]))

    filt = jnp.dot(t_ref[...], f3_ref[...], precision=hi, preferred_element_type=F32)
    filt = filt * jnp.exp(-dist_ref[...] * del_ref[...])
    o_ref[...] = filt / (jnp.sum(jnp.abs(filt), 0, keepdims=True) + 1e-6)


def _hyena_filter(p, seq):
    ct = 256
    zp, dist, deltas = _pos_features(seq)
    full = lambda shape: pl.BlockSpec((None,) + shape, lambda l, j: (l,) + (0,) * len(shape))
    return pl.pallas_call(
        _filter_kernel,
        grid=(DEPTH, W // ct),
        in_specs=[pl.BlockSpec((seq, FH_PAD), lambda l, j: (0, 0)),
                  pl.BlockSpec((seq, 1), lambda l, j: (0, 0)),
                  pl.BlockSpec((1, ct), lambda l, j: (0, j)),
                  full((FH_PAD, FH_PAD)), full((1, FH_PAD)), full((FH_PAD, FH_PAD)),
                  full((1, FH_PAD)), full((1, FH_PAD)),
                  pl.BlockSpec((None, FH_PAD, ct), lambda l, j: (l, 0, j))],
        out_specs=pl.BlockSpec((None, seq, ct), lambda l, j: (l, 0, j)),
        out_shape=jax.ShapeDtypeStruct((DEPTH, seq, W), F32),
        scratch_shapes=[pltpu.VMEM((seq, FH_PAD), F32)],
        compiler_params=_params(("parallel", "arbitrary")), name=f"hyena_filter_{seq}",
    )(jnp.asarray(zp), jnp.asarray(dist), jnp.asarray(deltas),
      p['hy_f1p'], p['hy_fb1p'], p['hy_f2p'], p['hy_fb2p'], p['hy_freqp'], p['hy_f3p'])


def _freq_block(seq):
    return min(seq, 256)


def _dft_mats(seq):
    n = 2 * seq
    fq = _freq_block(seq)
    k = jnp.arange(seq, dtype=jnp.int32)
    t = jnp.arange(seq, dtype=jnp.int32)
    ph = ((2 * k[:, None] + 1) * t[None, :]) % (2 * n)
    ang = ph.astype(F32) * np.float32(np.pi / n)
    fwd = jnp.stack([jnp.cos(ang).reshape(seq // fq, fq, seq),
                     jnp.sin(ang).reshape(seq // fq, fq, seq)], 1).reshape(2 * seq, seq)
    m = t + seq // 2
    ph2 = ((2 * k[None, :] + 1) * m[:, None]) % (2 * n)
    ang2 = ph2.astype(F32) * np.float32(np.pi / n)
    inv = jnp.stack([jnp.cos(ang2).reshape(seq, seq // fq, fq),
                     jnp.sin(ang2).reshape(seq, seq // fq, fq)], 2).reshape(seq, 2 * seq)
    inv = inv * np.float32(2.0 / n)
    return fwd.astype(BF16), inv.astype(BF16)


def _spectrum_kernel(a_ref, b_ref, o_ref):
    o_ref[...] = _dot(a_ref[...], b_ref[...].astype(BF16))


def _spectrum(fwd, filt, seq):
    tmm, tn = 512, 512
    return pl.pallas_call(
        _spectrum_kernel,
        grid=(DEPTH, 2 * seq // tmm, W // tn),
        in_specs=[pl.BlockSpec((tmm, seq), lambda l, i, j: (i, 0)),
                  pl.BlockSpec((None, seq, tn), lambda l, i, j: (l, 0, j))],
        out_specs=pl.BlockSpec((None, tmm, tn), lambda l, i, j: (l, i, j)),
        out_shape=jax.ShapeDtypeStruct((DEPTH, 2 * seq, W), F32),
        compiler_params=_params(("parallel", "parallel", "parallel")), name=f"hyena_spectrum_{seq}",
    )(fwd, filt)


def _hyena_kernel(l_ref, x0_ref, x1_ref, vv_ref, cw0_ref, cw1_ref, cw2_ref, cb0_ref, cb1_ref, cb2_ref,
                  fwd_ref, inv_ref, h_ref, d_ref, *rest):
    o_ref, z_ref, zb_ref, acc_ref = rest[-4:]
    fb = pl.program_id(2)
    fq = h_ref.shape[0] // 2

    @pl.when(fb == 0)
    def _():
        z = _conv3(vv_ref[...], cw2_ref, cb2_ref) * _conv3(x1_ref[...], cw1_ref, cb1_ref)
        z_ref[...] = z
        zb_ref[...] = z.astype(BF16)
        acc_ref[...] = jnp.zeros_like(acc_ref)

    zf = _dot(fwd_ref[...], zb_ref[...])
    zc, zs = zf[:fq], zf[fq:]
    hc, hs = h_ref[:fq, :], h_ref[fq:, :]
    pr = zc * hc - zs * hs
    pq = zc * hs + zs * hc
    acc_ref[...] += _dot(inv_ref[...], jnp.concatenate([pr, pq], 0).astype(BF16))

    @pl.when(fb == pl.num_programs(2) - 1)
    def _():
        y = acc_ref[...] + z_ref[...] * d_ref[...]
        o_ref[...] = (_conv3(x0_ref[...], cw0_ref, cb0_ref) * y).astype(BF16)


def _hyena(lidx, proj, p, mats, spec, prev, *, nb, seq, rb):
    ct = 512
    fq = _freq_block(seq)
    nc = W // ct
    off = HY_OFF // ct
    fwd_hi, inv_hi = mats

    def col(c):
        return pl.BlockSpec((seq, ct), lambda b, j, f, l: (rb + b, off + c * nc + j))

    def cw(c):
        return pl.BlockSpec((None, 3, ct), lambda b, j, f, l: (l[0], 0, c * nc + j))

    def cb(c):
        return pl.BlockSpec((None, 1, ct), lambda b, j, f, l: (l[0], 0, c * nc + j))

    args = (proj, proj, proj, p['hy_conv_w'], p['hy_conv_w'], p['hy_conv_w'],
            p['hy_conv_b'], p['hy_conv_b'], p['hy_conv_b'], fwd_hi, inv_hi, spec, p['hy_d'])
    in_specs = [col(0), col(1), col(2), cw(0), cw(1), cw(2), cb(0), cb(1), cb(2),
                pl.BlockSpec((2 * fq, seq), lambda b, j, f, l: (f, 0)),
                pl.BlockSpec((seq, 2 * fq), lambda b, j, f, l: (0, f)),
                pl.BlockSpec((None, 2 * fq, ct), lambda b, j, f, l: (l[0], f, j)),
                pl.BlockSpec((None, 1, ct), lambda b, j, f, l: (l[0], 0, j))]
    aliases = None
    if prev is not None:
        args = args + (prev,)
        in_specs = in_specs + [_ANY]
        aliases = {len(args): 0}
    return _call(
        _hyena_kernel, lidx, args,
        grid=(nb, nc, seq // fq), in_specs=in_specs,
        out_specs=pl.BlockSpec((seq, ct), lambda b, j, f, l: (rb + b, j)),
        out_shape=jax.ShapeDtypeStruct((N_TOK, W), BF16),
        scratch=[pltpu.VMEM((seq, ct), F32), pltpu.VMEM((seq, ct), BF16), pltpu.VMEM((seq, ct), F32)],
        sem=("parallel", "parallel", "arbitrary"), name=f"hyena_{seq}", aliases=aliases)


def _merge_kernel(l_ref, oa_ref, oc_ref, mr_ref, ga_ref, gc_ref, wa_ref, wc_ref, o_ref):
    m = (jax.nn.sigmoid(ga_ref[...]) * _dot(oa_ref[...], wa_ref[...]) + mr_ref[...]
         + jax.nn.sigmoid(gc_ref[...]) * _dot(oc_ref[...], wc_ref[...]))
    o_ref[...] = m.astype(BF16)


def _merge(lidx, o_a, o_c, m_r, proj, p):
    tm, tn = 1024, 512
    goff = GL_OFF // tn
    nd = D // tn
    row = lambda: pl.BlockSpec((tm, W), lambda i, j, l: (i, 0))
    gate = lambda c: pl.BlockSpec((tm, tn), lambda i, j, l: (i, goff + c * nd + j))
    wsp = lambda: pl.BlockSpec((None, W, tn), lambda i, j, l: (l[0], 0, j))
    return _call(
        _merge_kernel, lidx, (o_a, o_c, m_r, proj, proj, p['w_pa'], p['w_pc']),
        grid=(N_TOK // tm, nd),
        in_specs=[row(), row(), pl.BlockSpec((tm, tn), lambda i, j, l: (i, j)),
                  gate(0), gate(2), wsp(), wsp()],
        out_specs=pl.BlockSpec((tm, tn), lambda i, j, l: (i, j)),
        out_shape=jax.ShapeDtypeStruct((N_TOK, D), BF16),
        sem=("parallel", "parallel"), name="merge")


def _out_proj_kernel(l_ref, m_ref, w_ref, x_ref, ga_ref, g_ref, sh_ref, sc_ref, o_ref, h_ref, *, tm):
    row = _mod_row(pl.program_id(0), tm)
    xn = x_ref[...] + ga_ref[pl.ds(row, 1), :] * _dot(m_ref[...], w_ref[...])
    o_ref[...] = xn
    h_ref[...] = _modnorm(xn, g_ref[...], sc_ref[pl.ds(row, 1), :], sh_ref[pl.ds(row, 1), :]).astype(BF16)


def _out_proj(lidx, merged, x, mod, w_out, ln_g):
    tm = 512
    rows = pl.BlockSpec((tm, D), lambda i, l: (i, 0))
    chunk = lambda c: pl.BlockSpec((None, 8, D), lambda i, l: (l[0], 0, c))
    return _call(
        functools.partial(_out_proj_kernel, tm=tm), lidx, (merged, w_out, x, mod, ln_g, mod, mod),
        grid=(N_TOK // tm,),
        in_specs=[rows, pl.BlockSpec((None, D, D), lambda i, l: (l[0], 0, 0)), rows, chunk(2),
                  pl.BlockSpec((None, 1, D), lambda i, l: (l[0], 0, 0)), chunk(3), chunk(4)],
        out_specs=[rows, rows],
        out_shape=[jax.ShapeDtypeStruct((N_TOK, D), F32), jax.ShapeDtypeStruct((N_TOK, D), BF16)],
        sem=("parallel",), name="out_proj")


def _ffn_kernel(l_ref, x_ref, h_ref, ga_ref, w1_ref, b1_ref, w2_ref, b2_ref, o_ref, acc_ref, *, tm):
    j = pl.program_id(1)
    row = _mod_row(pl.program_id(0), tm)

    @pl.when(j == 0)
    def _():
        acc_ref[...] = jnp.zeros_like(acc_ref)

    a = _dot(h_ref[...], w1_ref[...]) + b1_ref[...]
    a = jnp.square(jnp.maximum(a, 0.0))
    acc_ref[...] += _dot(a.astype(BF16), w2_ref[...])

    @pl.when(j == pl.num_programs(1) - 1)
    def _():
        o_ref[...] = x_ref[...] + ga_ref[pl.ds(row, 1), :] * (acc_ref[...] + b2_ref[...])


def _ffn(lidx, x, h, mod, p):
    tm, tf = 512, 1024
    return _call(
        functools.partial(_ffn_kernel, tm=tm), lidx,
        (x, h, mod, p['w_ff1'], p['b_ff1'], p['w_ff2'], p['b_ff2']),
        grid=(N_TOK // tm, D_FF // tf),
        in_specs=[pl.BlockSpec((tm, D), lambda i, j, l: (i, 0)),
                  pl.BlockSpec((tm, D), lambda i, j, l: (i, 0)),
                  pl.BlockSpec((None, 8, D), lambda i, j, l: (l[0], 0, 5)),
                  pl.BlockSpec((None, D, tf), lambda i, j, l: (l[0], 0, j)),
                  pl.BlockSpec((None, 1, tf), lambda i, j, l: (l[0], 0, j)),
                  pl.BlockSpec((None, tf, D), lambda i, j, l: (l[0], j, 0)),
                  pl.BlockSpec((None, 1, D), lambda i, j, l: (l[0], 0, 0))],
        out_specs=pl.BlockSpec((tm, D), lambda i, j, l: (i, 0)),
        out_shape=jax.ShapeDtypeStruct((N_TOK, D), F32),
        scratch=[pltpu.VMEM((tm, D), F32)],
        sem=("parallel", "arbitrary"), name="ffn")


def _final_norm_kernel(x_ref, g_ref, o_ref):
    x = x_ref[...]
    o_ref[...] = x * lax.rsqrt(jnp.mean(x * x, -1, keepdims=True) + NORM_EPS) * g_ref[...]


def _final_norm(x, g, row0, nrows):
    tm = 1024
    rb = row0 // tm
    return pl.pallas_call(
        _final_norm_kernel,
        grid=(nrows // tm,),
        in_specs=[pl.BlockSpec((tm, D), lambda i: (rb + i, 0)), pl.BlockSpec((1, D), lambda i: (0, 0))],
        out_specs=pl.BlockSpec((tm, D), lambda i: (i, 0)),
        out_shape=jax.ShapeDtypeStruct((nrows, D), F32),
        compiler_params=_params(("parallel",)), name="final_norm",
    )(x, g.reshape(1, D))


def kernel(x_prompt, x_sample, cache_k, cache_v, state_wkv, c, c_ctx, ln1_g, ln2_g, w_mod, b_mod, w_in, rpb, wkv_conv_w, wkv_conv_b, wkv_w0, wkv_w1, wkv_w2, wkv_a0, wkv_a1, wkv_a2, wkv_g1, wkv_g2, wkv_k_k, wkv_k_a, wkv_r_k, wkv_gn_g, wkv_gn_b, hy_conv_w, hy_conv_b, hy_f1, hy_fb1, hy_f2, hy_fb2, hy_freq, hy_f3, hy_d, w_pa, w_pr, w_pc, w_out, w_ff1, b_ff1, w_ff2, b_ff2, final_g):
    x = jnp.concatenate([x_prompt.reshape(N_CTX, D), x_sample.reshape(N_LAT, D)], 0)
    cvec = jnp.zeros((8, D), F32).at[0].set(c_ctx).at[1:1 + B_LAT].set(c)
    mod = _modulation(cvec, w_mod, b_mod)

    perm = (np.arange(H)[None, :] * DH + np.arange(DH)[:, None]).reshape(-1)
    pad_c = lambda a, n: jnp.pad(a, [(0, 0)] * (a.ndim - 1) + [(0, n - a.shape[-1])])
    pad_r = lambda a, n: jnp.pad(a, [(0, 0)] * (a.ndim - 2) + [(0, n - a.shape[-2]), (0, 0)])
    conv = jnp.concatenate([wkv_conv_w, wkv_conv_b[:, None, :]], 1)
    conv = jnp.transpose(conv.reshape(DEPTH, 4, 3, W)[..., perm], (0, 2, 3, 1))
    rowp = jnp.stack([wkv_w0[:, 0], wkv_w0[:, 1], wkv_a0[:, 0], wkv_a0[:, 1]], -1)[:, perm]
    post = jnp.stack([wkv_r_k.reshape(DEPTH, W), wkv_gn_g, wkv_gn_b, jnp.zeros((DEPTH, W), F32)],
                     -1)[:, perm]
    p = {
        'wkv_conv': conv, 'wkv_rowp': rowp, 'wkv_post': post,
        'wkv_w2t': jnp.swapaxes(wkv_w2[..., perm], -1, -2),
        'wkv_a2t': jnp.swapaxes(wkv_a2[..., perm], -1, -2),
        'wkv_g2t': jnp.swapaxes(wkv_g2[..., perm], -1, -2),
        'kk_chain': _head_param(wkv_k_k, 2 * B_CTX), 'ka_chain': _head_param(wkv_k_a, 2 * B_CTX),
        'hy_conv_w': hy_conv_w, 'hy_conv_b': hy_conv_b.reshape(DEPTH, 1, 3 * W),
        'hy_d': hy_d.reshape(DEPTH, 1, W),
        'hy_f1p': pad_c(pad_r(hy_f1, FH_PAD), FH_PAD), 'hy_fb1p': pad_c(hy_fb1, FH_PAD).reshape(DEPTH, 1, FH_PAD),
        'hy_f2p': pad_c(pad_r(hy_f2, FH_PAD), FH_PAD), 'hy_fb2p': pad_c(hy_fb2, FH_PAD).reshape(DEPTH, 1, FH_PAD),
        'hy_freqp': pad_c(hy_freq, FH_PAD).reshape(DEPTH, 1, FH_PAD), 'hy_f3p': pad_r(hy_f3, FH_PAD),
        'w_pa': w_pa.astype(BF16), 'w_pr': w_pr[:, perm].astype(BF16), 'w_pc': w_pc.astype(BF16),
        'w_ff1': w_ff1.astype(BF16), 'b_ff1': b_ff1.reshape(DEPTH, 1, D_FF),
        'w_ff2': w_ff2.astype(BF16), 'b_ff2': b_ff2.reshape(DEPTH, 1, D),
    }
    w_a = jnp.concatenate([w_in[..., :3 * W], w_in[..., 6 * W:]], -1).astype(BF16)
    rkv = w_in[..., 3 * W:6 * W].reshape(DEPTH, D, 3, W)[..., perm].reshape(DEPTH, D, 3 * W)
    w_t = jnp.concatenate(
        [rkv, wkv_w1[:, 0], wkv_w1[:, 1], wkv_a1[:, 0], wkv_a1[:, 1], wkv_g1,
         jnp.zeros((DEPTH, D, N_LORA - 4 * LORA - LORA_G), F32)], -1)
    w_t = jnp.swapaxes(w_t, 1, 2).astype(BF16)
    w_out_b = w_out.astype(BF16)
    ln1 = ln1_g.reshape(DEPTH, 1, D)
    ln2 = ln2_g.reshape(DEPTH, 1, D)
    bias_tab = _bias_table(rpb)
    ck = cache_k.reshape(B_LAT, DEPTH, L_CTX, W)
    cv = cache_v.reshape(B_LAT, DEPTH, L_CTX, W)
    s0_lat = jnp.transpose(state_wkv, (1, 5, 4, 0, 2, 3)).reshape(DEPTH, DH, DH, B_LAT * 2 * H)
    s0_ctx = jnp.zeros((DH, DH, B_CTX * 2 * H), F32)

    hy = {}
    for seq in (L_CTX, L_LAT):
        mats = _dft_mats(seq)
        filt = _hyena_filter(p, seq)
        hy[seq] = (mats, _spectrum(mats[0], filt, seq))

    def layer(carry, l):
        x, k_acc, v_acc = carry
        lidx = jnp.reshape(l, (1,)).astype(jnp.int32)
        proj, h = _in_proj(lidx, x, ln1, mod, w_a)
        k_acc, v_acc = _kv_out(lidx, proj, k_acc, v_acc)
        proj_t = _in_proj_t(lidx, h, w_t)
        oa = _attn_lat(lidx, proj, ck, cv, bias_tab, _attn_ctx(lidx, proj))
        m_r, s_ctx = _rwkv_branch(lidx, proj, proj_t, p, s0_ctx, None, nb=B_CTX, seq=L_CTX,
                                  rb_t=0, rb_tok=0)
        s0 = lax.dynamic_index_in_dim(s0_lat, l, 0, keepdims=False)
        m_r, _ = _rwkv_branch(lidx, proj, proj_t, p, s0, m_r, nb=B_LAT, seq=L_LAT,
                              rb_t=N_CTX // L_LAT, rb_tok=N_CTX // 256)
        oc = _hyena(lidx, proj, p, *hy[L_CTX], None, nb=B_CTX, seq=L_CTX, rb=0)
        oc = _hyena(lidx, proj, p, *hy[L_LAT], oc, nb=B_LAT, seq=L_LAT, rb=N_CTX // L_LAT)
        merged = _merge(lidx, oa, oc, m_r, proj, p)
        x, h2 = _out_proj(lidx, merged, x, mod, w_out_b, ln2)
        x = _ffn(lidx, x, h2, mod, p)
        return (x, k_acc, v_acc), s_ctx

    kv0 = jnp.zeros((B_CTX, DEPTH, L_CTX, W), F32)
    (x, k_acc, v_acc), ss = lax.scan(layer, (x, kv0, kv0), jnp.arange(DEPTH, dtype=jnp.int32))
    y_prompt = _final_norm(x, final_g, 0, N_CTX).reshape(B_CTX, L_CTX, D)
    y_sample = _final_norm(x, final_g, N_CTX, N_LAT).reshape(B_LAT, L_LAT, D)
    new_k = k_acc.reshape(B_CTX, DEPTH, L_CTX, H, DH)
    new_v = v_acc.reshape(B_CTX, DEPTH, L_CTX, H, DH)
    new_s = jnp.transpose(ss.reshape(DEPTH, DH, DH, B_CTX, 2, H), (3, 0, 4, 5, 2, 1))
    return (y_prompt, y_sample, new_k, new_v, new_s)
```

```python
import functools
import math

import numpy as np
import jax
import jax.numpy as jnp
from jax import lax
from jax.experimental import pallas as pl
from jax.experimental.pallas import tpu as pltpu

F32 = jnp.float32
BF16 = jnp.bfloat16

D = 2048
DEPTH = 4
B_CTX, L_CTX = 32, 256
B_LAT, L_LAT = 4, 2048
N_CTX = B_CTX * L_CTX
N_LAT = B_LAT * L_LAT
N_TOK = N_CTX + N_LAT
H = 16
DH = 64
W = H * DH
GRID_W = 64
ROWS = L_LAT // GRID_W
WIN_R, WIN_C = 8, 16
NL = WIN_R * GRID_W
LORA = 64
LORA_G = 128
N_LORA = 512
N_A = 6 * W + 3 * D
HY_OFF = 3 * W
GL_OFF = 6 * W
N_T = 3 * W + N_LORA
D_FF = 4 * D
N_MOD = 6
NORM_EPS = 1e-6
GN_EPS = 64e-5
NEG_INF = -1e30
POS_BANDS = 16
FH_PAD = 128
LANES = 128
GB = LANES // (2 * H)
TCH = 128
WKV_RUN = 16
WKV_STAGE = 4
VMEM_LIMIT = 56 * 1024 * 1024


def _params(sem):
    return pltpu.CompilerParams(dimension_semantics=sem, vmem_limit_bytes=VMEM_LIMIT)


def _call(kernel, lidx, args, *, grid, in_specs, out_specs, out_shape, scratch=(), sem, name,
          aliases=None):
    gs = pltpu.PrefetchScalarGridSpec(num_scalar_prefetch=1, grid=grid, in_specs=in_specs,
                                      out_specs=out_specs, scratch_shapes=list(scratch))
    return pl.pallas_call(kernel, grid_spec=gs, out_shape=out_shape,
                          compiler_params=_params(sem), name=name,
                          input_output_aliases=aliases or {})(lidx, *args)


_ANY = pl.BlockSpec(memory_space=pl.ANY)


def _mod_row(i, tm):
    start = i * tm
    return jnp.where(start < N_CTX, 0, 1 + (start - N_CTX) // L_LAT)


def _modnorm(x, g, sc, sh):
    y = x * lax.rsqrt(jnp.mean(x * x, -1, keepdims=True) + NORM_EPS)
    return (y * g) * (1.0 + sc) + sh


def _dot(a, b):
    return jnp.dot(a, b, preferred_element_type=F32)


def _dot_nt(a, b):
    return lax.dot_general(a, b, (((1,), (1,)), ((), ())), preferred_element_type=F32)


def _dot_tn(a, b):
    return lax.dot_general(a, b, (((0,), (0,)), ((), ())), preferred_element_type=F32)


def _mod_kernel(c_ref, w_ref, b_ref, o_ref):
    c = c_ref[...]
    s = c * jax.nn.sigmoid(c)
    o_ref[...] = _dot(s.astype(BF16), w_ref[...].astype(BF16)) + b_ref[...]


def _modulation(cvec8, w_mod, b_mod):
    tn = 1024
    return pl.pallas_call(
        _mod_kernel,
        grid=(DEPTH, N_MOD * D // tn),
        in_specs=[pl.BlockSpec((8, D), lambda l, j: (0, 0)),
                  pl.BlockSpec((None, D, tn), lambda l, j: (l, 0, j)),
                  pl.BlockSpec((None, 1, tn), lambda l, j: (l, 0, j))],
        out_specs=pl.BlockSpec((None, 8, tn), lambda l, j: (l, 0, j)),
        out_shape=jax.ShapeDtypeStruct((DEPTH, 8, N_MOD * D), F32),
        compiler_params=_params(("parallel", "parallel")), name="modulation",
    )(cvec8, w_mod, b_mod.reshape(DEPTH, 1, N_MOD * D))


def _in_proj_kernel(l_ref, x_ref, g_ref, sh_ref, sc_ref, w_ref, o_ref, h_ref, *, tm):
    @pl.when(pl.program_id(1) == 0)
    def _():
        row = _mod_row(pl.program_id(0), tm)
        h = _modnorm(x_ref[...], g_ref[...], sc_ref[pl.ds(row, 1), :], sh_ref[pl.ds(row, 1), :])
        h_ref[...] = h.astype(BF16)

    o_ref[...] = _dot(h_ref[...], w_ref[...])


def _in_proj(lidx, x, ln_g, mod, w):
    tm, tn = 1024, 1024
    n_out = w.shape[2]
    return _call(
        functools.partial(_in_proj_kernel, tm=tm), lidx, (x, ln_g, mod, mod, w),
        grid=(N_TOK // tm, n_out // tn),
        in_specs=[pl.BlockSpec((tm, D), lambda i, j, l: (i, 0)),
                  pl.BlockSpec((None, 1, D), lambda i, j, l: (l[0], 0, 0)),
                  pl.BlockSpec((None, 8, D), lambda i, j, l: (l[0], 0, 0)),
                  pl.BlockSpec((None, 8, D), lambda i, j, l: (l[0], 0, 1)),
                  pl.BlockSpec((None, D, tn), lambda i, j, l: (l[0], 0, j))],
        out_specs=[pl.BlockSpec((tm, tn), lambda i, j, l: (i, j)),
                   pl.BlockSpec((tm, D), lambda i, j, l: (i, 0))],
        out_shape=[jax.ShapeDtypeStruct((N_TOK, n_out), F32), jax.ShapeDtypeStruct((N_TOK, D), BF16)],
        sem=("parallel", "arbitrary"), name="in_proj")


def _in_proj_t_kernel(l_ref, h_ref, w_ref, o_ref):
    o_ref[...] = _dot_nt(w_ref[...], h_ref[...])


def _in_proj_t(lidx, h, w):
    tm, tn = 1024, 1792
    n_out = w.shape[1]
    return _call(
        _in_proj_t_kernel, lidx, (h, w),
        grid=(N_TOK // tm, n_out // tn),
        in_specs=[pl.BlockSpec((tm, D), lambda i, j, l: (i, 0)),
                  pl.BlockSpec((None, tn, D), lambda i, j, l: (l[0], j, 0))],
        out_specs=pl.BlockSpec((tn, tm), lambda i, j, l: (j, i)),
        out_shape=jax.ShapeDtypeStruct((n_out, N_TOK), F32),
        sem=("parallel", "parallel"), name="in_proj_t")


def _kv_out_kernel(l_ref, k_ref, v_ref, ka_ref, va_ref, ko_ref, vo_ref):
    ko_ref[...] = k_ref[...]
    vo_ref[...] = v_ref[...]


def _kv_out(lidx, proj, k_acc, v_acc):
    cur = pl.BlockSpec((L_CTX, W), lambda b, l: (b, 1))
    cur_v = pl.BlockSpec((L_CTX, W), lambda b, l: (b, 2))
    dst = pl.BlockSpec((None, None, L_CTX, W), lambda b, l: (b, l[0], 0, 0))
    shape = jax.ShapeDtypeStruct((B_CTX, DEPTH, L_CTX, W), F32)
    return _call(
        _kv_out_kernel, lidx, (proj, proj, k_acc, v_acc),
        grid=(B_CTX,), in_specs=[cur, cur_v, _ANY, _ANY],
        out_specs=[dst, dst], out_shape=[shape, shape],
        sem=("parallel",), name="kv_out", aliases={3: 0, 4: 1})


ATT_CH = 8


def _attn_ctx_kernel(l_ref, q_ref, k_ref, v_ref, o_ref, s_ref):
    scale = DH ** -0.5
    for hh in range(ATT_CH):
        sl = slice(hh * DH, (hh + 1) * DH)
        q = (q_ref[:, sl] * scale).astype(BF16)
        s_ref[hh] = _dot_nt(q, k_ref[:, sl].astype(BF16))
    outs = []
    for hh in range(ATT_CH):
        sl = slice(hh * DH, (hh + 1) * DH)
        s = s_ref[hh]
        p = jnp.exp(s - jnp.max(s, -1, keepdims=True))
        den = jnp.sum(p, -1, keepdims=True)
        outs.append(_dot(p.astype(BF16), v_ref[:, sl].astype(BF16)) / den)
    o_ref[...] = jnp.concatenate(outs, -1).astype(BF16)


def _attn_ctx(lidx, proj):
    wb = ATT_CH * DH
    nq = W // wb
    return _call(
        _attn_ctx_kernel, lidx, (proj, proj, proj),
        grid=(B_CTX, nq),
        in_specs=[pl.BlockSpec((L_CTX, wb), lambda b, p, l: (b, p)),
                  pl.BlockSpec((L_CTX, wb), lambda b, p, l: (b, nq + p)),
                  pl.BlockSpec((L_CTX, wb), lambda b, p, l: (b, 2 * nq + p))],
        out_specs=pl.BlockSpec((L_CTX, wb), lambda b, p, l: (b, p)),
        out_shape=jax.ShapeDtypeStruct((N_TOK, W), BF16),
        scratch=[pltpu.VMEM((ATT_CH, L_CTX, L_CTX), F32)],
        sem=("parallel", "parallel"), name="attn_ctx")


ATT_RB = 8


def _attn_lat_kernel(l_ref, q_ref, k_ref, v_ref, kc_ref, vc_ref, bias_ref, alias_ref, o_ref,
                     kb_ref, vb_ref, kcb_ref, vcb_ref, s_ref):
    scale = DH ** -0.5
    for hh in range(2):
        sl = slice(hh * DH, (hh + 1) * DH)
        kb_ref[hh] = k_ref[:, sl].astype(BF16)
        vb_ref[hh] = v_ref[:, sl].astype(BF16)
        kcb_ref[hh] = kc_ref[:, sl].astype(BF16)
        vcb_ref[hh] = vc_ref[:, sl].astype(BF16)

    def window(r):
        r0 = jnp.clip(r - WIN_R // 2, 0, ROWS - WIN_R)
        return r - r0, pl.multiple_of(r * GRID_W, GRID_W), pl.multiple_of(r0 * GRID_W, GRID_W)

    def row_block(rr, carry):
        for i in range(ATT_RB):
            d, qrow, krow = window(rr * ATT_RB + i)
            for hh in range(2):
                sl = slice(hh * DH, (hh + 1) * DH)
                q = (q_ref[pl.ds(qrow, GRID_W), sl] * scale).astype(BF16)
                s_ref[2 * i + hh, :, :NL] = _dot_nt(q, kb_ref[hh, pl.ds(krow, NL), :]) + bias_ref[hh, d]
                s_ref[2 * i + hh, :, NL:] = _dot_nt(q, kcb_ref[hh])
        for i in range(ATT_RB):
            d, qrow, krow = window(rr * ATT_RB + i)
            outs = []
            for hh in range(2):
                s = s_ref[2 * i + hh]
                p = jnp.exp(s - jnp.max(s, -1, keepdims=True))
                den = jnp.sum(p, -1, keepdims=True)
                pb = p.astype(BF16)
                o = _dot(pb[:, :NL], vb_ref[hh, pl.ds(krow, NL), :]) + _dot(pb[:, NL:], vcb_ref[hh])
                outs.append(o / den)
            o_ref[pl.ds(qrow, GRID_W), :] = jnp.concatenate(outs, -1).astype(BF16)
        return carry

    lax.fori_loop(0, ROWS // ATT_RB, row_block, 0)


def _attn_lat(lidx, proj, cache_k, cache_v, bias_tab, o_ctx):
    nq = W // LANES
    rb = N_CTX // L_LAT
    return _call(
        _attn_lat_kernel, lidx, (proj, proj, proj, cache_k, cache_v, bias_tab, o_ctx),
        grid=(B_LAT, nq),
        in_specs=[pl.BlockSpec((L_LAT, LANES), lambda b, p, l: (rb + b, p)),
                  pl.BlockSpec((L_LAT, LANES), lambda b, p, l: (rb + b, nq + p)),
                  pl.BlockSpec((L_LAT, LANES), lambda b, p, l: (rb + b, 2 * nq + p)),
                  pl.BlockSpec((None, None, L_CTX, LANES), lambda b, p, l: (b, l[0], 0, p)),
                  pl.BlockSpec((None, None, L_CTX, LANES), lambda b, p, l: (b, l[0], 0, p)),
                  pl.BlockSpec((None, 2, WIN_R, GRID_W, NL), lambda b, p, l: (l[0], p, 0, 0, 0)),
                  _ANY],
        out_specs=pl.BlockSpec((L_LAT, LANES), lambda b, p, l: (rb + b, p)),
        out_shape=jax.ShapeDtypeStruct((N_TOK, W), BF16),
        scratch=[pltpu.VMEM((2, L_LAT, DH), BF16), pltpu.VMEM((2, L_LAT, DH), BF16),
                 pltpu.VMEM((2, L_CTX, DH), BF16), pltpu.VMEM((2, L_CTX, DH), BF16),
                 pltpu.VMEM((2 * ATT_RB, GRID_W, NL + L_CTX), F32)],
        sem=("parallel", "parallel"), name="attn_lat", aliases={7: 0})


def _bias_table(rpb):
    cq = np.arange(GRID_W)
    c0 = np.clip(cq - WIN_C // 2, 0, GRID_W - WIN_C)
    ck = np.arange(GRID_W)
    col_ok = (ck[None, :] >= c0[:, None]) & (ck[None, :] < c0[:, None] + WIN_C)
    dc = np.clip(ck[None, :] - cq[:, None], -(WIN_C - 1), WIN_C - 1) + (WIN_C - 1)
    onehot = (dc.reshape(-1)[None, :] == np.arange(2 * WIN_C - 1)[:, None]).astype(np.float32)
    cols = jnp.einsum('lhrc,cx->lhrx', rpb, jnp.asarray(onehot), precision=lax.Precision.HIGHEST)
    cols = cols.reshape(DEPTH, H, 2 * WIN_R - 1, GRID_W, GRID_W)
    tab = jnp.stack([cols[:, :, WIN_R - 1 - d:2 * WIN_R - 1 - d] for d in range(WIN_R)], 2)
    tab = jnp.transpose(tab, (0, 1, 2, 4, 3, 5))
    tab = jnp.where(col_ok[None, None, None, :, None, :], tab, NEG_INF)
    return tab.reshape(DEPTH, H, WIN_R, GRID_W, NL).astype(F32)


def _conv3(x, w_ref, b_ref):
    n = x.shape[0]
    row = lax.broadcasted_iota(jnp.int32, x.shape, 0)
    prev = jnp.where(row == 0, 0.0, pltpu.roll(x, 1, 0))
    nxt = jnp.where(row == n - 1, 0.0, pltpu.roll(x, n - 1, 0))
    return prev * w_ref[0:1, :] + x * w_ref[1:2, :] + nxt * w_ref[2:3, :] + b_ref[...]


def _conv3_t(x, p):
    n = x.shape[1]
    lane = lax.broadcasted_iota(jnp.int32, x.shape, 1)
    prev = jnp.where(lane == 0, 0.0, pltpu.roll(x, 1, 1))
    nxt = jnp.where(lane == n - 1, 0.0, pltpu.roll(x, n - 1, 1))
    return prev * p[:, 0:1] + x * p[:, 1:2] + nxt * p[:, 2:3] + p[:, 3:4]


def _wkv_prep_kernel(l_ref, r_ref, k_ref, v_ref, lora_ref, cp_ref, rp_ref, w2_ref, a2_ref, g2_ref,
                     ro_ref, ko_ref, vo_ref, d0_ref, d1_ref, a0o_ref, a1o_ref, go_ref):
    ro_ref[...] = _conv3_t(r_ref[...], cp_ref[0])
    ko_ref[...] = _conv3_t(k_ref[...], cp_ref[1])
    vo_ref[...] = _conv3_t(v_ref[...], cp_ref[2])
    rp = rp_ref[...]
    for e, (d_ref, ao_ref) in enumerate(((d0_ref, a0o_ref), (d1_ref, a1o_ref))):
        lw = jnp.tanh(lora_ref[e * LORA:(e + 1) * LORA, :]).astype(BF16)
        la = lora_ref[(2 + e) * LORA:(3 + e) * LORA, :].astype(BF16)
        w_log = rp[:, e:e + 1] + _dot(w2_ref[e].astype(BF16), lw)
        d_ref[...] = jnp.exp(-math.exp(-0.5) * jax.nn.sigmoid(w_log))
        ao_ref[...] = jax.nn.sigmoid(rp[:, 2 + e:3 + e] + _dot(a2_ref[e].astype(BF16), la))
    lg = jax.nn.sigmoid(lora_ref[4 * LORA:4 * LORA + LORA_G, :]).astype(BF16)
    go_ref[...] = _dot(g2_ref[...].astype(BF16), lg)


def _wkv_prep(lidx, proj_t, p, *, nb, seq, rb):
    ct = 512 if seq <= 256 else 128
    nc = W // ct

    def row(c):
        return pl.BlockSpec((ct, seq), lambda b, j, l: (c * nc + j, rb + b))

    out = pl.BlockSpec((None, ct, seq), lambda b, j, l: (b, j, 0))
    return _call(
        _wkv_prep_kernel, lidx,
        (proj_t, proj_t, proj_t, proj_t, p['wkv_conv'], p['wkv_rowp'], p['wkv_w2t'], p['wkv_a2t'],
         p['wkv_g2t']),
        grid=(nb, nc),
        in_specs=[row(0), row(1), row(2),
                  pl.BlockSpec((N_LORA, seq), lambda b, j, l: (3 * W // N_LORA, rb + b)),
                  pl.BlockSpec((None, 3, ct, 4), lambda b, j, l: (l[0], 0, j, 0)),
                  pl.BlockSpec((None, ct, 4), lambda b, j, l: (l[0], j, 0)),
                  pl.BlockSpec((None, 2, ct, LORA), lambda b, j, l: (l[0], 0, j, 0)),
                  pl.BlockSpec((None, 2, ct, LORA), lambda b, j, l: (l[0], 0, j, 0)),
                  pl.BlockSpec((None, ct, LORA_G), lambda b, j, l: (l[0], j, 0))],
        out_specs=[out] * 8,
        out_shape=[jax.ShapeDtypeStruct((nb, W, seq), F32)] * 8,
        sem=("parallel", "parallel"), name=f"wkv_prep_{seq}")


def _to_chains_kernel(l_ref, *refs, shared):
    o_ref = refs[-1]
    xs = [x for x in refs[:-1] for _ in range(2)] if shared else refs[:-1]
    for k0 in range(0, DH, 8):
        tiles = []
        for k in range(k0, k0 + 8):
            m = jnp.concatenate([x[k * H:(k + 1) * H, :] for x in xs], 0)
            tiles.append(m.T)
        o_ref[:, k0:k0 + 8, :] = pltpu.einshape("ktc->tkc", jnp.stack(tiles, 0))


def _to_chains(lidx, x0, x1, *, nb, seq):
    def src(b):
        return pl.BlockSpec((None, W, TCH), lambda g, t, l: (GB * g + b, 0, t))

    shared = x1 is None
    if shared:
        args = (x0,) * GB
        in_specs = [src(b) for b in range(GB)]
    else:
        args = tuple(x0 if i % 2 == 0 else x1 for i in range(2 * GB))
        in_specs = [src(i // 2) for i in range(2 * GB)]
    return _call(
        functools.partial(_to_chains_kernel, shared=shared), lidx, args,
        grid=(nb // GB, seq // TCH),
        in_specs=in_specs,
        out_specs=pl.BlockSpec((TCH, DH, LANES), lambda g, t, l: (t, 0, g)),
        out_shape=jax.ShapeDtypeStruct((seq, DH, nb * 2 * H), F32),
        sem=("parallel", "parallel"), name=f"to_chains_{seq}")


def _wkv_kernel(l_ref, rf_ref, kf_ref, vf_ref, wf_ref, af_ref, rb_ref, kb_ref, vb_ref, wb_ref, ab_ref,
                kkp_ref, kap_ref, s0_ref, yf_ref, yb_ref, s_ref, st_ref, p_ref, *, tb):
    @pl.when(pl.program_id(1) == 0)
    def _():
        s_ref[...] = s0_ref[...]

    kkp = kkp_ref[...]
    kap = kap_ref[...]
    lane = lax.broadcasted_iota(jnp.int32, (DH, LANES), 1)
    bwd = (lane // H) % 2 == 1
    kc = 32
    zero = jnp.zeros((DH, LANES), F32)

    def stage(t, u, p_prev):
        tr = tb - 1 - t
        pick = lambda f_ref, b_ref: jnp.where(bwd, b_ref[tr], f_ref[t])
        kt = pick(kf_ref, kb_ref)
        at = pick(af_ref, ab_ref)
        kk = kt * kkp
        kk = kk * lax.rsqrt(jnp.sum(kk * kk, 0, keepdims=True) + 1e-12)
        p_new = p_prev * pick(wf_ref, wb_ref)
        p_inv = 1.0 / p_new
        st_ref[u, 0] = kk * p_prev
        st_ref[u, 1] = (kk * at) * p_inv
        st_ref[u, 2] = (kt * (1.0 + (at - 1.0) * kap)) * p_inv
        st_ref[u, 3] = pick(rf_ref, rb_ref) * p_new
        st_ref[u, 4] = pick(vf_ref, vb_ref)
        return p_new

    def step(t, u):
        def sa_body(c, acc):
            a0, a1 = acc
            for j in range(kc):
                k = c * kc + j
                term = s_ref[k] * st_ref[u, 0, pl.ds(k, 1), :]
                if j % 2 == 0:
                    a0 = a0 + term
                else:
                    a1 = a1 + term
            return a0, a1

        a0, a1 = lax.fori_loop(0, DH // kc, sa_body, (zero, zero))
        sa = -(a0 + a1)
        vt = st_ref[u, 4]

        def up_body(c, acc):
            y0, y1 = acc
            for j in range(kc):
                k = c * kc + j
                sk = s_ref[k] + (sa * st_ref[u, 1, pl.ds(k, 1), :] + vt * st_ref[u, 2, pl.ds(k, 1), :])
                s_ref[k] = sk
                term = sk * st_ref[u, 3, pl.ds(k, 1), :]
                if j % 2 == 0:
                    y0 = y0 + term
                else:
                    y1 = y1 + term
            return y0, y1

        y0, y1 = lax.fori_loop(0, DH // kc, up_body, (zero, zero))
        y = y0 + y1
        yf_ref[t] = y
        yb_ref[tb - 1 - t] = y

    def group(t0):
        p = p_ref[...]
        for u in range(WKV_STAGE):
            p = stage(t0 + u, u, p)
        p_ref[...] = p
        for u in range(WKV_STAGE):
            step(t0 + u, u)

    def rescale(c, carry):
        for j in range(kc):
            k = c * kc + j
            s_ref[k] = s_ref[k] * p_ref[pl.ds(k, 1), :]
        return carry

    def run(i, carry):
        p_ref[...] = jnp.ones((DH, LANES), F32)

        def groups(q, c):
            group(i * WKV_RUN + q * WKV_STAGE)
            return c

        lax.fori_loop(0, WKV_RUN // WKV_STAGE, groups, 0)
        lax.fori_loop(0, DH // kc, rescale, 0)
        return carry

    lax.fori_loop(0, tb // WKV_RUN, run, 0)


def _wkv(lidx, r, k, v, w, a, kkp, kap, s0, *, seq):
    tb = 32
    chains = r.shape[-1]
    nt = seq // tb
    f_spec = pl.BlockSpec((tb, DH, LANES), lambda g, t, l: (t, 0, g))
    b_spec = pl.BlockSpec((tb, DH, LANES), lambda g, t, l: (nt - 1 - t, 0, g))
    par_spec = pl.BlockSpec((None, DH, LANES), lambda g, t, l: (l[0], 0, g))
    st_spec = pl.BlockSpec((DH, DH, LANES), lambda g, t, l: (0, 0, g))
    y_shape = jax.ShapeDtypeStruct((seq, DH, chains), F32)
    return _call(
        functools.partial(_wkv_kernel, tb=tb), lidx, (r, k, v, w, a, r, k, v, w, a, kkp, kap, s0),
        grid=(chains // LANES, nt),
        in_specs=[f_spec] * 5 + [b_spec] * 5 + [par_spec, par_spec, st_spec],
        out_specs=[f_spec, b_spec, st_spec],
        out_shape=[y_shape, y_shape, jax.ShapeDtypeStruct((DH, DH, chains), F32)],
        scratch=[pltpu.VMEM((WKV_STAGE, 5, DH, LANES), F32), pltpu.VMEM((DH, LANES), F32)],
        sem=("parallel", "arbitrary"), name=f"wkv_{seq}")


def _from_chains_kernel(l_ref, yf_ref, yb_ref, o_ref):
    for v0 in range(0, DH, 8):
        f3 = pltpu.einshape("tvc->vtc", yf_ref[:, v0:v0 + 8, :])
        b3 = pltpu.einshape("tvc->vtc", yb_ref[:, v0:v0 + 8, :])
        for i in range(8):
            v = v0 + i
            tf = f3[i].T
            tb = b3[i].T
            for b in range(GB):
                lo = b * 2 * H
                o_ref[b, v * H:(v + 1) * H, :] = tf[lo:lo + H] + tb[lo + H:lo + 2 * H]


def _from_chains(lidx, yf, yb, *, nb, seq):
    y_spec = pl.BlockSpec((TCH, DH, LANES), lambda g, t, l: (t, 0, g))
    return _call(
        _from_chains_kernel, lidx, (yf, yb),
        grid=(nb // GB, seq // TCH),
        in_specs=[y_spec, y_spec],
        out_specs=pl.BlockSpec((GB, W, TCH), lambda g, t, l: (g, 0, t)),
        out_shape=jax.ShapeDtypeStruct((nb, W, seq), F32),
        sem=("parallel", "parallel"), name=f"from_chains_{seq}")


def _wkv_post_kernel(l_ref, y_ref, r_ref, k_ref, v_ref, g_ref, gate_ref, par_ref, w_ref, *rest):
    o_ref = rest[-1]
    t = y_ref.shape[-1]
    par = par_ref[...]
    y = y_ref[...].reshape(DH, H, t)
    mu = jnp.mean(y, 0, keepdims=True)
    yc = y - mu
    var = jnp.mean(yc * yc, 0, keepdims=True)
    yn = (yc * lax.rsqrt(var + GN_EPS)).reshape(W, t) * par[:, 1:2] + par[:, 2:3]
    rk = (r_ref[...] * k_ref[...] * par[:, 0:1]).reshape(DH, H, t)
    bonus = jnp.broadcast_to(jnp.sum(rk, 0, keepdims=True), (DH, H, t)).reshape(W, t) * v_ref[...]
    o = ((yn + bonus) * g_ref[...]).astype(BF16)
    o_ref[...] = jax.nn.sigmoid(gate_ref[...]) * _dot_tn(o, w_ref[...])


def _wkv_post(lidx, ysum, r, k, v, g, proj, p, prev, *, nb, seq, rb):
    tt = 256
    nt = seq // tt
    t_spec = pl.BlockSpec((None, W, tt), lambda b, t, l: (b, 0, t))
    tok = lambda b, t: rb + b * nt + t
    args = (ysum, r, k, v, g, proj, p['wkv_post'], p['w_pr'])
    in_specs = [t_spec] * 5 + [
        pl.BlockSpec((tt, D), lambda b, t, l: (tok(b, t), (GL_OFF + D) // D)),
        pl.BlockSpec((None, W, 4), lambda b, t, l: (l[0], 0, 0)),
        pl.BlockSpec((None, W, D), lambda b, t, l: (l[0], 0, 0))]
    aliases = None
    if prev is not None:
        args = args + (prev,)
        in_specs = in_specs + [_ANY]
        aliases = {len(args): 0}
    return _call(
        _wkv_post_kernel, lidx, args,
        grid=(nb, nt), in_specs=in_specs,
        out_specs=pl.BlockSpec((tt, D), lambda b, t, l: (tok(b, t), 0)),
        out_shape=jax.ShapeDtypeStruct((N_TOK, D), F32),
        sem=("parallel", "parallel"), name=f"wkv_post_{seq}", aliases=aliases)


def _head_param(p, reps):
    t = jnp.transpose(p.reshape(DEPTH, H, DH), (0, 2, 1))
    return jnp.tile(t, (1, 1, reps))


def _rwkv_branch(lidx, proj, proj_t, p, s0_chain, prev, *, nb, seq, rb_t, rb_tok):
    r, k, v, d0, d1, a0, a1, g = _wkv_prep(lidx, proj_t, p, nb=nb, seq=seq, rb=rb_t)
    tc = functools.partial(_to_chains, lidx, nb=nb, seq=seq)
    chains = nb * 2 * H
    yf, yb, s_fin = _wkv(lidx, tc(r, None), tc(k, None), tc(v, None), tc(d0, d1), tc(a0, a1),
                         p['kk_chain'][:, :, :chains], p['ka_chain'][:, :, :chains], s0_chain, seq=seq)
    ysum = _from_chains(lidx, yf, yb, nb=nb, seq=seq)
    m_r = _wkv_post(lidx, ysum, r, k, v, g, proj, p, prev, nb=nb, seq=seq, rb=rb_tok)
    return m_r, s_fin


def _pos_features(seq):
    t = np.linspace(0.0, 1.0, seq, dtype=np.float32)[:, None]
    w = 2.0 * np.pi * np.arange(seq, dtype=np.float32)[:, None] / seq
    f = np.linspace(1e-4, POS_BANDS - 1, POS_BANDS, dtype=np.float32)[None, :]
    z = np.concatenate([t, np.cos(f * w), -np.sin(f * w)], -1).astype(np.float32)
    zp = np.zeros((seq, FH_PAD), np.float32)
    zp[:, :z.shape[1]] = z
    dist = (np.abs(np.arange(seq) - seq // 2).astype(np.float32) / seq)[:, None]
    deltas = np.abs(np.linspace(math.log(1e-2) / 1.5, math.log(1e-2) / 0.3, W,
                                dtype=np.float32))[None, :]
    return zp, dist, deltas


def _filter_kernel(zp_ref, dist_ref, del_ref, f1_ref, b1_ref, f2_ref, b2_ref, fr_ref, f3_ref, o_ref,
                   t_ref):
    hi = lax.Precision.HIGHEST

    @pl.when(pl.program_id(1) == 0)
    def _():
        fr = fr_ref[...]
        t = jnp.sin(fr * (jnp.dot(zp_ref[...], f1_ref[...], precision=hi,
                                  preferred_element_type=F32) + b1_ref[...]))
        t_ref[...] = jnp.sin(fr * (jnp.dot(t, f2_ref[...], precision=hi,
                                           preferred_element_type=F32) + b2_ref[...]))

    filt = jnp.dot(t_ref[...], f3_ref[...], precision=hi, preferred_element_type=F32)
    filt = filt * jnp.exp(-dist_ref[...] * del_ref[...])
    o_ref[...] = filt / (jnp.sum(jnp.abs(filt), 0, keepdims=True) + 1e-6)


def _hyena_filter(p, seq):
    ct = 256
    zp, dist, deltas = _pos_features(seq)
    full = lambda shape: pl.BlockSpec((None,) + shape, lambda l, j: (l,) + (0,) * len(shape))
    return pl.pallas_call(
        _filter_kernel,
        grid=(DEPTH, W // ct),
        in_specs=[pl.BlockSpec((seq, FH_PAD), lambda l, j: (0, 0)),
                  pl.BlockSpec((seq, 1), lambda l, j: (0, 0)),
                  pl.BlockSpec((1, ct), lambda l, j: (0, j)),
                  full((FH_PAD, FH_PAD)), full((1, FH_PAD)), full((FH_PAD, FH_PAD)),
                  full((1, FH_PAD)), full((1, FH_PAD)),
                  pl.BlockSpec((None, FH_PAD, ct), lambda l, j: (l, 0, j))],
        out_specs=pl.BlockSpec((None, seq, ct), lambda l, j: (l, 0, j)),
        out_shape=jax.ShapeDtypeStruct((DEPTH, seq, W), F32),
        scratch_shapes=[pltpu.VMEM((seq, FH_PAD), F32)],
        compiler_params=_params(("parallel", "arbitrary")), name=f"hyena_filter_{seq}",
    )(jnp.asarray(zp), jnp.asarray(dist), jnp.asarray(deltas),
      p['hy_f1p'], p['hy_fb1p'], p['hy_f2p'], p['hy_fb2p'], p['hy_freqp'], p['hy_f3p'])


def _freq_block(seq):
    return min(seq, 256)


def _dft_mats(seq):
    n = 2 * seq
    fq = _freq_block(seq)
    k = jnp.arange(seq, dtype=jnp.int32)
    t = jnp.arange(seq, dtype=jnp.int32)
    ph = ((2 * k[:, None] + 1) * t[None, :]) % (2 * n)
    ang = ph.astype(F32) * np.float32(np.pi / n)
    fwd = jnp.stack([jnp.cos(ang).reshape(seq // fq, fq, seq),
                     jnp.sin(ang).reshape(seq // fq, fq, seq)], 1).reshape(2 * seq, seq)
    m = t + seq // 2
    ph2 = ((2 * k[None, :] + 1) * m[:, None]) % (2 * n)
    ang2 = ph2.astype(F32) * np.float32(np.pi / n)
    inv = jnp.stack([jnp.cos(ang2).reshape(seq, seq // fq, fq),
                     jnp.sin(ang2).reshape(seq, seq // fq, fq)], 2).reshape(seq, 2 * seq)
    inv = inv * np.float32(2.0 / n)
    return fwd.astype(BF16), inv.astype(BF16)


def _spectrum_kernel(a_ref, b_ref, o_ref):
    o_ref[...] = _dot(a_ref[...], b_ref[...].astype(BF16))


def _spectrum(fwd, filt, seq):
    tmm, tn = 512, 512
    return pl.pallas_call(
        _spectrum_kernel,
        grid=(DEPTH, 2 * seq // tmm, W // tn),
        in_specs=[pl.BlockSpec((tmm, seq), lambda l, i, j: (i, 0)),
                  pl.BlockSpec((None, seq, tn), lambda l, i, j: (l, 0, j))],
        out_specs=pl.BlockSpec((None, tmm, tn), lambda l, i, j: (l, i, j)),
        out_shape=jax.ShapeDtypeStruct((DEPTH, 2 * seq, W), F32),
        compiler_params=_params(("parallel", "parallel", "parallel")), name=f"hyena_spectrum_{seq}",
    )(fwd, filt)


def _hyena_kernel(l_ref, x0_ref, x1_ref, vv_ref, cw0_ref, cw1_ref, cw2_ref, cb0_ref, cb1_ref, cb2_ref,
                  fwd_ref, inv_ref, h_ref, d_ref, *rest):
    o_ref, z_ref, zb_ref, acc_ref = rest[-4:]
    fb = pl.program_id(2)
    fq = h_ref.shape[0] // 2

    @pl.when(fb == 0)
    def _():
        z = _conv3(vv_ref[...], cw2_ref, cb2_ref) * _conv3(x1_ref[...], cw1_ref, cb1_ref)
        z_ref[...] = z
        zb_ref[...] = z.astype(BF16)
        acc_ref[...] = jnp.zeros_like(acc_ref)

    zf = _dot(fwd_ref[...], zb_ref[...])
    zc, zs = zf[:fq], zf[fq:]
    hc, hs = h_ref[:fq, :], h_ref[fq:, :]
    pr = zc * hc - zs * hs
    pq = zc * hs + zs * hc
    acc_ref[...] += _dot(inv_ref[...], jnp.concatenate([pr, pq], 0).astype(BF16))

    @pl.when(fb == pl.num_programs(2) - 1)
    def _():
        y = acc_ref[...] + z_ref[...] * d_ref[...]
        o_ref[...] = (_conv3(x0_ref[...], cw0_ref, cb0_ref) * y).astype(BF16)


def _hyena(lidx, proj, p, mats, spec, prev, *, nb, seq, rb):
    ct = 512
    fq = _freq_block(seq)
    nc = W // ct
    off = HY_OFF // ct
    fwd_hi, inv_hi = mats

    def col(c):
        return pl.BlockSpec((seq, ct), lambda b, j, f, l: (rb + b, off + c * nc + j))

    def cw(c):
        return pl.BlockSpec((None, 3, ct), lambda b, j, f, l: (l[0], 0, c * nc + j))

    def cb(c):
        return pl.BlockSpec((None, 1, ct), lambda b, j, f, l: (l[0], 0, c * nc + j))

    args = (proj, proj, proj, p['hy_conv_w'], p['hy_conv_w'], p['hy_conv_w'],
            p['hy_conv_b'], p['hy_conv_b'], p['hy_conv_b'], fwd_hi, inv_hi, spec, p['hy_d'])
    in_specs = [col(0), col(1), col(2), cw(0), cw(1), cw(2), cb(0), cb(1), cb(2),
                pl.BlockSpec((2 * fq, seq), lambda b, j, f, l: (f, 0)),
                pl.BlockSpec((seq, 2 * fq), lambda b, j, f, l: (0, f)),
                pl.BlockSpec((None, 2 * fq, ct), lambda b, j, f, l: (l[0], f, j)),
                pl.BlockSpec((None, 1, ct), lambda b, j, f, l: (l[0], 0, j))]
    aliases = None
    if prev is not None:
        args = args + (prev,)
        in_specs = in_specs + [_ANY]
        aliases = {len(args): 0}
    return _call(
        _hyena_kernel, lidx, args,
        grid=(nb, nc, seq // fq), in_specs=in_specs,
        out_specs=pl.BlockSpec((seq, ct), lambda b, j, f, l: (rb + b, j)),
        out_shape=jax.ShapeDtypeStruct((N_TOK, W), BF16),
        scratch=[pltpu.VMEM((seq, ct), F32), pltpu.VMEM((seq, ct), BF16), pltpu.VMEM((seq, ct), F32)],
        sem=("parallel", "parallel", "arbitrary"), name=f"hyena_{seq}", aliases=aliases)


def _merge_kernel(l_ref, oa_ref, oc_ref, mr_ref, ga_ref, gc_ref, wa_ref, wc_ref, o_ref):
    m = (jax.nn.sigmoid(ga_ref[...]) * _dot(oa_ref[...], wa_ref[...]) + mr_ref[...]
         + jax.nn.sigmoid(gc_ref[...]) * _dot(oc_ref[...], wc_ref[...]))
    o_ref[...] = m.astype(BF16)


def _merge(lidx, o_a, o_c, m_r, proj, p):
    tm, tn = 1024, 1024
    goff = GL_OFF // tn
    nd = D // tn
    row = lambda: pl.BlockSpec((tm, W), lambda i, j, l: (i, 0))
    gate = lambda c: pl.BlockSpec((tm, tn), lambda i, j, l: (i, goff + c * nd + j))
    wsp = lambda: pl.BlockSpec((None, W, tn), lambda i, j, l: (l[0], 0, j))
    return _call(
        _merge_kernel, lidx, (o_a, o_c, m_r, proj, proj, p['w_pa'], p['w_pc']),
        grid=(N_TOK // tm, nd),
        in_specs=[row(), row(), pl.BlockSpec((tm, tn), lambda i, j, l: (i, j)),
                  gate(0), gate(2), wsp(), wsp()],
        out_specs=pl.BlockSpec((tm, tn), lambda i, j, l: (i, j)),
        out_shape=jax.ShapeDtypeStruct((N_TOK, D), BF16),
        sem=("parallel", "parallel"), name="merge")


def _out_proj_kernel(l_ref, m_ref, w_ref, x_ref, ga_ref, g_ref, sh_ref, sc_ref, o_ref, h_ref, *, tm):
    row = _mod_row(pl.program_id(0), tm)
    xn = x_ref[...] + ga_ref[pl.ds(row, 1), :] * _dot(m_ref[...], w_ref[...])
    o_ref[...] = xn
    h_ref[...] = _modnorm(xn, g_ref[...], sc_ref[pl.ds(row, 1), :], sh_ref[pl.ds(row, 1), :]).astype(BF16)


def _out_proj(lidx, merged, x, mod, w_out, ln_g):
    tm = 512
    rows = pl.BlockSpec((tm, D), lambda i, l: (i, 0))
    chunk = lambda c: pl.BlockSpec((None, 8, D), lambda i, l: (l[0], 0, c))
    return _call(
        functools.partial(_out_proj_kernel, tm=tm), lidx, (merged, w_out, x, mod, ln_g, mod, mod),
        grid=(N_TOK // tm,),
        in_specs=[rows, pl.BlockSpec((None, D, D), lambda i, l: (l[0], 0, 0)), rows, chunk(2),
                  pl.BlockSpec((None, 1, D), lambda i, l: (l[0], 0, 0)), chunk(3), chunk(4)],
        out_specs=[rows, rows],
        out_shape=[jax.ShapeDtypeStruct((N_TOK, D), F32), jax.ShapeDtypeStruct((N_TOK, D), BF16)],
        sem=("parallel",), name="out_proj")


def _ffn_kernel(l_ref, x_ref, h_ref, ga_ref, w1_ref, b1_ref, w2_ref, b2_ref, o_ref, acc_ref, *, tm):
    j = pl.program_id(1)
    row = _mod_row(pl.program_id(0), tm)

    @pl.when(j == 0)
    def _():
        acc_ref[...] = jnp.zeros_like(acc_ref)

    a = _dot(h_ref[...], w1_ref[...]) + b1_ref[...]
    a = jnp.square(jnp.maximum(a, 0.0))
    acc_ref[...] += _dot(a.astype(BF16), w2_ref[...])

    @pl.when(j == pl.num_programs(1) - 1)
    def _():
        o_ref[...] = x_ref[...] + ga_ref[pl.ds(row, 1), :] * (acc_ref[...] + b2_ref[...])


def _ffn(lidx, x, h, mod, p):
    tm, tf = 512, 1024
    return _call(
        functools.partial(_ffn_kernel, tm=tm), lidx,
        (x, h, mod, p['w_ff1'], p['b_ff1'], p['w_ff2'], p['b_ff2']),
        grid=(N_TOK // tm, D_FF // tf),
        in_specs=[pl.BlockSpec((tm, D), lambda i, j, l: (i, 0)),
                  pl.BlockSpec((tm, D), lambda i, j, l: (i, 0)),
                  pl.BlockSpec((None, 8, D), lambda i, j, l: (l[0], 0, 5)),
                  pl.BlockSpec((None, D, tf), lambda i, j, l: (l[0], 0, j)),
                  pl.BlockSpec((None, 1, tf), lambda i, j, l: (l[0], 0, j)),
                  pl.BlockSpec((None, tf, D), lambda i, j, l: (l[0], j, 0)),
                  pl.BlockSpec((None, 1, D), lambda i, j, l: (l[0], 0, 0))],
        out_specs=pl.BlockSpec((tm, D), lambda i, j, l: (i, 0)),
        out_shape=jax.ShapeDtypeStruct((N_TOK, D), F32),
        scratch=[pltpu.VMEM((tm, D), F32)],
        sem=("parallel", "arbitrary"), name="ffn")


def _final_norm_kernel(x_ref, g_ref, o_ref):
    x = x_ref[...]
    o_ref[...] = x * lax.rsqrt(jnp.mean(x * x, -1, keepdims=True) + NORM_EPS) * g_ref[...]


def _final_norm(x, g, row0, nrows):
    tm = 1024
    rb = row0 // tm
    return pl.pallas_call(
        _final_norm_kernel,
        grid=(nrows // tm,),
        in_specs=[pl.BlockSpec((tm, D), lambda i: (rb + i, 0)), pl.BlockSpec((1, D), lambda i: (0, 0))],
        out_specs=pl.BlockSpec((tm, D), lambda i: (i, 0)),
        out_shape=jax.ShapeDtypeStruct((nrows, D), F32),
        compiler_params=_params(("parallel",)), name="final_norm",
    )(x, g.reshape(1, D))


def kernel(x_prompt, x_sample, cache_k, cache_v, state_wkv, c, c_ctx, ln1_g, ln2_g, w_mod, b_mod, w_in, rpb, wkv_conv_w, wkv_conv_b, wkv_w0, wkv_w1, wkv_w2, wkv_a0, wkv_a1, wkv_a2, wkv_g1, wkv_g2, wkv_k_k, wkv_k_a, wkv_r_k, wkv_gn_g, wkv_gn_b, hy_conv_w, hy_conv_b, hy_f1, hy_fb1, hy_f2, hy_fb2, hy_freq, hy_f3, hy_d, w_pa, w_pr, w_pc, w_out, w_ff1, b_ff1, w_ff2, b_ff2, final_g):
    x = jnp.concatenate([x_prompt.reshape(N_CTX, D), x_sample.reshape(N_LAT, D)], 0)
    cvec = jnp.zeros((8, D), F32).at[0].set(c_ctx).at[1:1 + B_LAT].set(c)
    mod = _modulation(cvec, w_mod, b_mod)

    perm = (np.arange(H)[None, :] * DH + np.arange(DH)[:, None]).reshape(-1)
    pad_c = lambda a, n: jnp.pad(a, [(0, 0)] * (a.ndim - 1) + [(0, n - a.shape[-1])])
    pad_r = lambda a, n: jnp.pad(a, [(0, 0)] * (a.ndim - 2) + [(0, n - a.shape[-2]), (0, 0)])
    conv = jnp.concatenate([wkv_conv_w, wkv_conv_b[:, None, :]], 1)
    conv = jnp.transpose(conv.reshape(DEPTH, 4, 3, W)[..., perm], (0, 2, 3, 1))
    rowp = jnp.stack([wkv_w0[:, 0], wkv_w0[:, 1], wkv_a0[:, 0], wkv_a0[:, 1]], -1)[:, perm]
    post = jnp.stack([wkv_r_k.reshape(DEPTH, W), wkv_gn_g, wkv_gn_b, jnp.zeros((DEPTH, W), F32)],
                     -1)[:, perm]
    p = {
        'wkv_conv': conv, 'wkv_rowp': rowp, 'wkv_post': post,
        'wkv_w2t': jnp.swapaxes(wkv_w2[..., perm], -1, -2),
        'wkv_a2t': jnp.swapaxes(wkv_a2[..., perm], -1, -2),
        'wkv_g2t': jnp.swapaxes(wkv_g2[..., perm], -1, -2),
        'kk_chain': _head_param(wkv_k_k, 2 * B_CTX), 'ka_chain': _head_param(wkv_k_a, 2 * B_CTX),
        'hy_conv_w': hy_conv_w, 'hy_conv_b': hy_conv_b.reshape(DEPTH, 1, 3 * W),
        'hy_d': hy_d.reshape(DEPTH, 1, W),
        'hy_f1p': pad_c(pad_r(hy_f1, FH_PAD), FH_PAD), 'hy_fb1p': pad_c(hy_fb1, FH_PAD).reshape(DEPTH, 1, FH_PAD),
        'hy_f2p': pad_c(pad_r(hy_f2, FH_PAD), FH_PAD), 'hy_fb2p': pad_c(hy_fb2, FH_PAD).reshape(DEPTH, 1, FH_PAD),
        'hy_freqp': pad_c(hy_freq, FH_PAD).reshape(DEPTH, 1, FH_PAD), 'hy_f3p': pad_r(hy_f3, FH_PAD),
        'w_pa': w_pa.astype(BF16), 'w_pr': w_pr[:, perm].astype(BF16), 'w_pc': w_pc.astype(BF16),
        'w_ff1': w_ff1.astype(BF16), 'b_ff1': b_ff1.reshape(DEPTH, 1, D_FF),
        'w_ff2': w_ff2.astype(BF16), 'b_ff2': b_ff2.reshape(DEPTH, 1, D),
    }
    w_a = jnp.concatenate([w_in[..., :3 * W], w_in[..., 6 * W:]], -1).astype(BF16)
    rkv = w_in[..., 3 * W:6 * W].reshape(DEPTH, D, 3, W)[..., perm].reshape(DEPTH, D, 3 * W)
    w_t = jnp.concatenate(
        [rkv, wkv_w1[:, 0], wkv_w1[:, 1], wkv_a1[:, 0], wkv_a1[:, 1], wkv_g1,
         jnp.zeros((DEPTH, D, N_LORA - 4 * LORA - LORA_G), F32)], -1)
    w_t = jnp.swapaxes(w_t, 1, 2).astype(BF16)
    w_out_b = w_out.astype(BF16)
    ln1 = ln1_g.reshape(DEPTH, 1, D)
    ln2 = ln2_g.reshape(DEPTH, 1, D)
    bias_tab = _bias_table(rpb)
    ck = cache_k.reshape(B_LAT, DEPTH, L_CTX, W)
    cv = cache_v.reshape(B_LAT, DEPTH, L_CTX, W)
    s0_lat = jnp.transpose(state_wkv, (1, 5, 4, 0, 2, 3)).reshape(DEPTH, DH, DH, B_LAT * 2 * H)
    s0_ctx = jnp.zeros((DH, DH, B_CTX * 2 * H), F32)

    hy = {}
    for seq in (L_CTX, L_LAT):
        mats = _dft_mats(seq)
        filt = _hyena_filter(p, seq)
        hy[seq] = (mats, _spectrum(mats[0], filt, seq))

    def layer(carry, l):
        x, k_acc, v_acc = carry
        lidx = jnp.reshape(l, (1,)).astype(jnp.int32)
        proj, h = _in_proj(lidx, x, ln1, mod, w_a)
        k_acc, v_acc = _kv_out(lidx, proj, k_acc, v_acc)
        proj_t = _in_proj_t(lidx, h, w_t)
        oa = _attn_lat(lidx, proj, ck, cv, bias_tab, _attn_ctx(lidx, proj))
        m_r, s_ctx = _rwkv_branch(lidx, proj, proj_t, p, s0_ctx, None, nb=B_CTX, seq=L_CTX,
                                  rb_t=0, rb_tok=0)
        s0 = lax.dynamic_index_in_dim(s0_lat, l, 0, keepdims=False)
        m_r, _ = _rwkv_branch(lidx, proj, proj_t, p, s0, m_r, nb=B_LAT, seq=L_LAT,
                              rb_t=N_CTX // L_LAT, rb_tok=N_CTX // 256)
        oc = _hyena(lidx, proj, p, *hy[L_CTX], None, nb=B_CTX, seq=L_CTX, rb=0)
        oc = _hyena(lidx, proj, p, *hy[L_LAT], oc, nb=B_LAT, seq=L_LAT, rb=N_CTX // L_LAT)
        merged = _merge(lidx, oa, oc, m_r, proj, p)
        x, h2 = _out_proj(lidx, merged, x, mod, w_out_b, ln2)
        x = _ffn(lidx, x, h2, mod, p)
        return (x, k_acc, v_acc), s_ctx

    kv0 = jnp.zeros((B_CTX, DEPTH, L_CTX, W), F32)
    (x, k_acc, v_acc), ss = lax.scan(layer, (x, kv0, kv0), jnp.arange(DEPTH, dtype=jnp.int32))
    y_prompt = _final_norm(x, final_g, 0, N_CTX).reshape(B_CTX, L_CTX, D)
    y_sample = _final_norm(x, final_g, N_CTX, N_LAT).reshape(B_LAT, L_LAT, D)
    new_k = k_acc.reshape(B_CTX, DEPTH, L_CTX, H, DH)
    new_v = v_acc.reshape(B_CTX, DEPTH, L_CTX, H, DH)
    new_s = jnp.transpose(ss.reshape(DEPTH, DH, DH, B_CTX, 2, H), (3, 0, 4, 5, 2, 1))
    return (y_prompt, y_sample, new_k, new_v, new_s)
```
